```python
import math
import jax, jax.numpy as jnp
from jax import lax
import numpy as np

D_MODEL = 1024
BATCH = 8
SEQ = 4096
DEPTH = 4

N_EVEN = (DEPTH + 1) // 2
N_ODD = DEPTH // 2
ALPHA = (2.0 * DEPTH) ** 0.25
BETA = (8.0 * DEPTH) ** -0.25
LN_EPS = 1e-5
NEG_INF = -1e30

D_S5 = D_MODEL // 2
S5_GROUP = 16
S5_GROUPS = D_S5 // S5_GROUP
S5_STATE = 64
DT_MIN = 0.001
DT_MAX = 0.1
D_CONV = D_MODEL // 2
CONV_WIDTH = 31
EVEN_IN = D_S5 + 2 * D_CONV
HEAD_DIM = 64
N_HEADS = D_MODEL // HEAD_DIM
N_KV_HEADS = 4
GQA = N_HEADS // N_KV_HEADS
CMP_LEN = 32
CMP_STRIDE = 16
SEL_LEN = 64
N_SELECT = 16
WINDOW = 512
Q_BLOCK = 64
FORCE_BONUS = 1e4
ROPE_THETA = 10000.0
Q_WIDTH = N_HEADS * HEAD_DIM
KV_WIDTH = N_KV_HEADS * HEAD_DIM
GATE_WIDTH = 3 * N_HEADS
ODD_IN = Q_WIDTH + 6 * KV_WIDTH + GATE_WIDTH
N_EXPERTS = 64
TOP_K = 8
D_EXPERT = D_MODEL // 4
D_SHARED = D_MODEL // 4
ROUTE_SCALE = 2.5
EXPERT_BLOCK = 128

kernel_name = "s5_conformer_nsa_moe_hybrid"


def layer_norm(x, g, b):
    xf = x.astype(jnp.float32)
    mu = xf.mean(-1, keepdims=True)
    var = jnp.square(xf - mu).mean(-1, keepdims=True)
    return ((xf - mu) * lax.rsqrt(var + LN_EPS) * g + b).astype(x.dtype)


def rope(t, pos):
    half = HEAD_DIM // 2
    inv = ROPE_THETA ** (-jnp.arange(half, dtype=jnp.float32) / half)
    ang = pos.astype(jnp.float32)[:, None] * inv[None, :]
    cos, sin = jnp.cos(ang)[None, :, None, :], jnp.sin(ang)[None, :, None, :]
    t1, t2 = t[..., :half], t[..., half:]
    return jnp.concatenate([t1 * cos - t2 * sin, t1 * sin + t2 * cos], axis=-1).astype(t.dtype)


def masked_softmax(s, mask):
    p = jax.nn.softmax(jnp.where(mask, s.astype(jnp.float32), NEG_INF), axis=-1)
    return jnp.where(mask, p, 0.0)


def _complex_affine_combine(e1, e2):
    a1r, a1i, b1r, b1i = e1
    a2r, a2i, b2r, b2i = e2
    return (a2r * a1r - a2i * a1i,
            a2r * a1i + a2i * a1r,
            a2r * b1r - a2i * b1i + b2r,
            a2r * b1i + a2i * b1r + b2i)


def s5_branch(u, lam_re, lam_im, log_dt, b_re, b_im, c_re, c_im, d_skip, w_glu, b_glu):
    f32 = jnp.float32
    bsz, seq, _ = u.shape
    u = u.astype(f32).reshape(bsz, seq, S5_GROUPS, S5_GROUP)
    lam_re, lam_im = lam_re.astype(f32), lam_im.astype(f32)
    dt = jnp.exp(log_dt.astype(f32))[:, None]
    decay = jnp.exp(lam_re * dt)
    a_re, a_im = decay * jnp.cos(lam_im * dt), decay * jnp.sin(lam_im * dt)
    den = lam_re ** 2 + lam_im ** 2
    f_re = ((a_re - 1.0) * lam_re + a_im * lam_im) / den
    f_im = (a_im * lam_re - (a_re - 1.0) * lam_im) / den
    b_re, b_im = b_re.astype(f32), b_im.astype(f32)
    bb_re = f_re[..., None] * b_re - f_im[..., None] * b_im
    bb_im = f_re[..., None] * b_im + f_im[..., None] * b_re
    bu_re = jnp.einsum('blgh,gph->blgp', u, bb_re)
    bu_im = jnp.einsum('blgh,gph->blgp', u, bb_im)
    shp = bu_re.shape
    _, _, x_re, x_im = lax.associative_scan(
        _complex_affine_combine,
        (jnp.broadcast_to(a_re, shp), jnp.broadcast_to(a_im, shp), bu_re, bu_im), axis=1)
    y = (jnp.einsum('blgp,ghp->blgh', x_re, c_re.astype(f32))
         - jnp.einsum('blgp,ghp->blgh', x_im, c_im.astype(f32))
         + d_skip.astype(f32) * u)
    y = jax.nn.gelu(y.reshape(bsz, seq, D_S5))
    return y * jax.nn.sigmoid(y @ w_glu.astype(f32) + b_glu.astype(f32))


def conv_branch(v, cv_w, cv_b, ln_g, ln_b):
    val, gate = jnp.split(v, 2, axis=-1)
    h = val * jax.nn.sigmoid(gate)
    h = lax.conv_general_dilated(h, cv_w[:, None, :].astype(h.dtype), (1,), [(CONV_WIDTH - 1, 0)],
                                 dimension_numbers=('NWC', 'WIO', 'NWC'),
                                 feature_group_count=D_CONV) + cv_b
    return jax.nn.silu(layer_norm(h, ln_g, ln_b))


def even_mixer(x, w_in, b_in, lam_re, lam_im, log_dt, b_re, b_im, c_re, c_im, d_skip,
               w_glu, b_glu, cv_w, cv_b, cv_ln_g, cv_ln_b, w_out):
    h = x @ w_in + b_in
    y_a = s5_branch(h[..., :D_S5], lam_re, lam_im, log_dt, b_re, b_im, c_re, c_im,
                    d_skip, w_glu, b_glu).astype(x.dtype)
    y_b = conv_branch(h[..., D_S5:], cv_w, cv_b, cv_ln_g, cv_ln_b)
    return jnp.concatenate([y_a, y_b], axis=-1) @ w_out


def compress(t, w, pe):
    seq = t.shape[1]
    n_cmp = (seq - CMP_LEN) // CMP_STRIDE + 1
    idx = np.arange(n_cmp)[:, None] * CMP_STRIDE + np.arange(CMP_LEN)[None, :]
    tb = t[:, idx]
    return jnp.einsum('bnlhd,lde->bnhe', tb + pe[:, None, :], w)


def nsa_mixer(x, w_in, cmp_w_k, cmp_w_v, cmp_pe_k, cmp_pe_v, w_out):
    bsz, seq, _ = x.shape
    n_q = seq // Q_BLOCK
    n_cmp = (seq - CMP_LEN) // CMP_STRIDE + 1
    n_sb = seq // SEL_LEN
    n_sel = min(N_SELECT, n_sb)
    pos = jnp.arange(seq)
    sizes = [Q_WIDTH] + [KV_WIDTH] * 6 + [GATE_WIDTH]
    q, kc, vc, ks, vs, kw, vw, gl = jnp.split(x @ w_in, np.cumsum(sizes)[:-1].tolist(), axis=-1)
    kv = lambda t: t.reshape(bsz, seq, N_KV_HEADS, HEAD_DIM)
    q = rope(q.reshape(bsz, seq, N_HEADS, HEAD_DIM), pos) * (HEAD_DIM ** -0.5)
    kc, ks, kw = rope(kv(kc), pos), rope(kv(ks), pos), rope(kv(kw), pos)
    vc, vs, vw = kv(vc), kv(vs), kv(vw)
    gates = jax.nn.sigmoid(gl.astype(jnp.float32)).astype(x.dtype)

    k_cmp = compress(kc, cmp_w_k, cmp_pe_k)
    v_cmp = compress(vc, cmp_w_v, cmp_pe_v)
    cmp_end = jnp.arange(n_cmp) * CMP_STRIDE + CMP_LEN - 1
    c_start = np.arange(n_cmp) * CMP_STRIDE
    s_start = np.arange(n_sb) * SEL_LEN
    overlap = jnp.asarray(((c_start[:, None] < s_start[None, :] + SEL_LEN)
                           & (c_start[:, None] + CMP_LEN > s_start[None, :])).astype(np.float32))
    k_sb = ks.reshape(bsz, n_sb, SEL_LEN, N_KV_HEADS, HEAD_DIM).transpose(0, 3, 1, 2, 4)
    v_sb = vs.reshape(bsz, n_sb, SEL_LEN, N_KV_HEADS, HEAD_DIM).transpose(0, 3, 1, 2, 4)
    k_wp = jnp.pad(kw, ((0, 0), (WINDOW, 0), (0, 0), (0, 0)))
    v_wp = jnp.pad(vw, ((0, 0), (WINDOW, 0), (0, 0), (0, 0)))
    b_idx = jnp.arange(bsz)[:, None, None, None]
    h_idx = jnp.arange(N_KV_HEADS)[None, :, None, None]
    blk = jnp.arange(n_sb)
    sel_off = jnp.arange(SEL_LEN)
    win_off = jnp.arange(WINDOW + Q_BLOCK)

    def query_block(args):
        qb, gb, q0 = args
        t = q0 + jnp.arange(Q_BLOCK)
        p_c = masked_softmax(jnp.einsum('bqhgd,bnhd->bhgqn', qb, k_cmp),
                             cmp_end[None, :] <= t[:, None])
        o_c = jnp.einsum('bhgqn,bnhd->bqhgd', p_c.astype(qb.dtype), v_cmp)
        imp = jnp.einsum('bhgqn,ns->bhqs', p_c, overlap)
        cur = t // SEL_LEN
        allowed = blk[None, :] <= cur[:, None]
        forced = (blk[None, :] == 0) | (blk[None, :] == cur[:, None]) | (blk[None, :] == cur[:, None] - 1)
        score = jnp.where(allowed, imp + FORCE_BONUS * forced, -1.0)
        top, idx = lax.top_k(score, n_sel)
        k_sel = k_sb[b_idx, h_idx, idx].reshape(bsz, N_KV_HEADS, Q_BLOCK, n_sel * SEL_LEN, HEAD_DIM)
        v_sel = v_sb[b_idx, h_idx, idx].reshape(bsz, N_KV_HEADS, Q_BLOCK, n_sel * SEL_LEN, HEAD_DIM)
        kpos = (idx[..., None] * SEL_LEN + sel_off).reshape(bsz, N_KV_HEADS, Q_BLOCK, n_sel * SEL_LEN)
        m_s = jnp.repeat(top >= 0, SEL_LEN, axis=-1) & (kpos <= t[:, None])
        p_s = masked_softmax(jnp.einsum('bqhgd,bhqkd->bhgqk', qb, k_sel), m_s[:, :, None])
        o_s = jnp.einsum('bhgqk,bhqkd->bqhgd', p_s.astype(qb.dtype), v_sel)
        k_w = lax.dynamic_slice_in_dim(k_wp, q0, WINDOW + Q_BLOCK, axis=1)
        v_w = lax.dynamic_slice_in_dim(v_wp, q0, WINDOW + Q_BLOCK, axis=1)
        wpos = q0 - WINDOW + win_off
        dist = t[:, None] - wpos[None, :]
        m_w = (wpos[None, :] >= 0) & (dist >= 0) & (dist < WINDOW)
        p_w = masked_softmax(jnp.einsum('bqhgd,bkhd->bhgqk', qb, k_w), m_w)
        o_w = jnp.einsum('bhgqk,bkhd->bqhgd', p_w.astype(qb.dtype), v_w)
        return gb[..., 0:1] * o_c + gb[..., 1:2] * o_s + gb[..., 2:3] * o_w

    q_blocks = q.reshape(bsz, n_q, Q_BLOCK, N_KV_HEADS, GQA, HEAD_DIM).swapaxes(0, 1)
    g_blocks = gates.reshape(bsz, n_q, Q_BLOCK, N_KV_HEADS, GQA, 3).swapaxes(0, 1)
    starts = jnp.arange(n_q, dtype=jnp.int32) * Q_BLOCK
    o = lax.map(query_block, (q_blocks, g_blocks, starts))
    return o.swapaxes(0, 1).reshape(bsz, seq, Q_WIDTH) @ w_out


def moe(x, router_w, router_b, w_gate, w_up, w_down, s_gate, s_up, s_down):
    bsz, seq, d = x.shape
    n = bsz * seq
    xt = x.reshape(n, d)
    scores = jax.nn.sigmoid(xt.astype(jnp.float32) @ router_w.astype(jnp.float32))
    _, idx = lax.top_k(scores + router_b.astype(jnp.float32), TOP_K)
    gate = jnp.take_along_axis(scores, idx, axis=1)
    gate = gate / gate.sum(-1, keepdims=True) * ROUTE_SCALE
    n_assign = n * TOP_K
    flat_e = idx.reshape(-1)
    flat_tok = jnp.arange(n_assign, dtype=jnp.int32) // TOP_K
    order = jnp.argsort(flat_e)
    e_sorted = flat_e[order]
    counts = jnp.bincount(flat_e, length=N_EXPERTS)
    padded = (counts + EXPERT_BLOCK - 1) // EXPERT_BLOCK * EXPERT_BLOCK
    pad_end = jnp.cumsum(padded)
    pad_start = pad_end - padded
    start = jnp.cumsum(counts) - counts
    dest = pad_start[e_sorted] + jnp.arange(n_assign) - start[e_sorted]
    n_rows = -(-n_assign // EXPERT_BLOCK) * EXPERT_BLOCK + N_EXPERTS * EXPERT_BLOCK
    n_blk = n_rows // EXPERT_BLOCK
    row_tok = jnp.full((n_rows,), n, jnp.int32).at[dest].set(flat_tok[order])
    row_gate = jnp.zeros((n_rows,), jnp.float32).at[dest].set(gate.reshape(-1)[order]).astype(x.dtype)
    blk_exp = jnp.minimum(jnp.searchsorted(pad_end, jnp.arange(n_blk) * EXPERT_BLOCK, side='right'),
                          N_EXPERTS - 1)
    x_pad = jnp.concatenate([xt, jnp.zeros((1, d), xt.dtype)], axis=0)

    def expert_rows(args):
        tok, g, e = args
        xb = x_pad[tok]
        h = jax.nn.silu(xb @ w_gate[e]) * (xb @ w_up[e])
        return (h @ w_down[e]) * g[:, None]

    y_rows = lax.map(expert_rows, (row_tok.reshape(n_blk, EXPERT_BLOCK),
                                   row_gate.reshape(n_blk, EXPERT_BLOCK), blk_exp))
    routed = jnp.zeros((n + 1, d), x.dtype).at[row_tok].add(y_rows.reshape(n_rows, d))[:n]
    shared = (jax.nn.silu(xt @ s_gate) * (xt @ s_up)) @ s_down
    return (routed + shared).reshape(bsz, seq, d)


def setup_inputs(seed: int = 0) -> dict:
    key = jax.random.key(seed)
    keys = iter(jax.random.split(key, 48))
    f32 = jnp.float32
    nrm = lambda shape, scale: jax.random.normal(next(keys), shape, f32) * scale
    NE, NO, G, P, HC = N_EVEN, N_ODD, S5_GROUPS, S5_STATE, S5_GROUP
    d_mix = D_S5 + D_CONV
    return {
        "x": nrm((BATCH, SEQ, D_MODEL), 1.0),
        "ev_w_in": nrm((NE, D_MODEL, EVEN_IN), D_MODEL ** -0.5),
        "ev_b_in": nrm((NE, EVEN_IN), 0.02),
        "s5_lam_re": -0.5 + nrm((NE, G, P), 0.01),
        "s5_lam_im": math.pi * jnp.arange(P, dtype=f32) + nrm((NE, G, P), 0.01),
        "s5_log_dt": jax.random.uniform(next(keys), (NE, G), f32, math.log(DT_MIN), math.log(DT_MAX)),
        "s5_b_re": nrm((NE, G, P, HC), (2.0 * HC) ** -0.5),
        "s5_b_im": nrm((NE, G, P, HC), (2.0 * HC) ** -0.5),
        "s5_c_re": nrm((NE, G, HC, P), P ** -0.5),
        "s5_c_im": nrm((NE, G, HC, P), P ** -0.5),
        "s5_d": nrm((NE, G, HC), 1.0),
        "s5_w_glu": nrm((NE, D_S5, D_S5), D_S5 ** -0.5),
        "s5_b_glu": nrm((NE, D_S5), 0.02),
        "cv_w": nrm((NE, CONV_WIDTH, D_CONV), CONV_WIDTH ** -0.5),
        "cv_b": nrm((NE, D_CONV), 0.02),
        "cv_ln_g": 1.0 + nrm((NE, D_CONV), 0.02),
        "cv_ln_b": nrm((NE, D_CONV), 0.02),
        "ev_w_out": nrm((NE, d_mix, D_MODEL), d_mix ** -0.5 * BETA),
        "od_w_in": nrm((NO, D_MODEL, ODD_IN), D_MODEL ** -0.5),
        "cmp_w_k": nrm((NO, CMP_LEN, HEAD_DIM, HEAD_DIM), (CMP_LEN * HEAD_DIM) ** -0.5),
        "cmp_w_v": nrm((NO, CMP_LEN, HEAD_DIM, HEAD_DIM), (CMP_LEN * HEAD_DIM) ** -0.5),
        "cmp_pe_k": nrm((NO, CMP_LEN, HEAD_DIM), 0.1),
        "cmp_pe_v": nrm((NO, CMP_LEN, HEAD_DIM), 0.1),
        "od_w_out": nrm((NO, Q_WIDTH, D_MODEL), Q_WIDTH ** -0.5 * BETA),
        "ln1_g": 1.0 + nrm((DEPTH, D_MODEL), 0.02),
        "ln1_b": nrm((DEPTH, D_MODEL), 0.02),
        "ln2_g": 1.0 + nrm((DEPTH, D_MODEL), 0.02),
        "ln2_b": nrm((DEPTH, D_MODEL), 0.02),
        "router_w": nrm((DEPTH, D_MODEL, N_EXPERTS), D_MODEL ** -0.5),
        "router_b": nrm((DEPTH, N_EXPERTS), 0.01),
        "ex_w_gate": nrm((DEPTH, N_EXPERTS, D_MODEL, D_EXPERT), D_MODEL ** -0.5),
        "ex_w_up": nrm((DEPTH, N_EXPERTS, D_MODEL, D_EXPERT), D_MODEL ** -0.5),
        "ex_w_down": nrm((DEPTH, N_EXPERTS, D_EXPERT, D_MODEL), D_EXPERT ** -0.5 * BETA),
        "sh_w_gate": nrm((DEPTH, D_MODEL, D_SHARED), D_MODEL ** -0.5),
        "sh_w_up": nrm((DEPTH, D_MODEL, D_SHARED), D_MODEL ** -0.5),
        "sh_w_down": nrm((DEPTH, D_SHARED, D_MODEL), D_SHARED ** -0.5 * BETA),
    }


def reference(x, ev_w_in, ev_b_in, s5_lam_re, s5_lam_im, s5_log_dt, s5_b_re, s5_b_im, s5_c_re,
              s5_c_im, s5_d, s5_w_glu, s5_b_glu, cv_w, cv_b, cv_ln_g, cv_ln_b, ev_w_out,
              od_w_in, cmp_w_k, cmp_w_v, cmp_pe_k, cmp_pe_v, od_w_out,
              ln1_g, ln1_b, ln2_g, ln2_b, router_w, router_b, ex_w_gate, ex_w_up, ex_w_down,
              sh_w_gate, sh_w_up, sh_w_down):
    h = x
    for layer in range(DEPTH):
        i = layer // 2
        if layer % 2 == 0:
            mix = even_mixer(h, ev_w_in[i], ev_b_in[i], s5_lam_re[i], s5_lam_im[i], s5_log_dt[i],
                             s5_b_re[i], s5_b_im[i], s5_c_re[i], s5_c_im[i], s5_d[i],
                             s5_w_glu[i], s5_b_glu[i], cv_w[i], cv_b[i], cv_ln_g[i], cv_ln_b[i],
                             ev_w_out[i])
        else:
            mix = nsa_mixer(h, od_w_in[i], cmp_w_k[i], cmp_w_v[i], cmp_pe_k[i], cmp_pe_v[i],
                            od_w_out[i])
        h = layer_norm(ALPHA * h + mix, ln1_g[layer], ln1_b[layer])
        ffn = moe(h, router_w[layer], router_b[layer], ex_w_gate[layer], ex_w_up[layer],
                  ex_w_down[layer], sh_w_gate[layer], sh_w_up[layer], sh_w_down[layer])
        h = layer_norm(ALPHA * h + ffn, ln2_g[layer], ln2_b[layer])
    return h
```

```python
import functools
import math

import numpy as np
import jax
import jax.numpy as jnp
from jax import lax
from jax.experimental import pallas as pl
from jax.experimental.pallas import tpu as pltpu

D_MODEL = 1024
DEPTH = 4
ALPHA = (2.0 * DEPTH) ** 0.25
LN_EPS = 1e-5
NEG_INF = -1e30

D_S5 = 512
S5_GROUP = 16
S5_GROUPS = 32
S5_STATE = 64
S5_LANES = S5_GROUPS * S5_STATE
D_CONV = 512
CONV_WIDTH = 31
EVEN_IN = D_S5 + 2 * D_CONV

HEAD_DIM = 64
N_HEADS = 16
N_KV_HEADS = 4
GQA = 4
CMP_LEN = 32
CMP_STRIDE = 16
SEL_LEN = 64
N_SELECT = 16
WINDOW = 512
FORCE_BONUS = 1e4
ROPE_THETA = 10000.0
Q_WIDTH = N_HEADS * HEAD_DIM
KV_WIDTH = N_KV_HEADS * HEAD_DIM
GATE_WIDTH = 3 * N_HEADS

N_EXPERTS = 64
TOP_K = 8
D_EXPERT = 256
ROUTE_SCALE = 2.5

LANE = 128
_MM = jnp.bfloat16
_VMEM_LIMIT = 56 * 1024 * 1024

ROW_TILE = 256
SCAN_ROWS = 128
SEQ_TILE = 256
CONV_HALO = 32
Q_TILE = 128
KEY_CHUNK = 512
EXPERT_BLOCK = 256


def _dot(a, b):
    return jnp.dot(a, b, preferred_element_type=jnp.float32)


def _dot_nt(a, b):
    return lax.dot_general(a, b, (((1,), (1,)), ((), ())), preferred_element_type=jnp.float32)


def _layer_norm(x, g, b):
    mu = jnp.mean(x, axis=-1, keepdims=True)
    xc = x - mu
    var = jnp.mean(xc * xc, axis=-1, keepdims=True)
    return xc * lax.rsqrt(var + LN_EPS) * g + b


def _params(*sem):
    return pltpu.CompilerParams(dimension_semantics=sem, vmem_limit_bytes=_VMEM_LIMIT)


def _row_call(body, row_ins, const_ins, outs, name, tm=ROW_TILE):
    m = row_ins[0].shape[0]
    tm = min(tm, m)
    assert m % tm == 0
    in_specs = [pl.BlockSpec((tm, a.shape[1]), lambda i: (i, 0)) for a in row_ins]
    in_specs += [pl.BlockSpec(a.shape, functools.partial(lambda nd, i: (0,) * nd, a.ndim)) for a in const_ins]
    out_specs = [pl.BlockSpec((tm, c), lambda i: (i, 0)) for c, _ in outs]
    out_shape = [jax.ShapeDtypeStruct((m, c), dt) for c, dt in outs]
    return pl.pallas_call(body, grid=(m // tm,), in_specs=in_specs, out_specs=out_specs,
                          out_shape=out_shape, compiler_params=_params("parallel"), name=name)(
                              *row_ins, *const_ins)


def _proj_bias_body(x_ref, w_ref, b_ref, o_ref):
    o_ref[...] = _dot(x_ref[...].astype(_MM), w_ref[...]) + b_ref[...]


def _proj_res_ln_body(y_ref, res_ref, w_ref, g_ref, b_ref, o_ref):
    mix = _dot(y_ref[...].astype(_MM), w_ref[...])
    o_ref[...] = _layer_norm(ALPHA * res_ref[...] + mix, g_ref[...], b_ref[...])


def _s5_tables(lam_re, lam_im, log_dt, b_re, b_im, c_re, c_im):
    f32 = jnp.float32
    dt = jnp.exp(log_dt.astype(f32))[:, None]
    decay = jnp.exp(lam_re * dt)
    a_re, a_im = decay * jnp.cos(lam_im * dt), decay * jnp.sin(lam_im * dt)
    den = lam_re ** 2 + lam_im ** 2
    f_re = ((a_re - 1.0) * lam_re + a_im * lam_im) / den
    f_im = (a_im * lam_re - (a_re - 1.0) * lam_im) / den
    bb_re = f_re[..., None] * b_re - f_im[..., None] * b_im
    bb_im = f_re[..., None] * b_im + f_im[..., None] * b_re

    gl = LANE // S5_GROUP
    nj = S5_GROUPS // gl
    eye = jnp.eye(gl, dtype=f32)

    def in_blocks(bb):
        t = bb.reshape(nj, gl, S5_STATE, S5_GROUP)
        t = jnp.einsum('jgph,gk->jghkp', t, eye)
        return t.reshape(nj, gl * S5_GROUP, gl * S5_STATE).astype(_MM)

    def out_blocks(c):
        t = c.reshape(nj, gl, S5_GROUP, S5_STATE)
        t = jnp.einsum('jghp,gk->jgpkh', t, eye)
        return t.reshape(nj, gl * S5_STATE, gl * S5_GROUP).astype(_MM)

    ar, ai = a_re.reshape(1, S5_LANES), a_im.reshape(1, S5_LANES)
    pows_r, pows_i = [ar], [ai]
    for _ in range(int(math.log2(SCAN_ROWS)) - 1):
        pr, pi = pows_r[-1], pows_i[-1]
        pows_r.append(pr * pr - pi * pi)
        pows_i.append(2.0 * pr * pi)
    tr, ti = ar, ai
    for k in range(int(math.log2(SCAN_ROWS))):
        pr, pi = pows_r[k], pows_i[k]
        tr, ti = (jnp.concatenate([tr, tr * pr - ti * pi], axis=0),
                  jnp.concatenate([ti, tr * pi + ti * pr], axis=0))
    nlb = S5_LANES // LANE
    to3 = lambda t: t.reshape(t.shape[0], nlb, LANE).transpose(1, 0, 2)
    return (in_blocks(bb_re), in_blocks(bb_im), out_blocks(c_re), out_blocks(c_im),
            to3(jnp.concatenate(pows_r, axis=0)), to3(jnp.concatenate(pows_i, axis=0)), to3(tr), to3(ti))


def _s5conv_body(h_ref, bre_ref, bim_ref, cre_ref, cim_ref, pwr_ref, pwi_ref, tbr_ref, tbi_ref,
                 d_ref, wglu_ref, bglu_ref, cvw_ref, cvb_ref, lng_ref, lnb_ref, o_ref,
                 st_re, st_im, xr_ref, xi_ref, hbuf):
    tile = h_ref.shape[0]
    n_steps = int(math.log2(SCAN_ROWS))
    nlb = S5_LANES // LANE
    nj = bre_ref.shape[0]
    per_j = nlb // nj

    @pl.when(pl.program_id(1) == 0)
    def _():
        st_re[...] = jnp.zeros_like(st_re)
        st_im[...] = jnp.zeros_like(st_im)
        hbuf[0:CONV_HALO, :] = jnp.zeros((CONV_HALO, D_CONV), jnp.float32)

    row = lax.broadcasted_iota(jnp.int32, (SCAN_ROWS, LANE), 0)

    for c in range(tile // SCAN_ROWS):
        r0 = c * SCAN_ROWS
        u = h_ref[r0:r0 + SCAN_ROWS, 0:D_S5]
        ub = u.astype(_MM)
        for j in range(nj):
            uj = ub[:, j * LANE:(j + 1) * LANE]
            br = _dot(uj, bre_ref[j])
            bi = _dot(uj, bim_ref[j])
            for q in range(per_j):
                xr_ref[j * per_j + q] = br[:, q * LANE:(q + 1) * LANE]
                xi_ref[j * per_j + q] = bi[:, q * LANE:(q + 1) * LANE]

        def scan_block(lb, carry):
            xr, xi = xr_ref[lb], xi_ref[lb]
            for k in range(n_steps):
                d = 1 << k
                ar = pwr_ref[lb, k:k + 1, :]
                ai = pwi_ref[lb, k:k + 1, :]
                keep = row >= d
                sr = jnp.where(keep, pltpu.roll(xr, d, 0), 0.0)
                si = jnp.where(keep, pltpu.roll(xi, d, 0), 0.0)
                xr, xi = xr + ar * sr - ai * si, xi + ar * si + ai * sr
            pr, pi = st_re[lb, 0:1, :], st_im[lb, 0:1, :]
            tr, ti = tbr_ref[lb], tbi_ref[lb]
            xr, xi = xr + tr * pr - ti * pi, xi + tr * pi + ti * pr
            xr_ref[lb] = xr
            xi_ref[lb] = xi
            st_re[lb, 0:1, :] = xr[SCAN_ROWS - 1:SCAN_ROWS, :]
            st_im[lb, 0:1, :] = xi[SCAN_ROWS - 1:SCAN_ROWS, :]
            return carry

        lax.fori_loop(0, nlb, scan_block, 0)

        ys = []
        for j in range(nj):
            xr = jnp.concatenate([xr_ref[j * per_j + q] for q in range(per_j)], axis=1).astype(_MM)
            xi = jnp.concatenate([xi_ref[j * per_j + q] for q in range(per_j)], axis=1).astype(_MM)
            ys.append(_dot(xr, cre_ref[j]) - _dot(xi, cim_ref[j]))
        y = jnp.concatenate(ys, axis=1) + d_ref[...] * u
        y = jax.nn.gelu(y)
        y = y * jax.nn.sigmoid(_dot(y.astype(_MM), wglu_ref[...]) + bglu_ref[...])
        o_ref[r0:r0 + SCAN_ROWS, 0:D_S5] = y.astype(o_ref.dtype)

    val = h_ref[:, D_S5:D_S5 + D_CONV]
    gate = h_ref[:, D_S5 + D_CONV:D_S5 + 2 * D_CONV]
    hbuf[CONV_HALO:CONV_HALO + tile, :] = val * jax.nn.sigmoid(gate)
    off = CONV_HALO - (CONV_WIDTH - 1)
    acc = jnp.zeros((tile, D_CONV), jnp.float32)
    for k in range(CONV_WIDTH):
        acc = acc + cvw_ref[k:k + 1, :] * hbuf[off + k:off + k + tile, :]
    acc = acc + cvb_ref[...]
    yb = _layer_norm(acc, lng_ref[...], lnb_ref[...])
    o_ref[:, D_S5:D_S5 + D_CONV] = (yb * jax.nn.sigmoid(yb)).astype(o_ref.dtype)
    hbuf[0:CONV_HALO, :] = hbuf[tile:tile + CONV_HALO, :]


def _s5conv(h, tables, d_skip, w_glu, b_glu, cv_w, cv_b, ln_g, ln_b):
    bsz, seq, _ = h.shape
    tile = min(SEQ_TILE, seq)
    assert seq % tile == 0 and tile % SCAN_ROWS == 0
    bre, bim, cre, cim, pwr, pwi, tbr, tbi = tables
    consts = [bre, bim, cre, cim, pwr, pwi, tbr, tbi,
              d_skip.reshape(1, D_S5), w_glu.astype(_MM), b_glu.reshape(1, D_S5),
              cv_w, cv_b.reshape(1, D_CONV), ln_g.reshape(1, D_CONV), ln_b.reshape(1, D_CONV)]
    nlb = S5_LANES // LANE
    in_specs = [pl.BlockSpec((None, tile, EVEN_IN), lambda b, l: (b, l, 0))]
    in_specs += [pl.BlockSpec(a.shape, functools.partial(lambda nd, b, l: (0,) * nd, a.ndim)) for a in consts]
    return pl.pallas_call(
        _s5conv_body, grid=(bsz, seq // tile), in_specs=in_specs,
        out_specs=pl.BlockSpec((None, tile, D_S5 + D_CONV), lambda b, l: (b, l, 0)),
        out_shape=jax.ShapeDtypeStruct((bsz, seq, D_S5 + D_CONV), _MM),
        scratch_shapes=[pltpu.VMEM((nlb, 8, LANE), jnp.float32), pltpu.VMEM((nlb, 8, LANE), jnp.float32),
                        pltpu.VMEM((nlb, SCAN_ROWS, LANE), jnp.float32),
                        pltpu.VMEM((nlb, SCAN_ROWS, LANE), jnp.float32),
                        pltpu.VMEM((tile + CONV_HALO, D_CONV), jnp.float32)],
        compiler_params=_params("arbitrary", "arbitrary"), name="s5conv")(h, *consts)


def _even_mixer(x2, bsz, seq, w_in, b_in, lam_re, lam_im, log_dt, b_re, b_im, c_re, c_im, d_skip,
                w_glu, b_glu, cv_w, cv_b, cv_ln_g, cv_ln_b):
    (h,) = _row_call(_proj_bias_body, [x2], [w_in.astype(_MM), b_in.reshape(1, EVEN_IN)],
                     [(EVEN_IN, jnp.float32)], "even_in_proj")
    tables = _s5_tables(lam_re, lam_im, log_dt, b_re, b_im, c_re, c_im)
    y = _s5conv(h.reshape(bsz, seq, EVEN_IN), tables, d_skip, w_glu, b_glu, cv_w, cv_b, cv_ln_g, cv_ln_b)
    return y.reshape(bsz * seq, D_S5 + D_CONV)


ROPE_W = Q_WIDTH + 3 * KV_WIDTH
V_W = 3 * KV_WIDTH


def _nsa_proj_body(x_ref, w_ref, cos_ref, sin_ref, q_ref, k_ref, v_ref, g_ref):
    y = _dot(x_ref[...].astype(_MM), w_ref[...])
    r = y[:, :ROPE_W]
    reps = ROPE_W // LANE
    cos = jnp.concatenate([cos_ref[...]] * reps, axis=1)
    sin = jnp.concatenate([sin_ref[...]] * reps, axis=1)
    lane = lax.broadcasted_iota(jnp.int32, r.shape, 1)
    first = (lane % HEAD_DIM) < (HEAD_DIM // 2)
    half = HEAD_DIM // 2
    rot = jnp.where(first, -pltpu.roll(r, ROPE_W - half, 1), pltpu.roll(r, half, 1))
    r = r * cos + rot * sin
    q_ref[...] = (r[:, :Q_WIDTH] * (HEAD_DIM ** -0.5)).astype(q_ref.dtype)
    k_ref[...] = r[:, Q_WIDTH:]
    v_ref[...] = y[:, ROPE_W:ROPE_W + V_W]
    g_ref[...] = y[:, ROPE_W + V_W:]


def _nsa_proj(x2, seq, w_in):
    n = x2.shape[0]
    tm = min(ROW_TILE, seq)
    sizes = [Q_WIDTH] + [KV_WIDTH] * 6 + [GATE_WIDTH]
    offs = np.cumsum([0] + sizes)
    cols = lambda i: w_in[:, offs[i]:offs[i + 1]]
    w = jnp.concatenate([cols(0), cols(1), cols(3), cols(5), cols(2), cols(4), cols(6), cols(7)], axis=1).astype(_MM)
    half = HEAD_DIM // 2
    inv = ROPE_THETA ** (-jnp.arange(half, dtype=jnp.float32) / half)
    ang = jnp.arange(seq, dtype=jnp.float32)[:, None] * inv[None, :]
    cos = jnp.tile(jnp.cos(ang), (1, LANE // half))
    sin = jnp.tile(jnp.sin(ang), (1, LANE // half))
    nt = seq // tm
    return pl.pallas_call(
        _nsa_proj_body, grid=(n // tm,),
        in_specs=[pl.BlockSpec((tm, D_MODEL), lambda i: (i, 0)),
                  pl.BlockSpec(w.shape, lambda i: (0, 0)),
                  pl.BlockSpec((tm, LANE), lambda i: (i % nt, 0)),
                  pl.BlockSpec((tm, LANE), lambda i: (i % nt, 0))],
        out_specs=[pl.BlockSpec((tm, Q_WIDTH), lambda i: (i, 0)),
                   pl.BlockSpec((tm, 3 * KV_WIDTH), lambda i: (i, 0)),
                   pl.BlockSpec((tm, V_W), lambda i: (i, 0)),
                   pl.BlockSpec((tm, GATE_WIDTH), lambda i: (i, 0))],
        out_shape=[jax.ShapeDtypeStruct((n, Q_WIDTH), _MM),
                   jax.ShapeDtypeStruct((n, 3 * KV_WIDTH), jnp.float32),
                   jax.ShapeDtypeStruct((n, V_W), jnp.float32),
                   jax.ShapeDtypeStruct((n, GATE_WIDTH), jnp.float32)],
        compiler_params=_params("parallel"), name="nsa_in_proj")(x2, w, cos, sin)


def _compress_body(k_ref, v_ref, pk_ref, pv_ref, wk_ref, wv_ref, ko_ref, vo_ref):
    rows = k_ref.shape[0]

    def one(x_ref, pe_ref, w_ref, o_ref):
        x = x_ref[...]
        lo = _dot((x + pe_ref[0:1, :]).astype(_MM), w_ref[0])
        hi = _dot((x + pe_ref[1:2, :]).astype(_MM), w_ref[1])
        o_ref[...] = (lo + pltpu.roll(hi, rows - 1, 0)).astype(o_ref.dtype)

    one(k_ref, pk_ref, wk_ref, ko_ref)
    one(v_ref, pv_ref, wv_ref, vo_ref)


def _compress(kc, vc, cmp_w_k, cmp_w_v, cmp_pe_k, cmp_pe_v):
    bsz, kvh, seq, _ = kc.shape
    grp = seq // CMP_STRIDE
    flat = CMP_STRIDE * HEAD_DIM
    k2 = kc.reshape(bsz * kvh * grp, flat)
    v2 = vc.reshape(bsz * kvh * grp, flat)
    pe2 = lambda pe: pe.reshape(2, flat)
    w2 = lambda w: w.reshape(2, flat, HEAD_DIM).astype(_MM)
    ko, vo = _row_call(_compress_body, [k2, v2], [pe2(cmp_pe_k), pe2(cmp_pe_v), w2(cmp_w_k), w2(cmp_w_v)],
                       [(HEAD_DIM, _MM), (HEAD_DIM, _MM)], "nsa_compress", tm=grp)
    return ko.reshape(bsz, kvh, grp, HEAD_DIM), vo.reshape(bsz, kvh, grp, HEAD_DIM)


def _nsa_attn_body(q_ref, g_ref, kc_ref, vct_ref, ks_ref, vst_ref, kw_ref, vwt_ref, ov_ref, o_ref, sel_ref):
    qi = pl.program_id(2)
    q0 = qi * Q_TILE
    lanes = GQA * Q_TILE
    n_grp = kc_ref.shape[0]
    n_sb = ov_ref.shape[0]
    seq = ks_ref.shape[0]

    q = q_ref[...].reshape(lanes, HEAD_DIM)
    t_lane = q0 + (lax.broadcasted_iota(jnp.int32, (1, lanes), 1) % Q_TILE)

    s = _dot_nt(kc_ref[...], q)
    n_idx = lax.broadcasted_iota(jnp.int32, (n_grp, lanes), 0)
    vis = (n_idx * CMP_STRIDE + (CMP_LEN - 1)) <= t_lane
    s = jnp.where(vis, s, NEG_INF)
    e = jnp.where(vis, jnp.exp(s - jnp.max(s, axis=0, keepdims=True)), 0.0)
    den = jnp.sum(e, axis=0, keepdims=True)
    p_c = e / jnp.where(den > 0.0, den, 1.0)
    o_c = _dot(vct_ref[...], p_c.astype(_MM))

    psum = p_c[:, 0:Q_TILE]
    for g in range(1, GQA):
        psum = psum + p_c[:, g * Q_TILE:(g + 1) * Q_TILE]
    imp = jnp.dot(ov_ref[...], psum, preferred_element_type=jnp.float32,
                  precision=lax.Precision.HIGHEST)
    blk = lax.broadcasted_iota(jnp.int32, (n_sb, Q_TILE), 0)
    cur = t_lane[:, 0:Q_TILE] // SEL_LEN
    allowed = blk <= cur
    forced = (blk == 0) | (blk == cur) | (blk == cur - 1)
    score = jnp.where(allowed, imp + jnp.where(forced, FORCE_BONUS, 0.0), -1.0)
    rank = jnp.zeros((n_sb, Q_TILE), jnp.float32)
    for j in range(n_sb):
        sj = score[j:j + 1, :]
        rank = rank + jnp.where(sj > score, 1.0, jnp.where((sj == score) & (blk > j), 1.0, 0.0))
    chosen = jnp.where((rank < float(N_SELECT)) & (score >= 0.0), 1.0, 0.0)
    sel_ref[0:n_sb, :] = jnp.concatenate([chosen] * GQA, axis=1)

    kpos_blk = lax.broadcasted_iota(jnp.int32, (SEL_LEN, lanes), 0)
    per_chunk = KEY_CHUNK // SEL_LEN

    def sel_chunk(c, carry):
        m_run, l_run, acc = carry
        k0 = pl.multiple_of(c * KEY_CHUNK, KEY_CHUNK)
        sc = _dot_nt(ks_ref[pl.ds(k0, KEY_CHUNK), :], q)
        parts = []
        sel_rows = sel_ref[pl.ds(pl.multiple_of(c * per_chunk, per_chunk), per_chunk), :]
        for j in range(per_chunk):
            on = sel_rows[j:j + 1, :] > 0.5
            ok = on & ((k0 + j * SEL_LEN + kpos_blk) <= t_lane)
            parts.append(jnp.where(ok, sc[j * SEL_LEN:(j + 1) * SEL_LEN, :], NEG_INF))
        sc = jnp.concatenate(parts, axis=0)
        m_new = jnp.maximum(m_run, jnp.max(sc, axis=0, keepdims=True))
        scale = jnp.exp(m_run - m_new)
        p = jnp.exp(sc - m_new)
        l_new = scale * l_run + jnp.sum(p, axis=0, keepdims=True)
        acc = scale * acc + _dot(vst_ref[:, pl.ds(k0, KEY_CHUNK)], p.astype(_MM))
        return m_new, l_new, acc

    n_chunks = (q0 + Q_TILE + KEY_CHUNK - 1) // KEY_CHUNK
    init = (jnp.full((1, lanes), NEG_INF, jnp.float32), jnp.zeros((1, lanes), jnp.float32),
            jnp.zeros((HEAD_DIM, lanes), jnp.float32))
    _, l_s, acc_s = lax.fori_loop(0, n_chunks, sel_chunk, init)
    o_s = acc_s / l_s

    span = WINDOW + Q_TILE
    w0 = pl.multiple_of(jnp.maximum(q0 - WINDOW, 0), Q_TILE)
    sw = _dot_nt(kw_ref[pl.ds(w0, span), :], q)
    dist = t_lane - (w0 + lax.broadcasted_iota(jnp.int32, (span, lanes), 0))
    okw = (dist >= 0) & (dist < WINDOW)
    sw = jnp.where(okw, sw, NEG_INF)
    pw = jnp.exp(sw - jnp.max(sw, axis=0, keepdims=True))
    o_w = _dot(vwt_ref[:, pl.ds(w0, span)], pw.astype(_MM)) / jnp.sum(pw, axis=0, keepdims=True)

    gts = jax.nn.sigmoid(g_ref[...])
    out = gts[0:1, :] * o_c + gts[1:2, :] * o_s + gts[2:3, :] * o_w
    for g in range(GQA):
        o_ref[g] = out[:, g * Q_TILE:(g + 1) * Q_TILE].astype(o_ref.dtype)


def _nsa_attention(q4, g5, k_cmp, v_cmp_t, ks, vs_t, kw, vw_t):
    bsz, _, seq, _ = q4.shape
    n_grp = k_cmp.shape[2]
    n_sb = seq // SEL_LEN
    assert seq % KEY_CHUNK == 0 and seq >= WINDOW + Q_TILE
    c_start = np.arange(n_grp) * CMP_STRIDE
    s_start = np.arange(n_sb) * SEL_LEN
    overlap = ((c_start[None, :] < s_start[:, None] + SEL_LEN)
               & (c_start[None, :] + CMP_LEN > s_start[:, None])
               & (np.arange(n_grp)[None, :] < n_grp - 1)).astype(np.float32)
    kv_row = lambda last: pl.BlockSpec((None, None, last, HEAD_DIM), lambda b, h, i: (b, h, 0, 0))
    kv_col = lambda last: pl.BlockSpec((None, None, HEAD_DIM, last), lambda b, h, i: (b, h, 0, 0))
    return pl.pallas_call(
        _nsa_attn_body, grid=(bsz, N_KV_HEADS, seq // Q_TILE),
        in_specs=[pl.BlockSpec((None, GQA, Q_TILE, HEAD_DIM), lambda b, h, i: (b, h, i, 0)),
                  pl.BlockSpec((None, None, None, 3, GQA * Q_TILE), lambda b, h, i: (b, h, i, 0, 0)),
                  kv_row(n_grp), kv_col(n_grp), kv_row(seq), kv_col(seq), kv_row(seq), kv_col(seq),
                  pl.BlockSpec(overlap.shape, lambda b, h, i: (0, 0))],
        out_specs=pl.BlockSpec((None, GQA, HEAD_DIM, Q_TILE), lambda b, h, i: (b, h, 0, i)),
        out_shape=jax.ShapeDtypeStruct((bsz, N_HEADS, HEAD_DIM, seq), _MM),
        scratch_shapes=[pltpu.VMEM((max(n_sb, 8), GQA * Q_TILE), jnp.float32)],
        compiler_params=_params("parallel", "parallel", "arbitrary"), name="nsa_attention")(
            q4, g5, k_cmp, v_cmp_t, ks, vs_t, kw, vw_t, jnp.asarray(overlap))


def _nsa_mixer(x2, bsz, seq, w_in, cmp_w_k, cmp_w_v, cmp_pe_k, cmp_pe_v):
    q, k3, v3, gl = _nsa_proj(x2, seq, w_in)
    q4 = q.reshape(bsz, seq, N_HEADS, HEAD_DIM).transpose(0, 2, 1, 3)
    k3 = k3.reshape(bsz, seq, 3, N_KV_HEADS, HEAD_DIM).transpose(2, 0, 3, 1, 4)
    v3 = v3.reshape(bsz, seq, 3, N_KV_HEADS, HEAD_DIM).transpose(2, 0, 3, 1, 4)
    k_cmp, v_cmp = _compress(k3[0], v3[0], cmp_w_k, cmp_w_v, cmp_pe_k, cmp_pe_v)
    tq = seq // Q_TILE
    g5 = gl.reshape(bsz, tq, Q_TILE, N_KV_HEADS, GQA, 3).transpose(0, 3, 1, 5, 4, 2)
    g5 = g5.reshape(bsz, N_KV_HEADS, tq, 3, GQA * Q_TILE)
    tr = lambda a: a.transpose(0, 1, 3, 2).astype(_MM)
    o = _nsa_attention(q4, g5, k_cmp, tr(v_cmp), k3[1].astype(_MM), tr(v3[1]), k3[2].astype(_MM), tr(v3[2]))
    return o.transpose(0, 3, 1, 2).reshape(bsz * seq, Q_WIDTH)


def _router_body(x_ref, w_ref, b_ref, idx_ref, gate_ref):
    logits = jnp.dot(x_ref[...], w_ref[...], preferred_element_type=jnp.float32,
                     precision=lax.Precision.HIGHEST)
    scores = jax.nn.sigmoid(logits)
    tm = scores.shape[0]
    pick = scores + b_ref[...]
    lane = lax.broadcasted_iota(jnp.int32, (tm, N_EXPERTS), 1).astype(jnp.float32)
    out_lane = lax.broadcasted_iota(jnp.int32, (tm, LANE), 1)
    idx_out = jnp.zeros((tm, LANE), jnp.int32)
    gate_out = jnp.zeros((tm, LANE), jnp.float32)
    total = jnp.zeros((tm, 1), jnp.float32)
    for k in range(TOP_K):
        best = jnp.max(pick, axis=1, keepdims=True)
        which = jnp.min(jnp.where(pick == best, lane, float(N_EXPERTS)), axis=1, keepdims=True)
        hit = lane == which
        gk = jnp.sum(jnp.where(hit, scores, 0.0), axis=1, keepdims=True)
        pick = jnp.where(hit, -jnp.inf, pick)
        idx_out = jnp.where(out_lane == k, which.astype(jnp.int32), idx_out)
        gate_out = jnp.where(out_lane == k, gk, gate_out)
        total = total + gk
    idx_ref[...] = idx_out
    gate_ref[...] = gate_out / total * ROUTE_SCALE


def _expert_body(blk_exp_ref, n_used_ref, x_ref, wgu_ref, wd_ref, o_ref):
    i = pl.program_id(0)

    @pl.when(i < n_used_ref[0])
    def _():
        gu = _dot(x_ref[...], wgu_ref[...])
        h = jax.nn.silu(gu[:, :D_EXPERT]) * gu[:, D_EXPERT:]
        o_ref[...] = _dot(h.astype(_MM), wd_ref[...]).astype(o_ref.dtype)

    @pl.when(i >= n_used_ref[0])
    def _():
        o_ref[...] = jnp.zeros_like(o_ref)


def _experts(xs, blk_exp, n_used, w_gu, w_down):
    n_rows = xs.shape[0]
    n_blk = n_rows // EXPERT_BLOCK
    grid_spec = pltpu.PrefetchScalarGridSpec(
        num_scalar_prefetch=2, grid=(n_blk,),
        in_specs=[pl.BlockSpec((EXPERT_BLOCK, D_MODEL), lambda i, be, nu: (i, 0)),
                  pl.BlockSpec((None, D_MODEL, 2 * D_EXPERT), lambda i, be, nu: (be[i], 0, 0)),
                  pl.BlockSpec((None, D_EXPERT, D_MODEL), lambda i, be, nu: (be[i], 0, 0))],
        out_specs=pl.BlockSpec((EXPERT_BLOCK, D_MODEL), lambda i, be, nu: (i, 0)))
    return pl.pallas_call(_expert_body, grid_spec=grid_spec,
                          out_shape=jax.ShapeDtypeStruct((n_rows, D_MODEL), _MM),
                          compiler_params=_params("arbitrary"), name="moe_experts")(
                              blk_exp, n_used, xs, w_gu, w_down)


def _combine_body(x_ref, y_ref, gate_ref, sgu_ref, sd_ref, g_ref, b_ref, o_ref):
    x = x_ref[...]
    gate = gate_ref[...]
    routed = jnp.zeros(x.shape, jnp.float32)
    for k in range(TOP_K):
        routed = routed + gate[:, k:k + 1] * y_ref[:, k * D_MODEL:(k + 1) * D_MODEL].astype(jnp.float32)
    gu = _dot(x.astype(_MM), sgu_ref[...])
    hs = jax.nn.silu(gu[:, :D_EXPERT]) * gu[:, D_EXPERT:]
    shared = _dot(hs.astype(_MM), sd_ref[...])
    o_ref[...] = _layer_norm(ALPHA * x + (routed + shared), g_ref[...], b_ref[...])


def _moe_ln(x2, router_w, router_b, w_gate, w_up, w_down, s_gate, s_up, s_down, ln_g, ln_b):
    n = x2.shape[0]
    idx128, gate128 = _row_call(_router_body, [x2], [router_w, router_b.reshape(1, N_EXPERTS)],
                                [(LANE, jnp.int32), (LANE, jnp.float32)], "moe_router")
    idx = idx128[:, :TOP_K]
    onehot = (idx[:, :, None] == jnp.arange(N_EXPERTS, dtype=jnp.int32)[None, None, :])
    member = jnp.sum(onehot.astype(jnp.int32), axis=1)
    before = jnp.cumsum(member, axis=0) - member
    counts = jnp.sum(member, axis=0)
    padded = (counts + EXPERT_BLOCK - 1) // EXPERT_BLOCK * EXPERT_BLOCK
    pad_end = jnp.cumsum(padded)
    pad_start = pad_end - padded
    dest = jnp.take_along_axis(before + pad_start[None, :], idx, axis=1)
    n_assign = n * TOP_K
    n_blk = -(-n_assign // EXPERT_BLOCK) + N_EXPERTS
    n_rows = n_blk * EXPERT_BLOCK
    blk_exp = jnp.minimum(jnp.searchsorted(pad_end, jnp.arange(n_blk, dtype=jnp.int32) * EXPERT_BLOCK,
                                           side='right'), N_EXPERTS - 1).astype(jnp.int32)
    n_used = (pad_end[-1:] // EXPERT_BLOCK).astype(jnp.int32)
    tok = jnp.broadcast_to(jnp.arange(n, dtype=jnp.int32)[:, None], (n, TOP_K))
    row_tok = jnp.full((n_rows,), n, jnp.int32).at[dest.reshape(-1)].set(tok.reshape(-1))
    x_pad = jnp.concatenate([x2.astype(_MM), jnp.zeros((1, D_MODEL), _MM)], axis=0)
    xs = x_pad[row_tok]
    w_gu = jnp.concatenate([w_gate, w_up], axis=-1).astype(_MM)
    ys = _experts(xs, blk_exp, n_used, w_gu, w_down.astype(_MM))
    yg = ys[dest.reshape(-1)].reshape(n, TOP_K * D_MODEL)
    s_gu = jnp.concatenate([s_gate, s_up], axis=-1).astype(_MM)
    (out,) = _row_call(_combine_body, [x2, yg, gate128],
                       [s_gu, s_down.astype(_MM), ln_g.reshape(1, D_MODEL), ln_b.reshape(1, D_MODEL)],
                       [(D_MODEL, jnp.float32)], "moe_combine")
    return out


def kernel(x, ev_w_in, ev_b_in, s5_lam_re, s5_lam_im, s5_log_dt, s5_b_re, s5_b_im, s5_c_re, s5_c_im, s5_d, s5_w_glu, s5_b_glu, cv_w, cv_b, cv_ln_g, cv_ln_b, ev_w_out, od_w_in, cmp_w_k, cmp_w_v, cmp_pe_k, cmp_pe_v, od_w_out, ln1_g, ln1_b, ln2_g, ln2_b, router_w, router_b, ex_w_gate, ex_w_up, ex_w_down, sh_w_gate, sh_w_up, sh_w_down):
    bsz, seq, _ = x.shape
    h = x.reshape(bsz * seq, D_MODEL)
    for layer in range(DEPTH):
        i = layer // 2
        if layer % 2 == 0:
            mix = _even_mixer(h, bsz, seq, ev_w_in[i], ev_b_in[i], s5_lam_re[i], s5_lam_im[i], s5_log_dt[i],
                              s5_b_re[i], s5_b_im[i], s5_c_re[i], s5_c_im[i], s5_d[i].reshape(-1),
                              s5_w_glu[i], s5_b_glu[i], cv_w[i], cv_b[i], cv_ln_g[i], cv_ln_b[i])
            w_out = ev_w_out[i]
        else:
            mix = _nsa_mixer(h, bsz, seq, od_w_in[i], cmp_w_k[i], cmp_w_v[i], cmp_pe_k[i], cmp_pe_v[i])
            w_out = od_w_out[i]
        (h,) = _row_call(_proj_res_ln_body, [mix, h],
                         [w_out.astype(_MM), ln1_g[layer].reshape(1, D_MODEL), ln1_b[layer].reshape(1, D_MODEL)],
                         [(D_MODEL, jnp.float32)], "out_proj_ln")
        h = _moe_ln(h, router_w[layer], router_b[layer], ex_w_gate[layer], ex_w_up[layer], ex_w_down[layer],
                    sh_w_gate[layer], sh_w_up[layer], sh_w_down[layer], ln2_g[layer], ln2_b[layer])
    return h.reshape(bsz, seq, D_MODEL)
```

```python
import functools
import math

import numpy as np
import jax
import jax.numpy as jnp
from jax import lax
from jax.experimental import pallas as pl
from jax.experimental.pallas import tpu as pltpu

D_MODEL = 1024
DEPTH = 4
ALPHA = (2.0 * DEPTH) ** 0.25
LN_EPS = 1e-5
NEG_INF = -1e30

D_S5 = 512
S5_GROUP = 16
S5_GROUPS = 32
S5_STATE = 64
S5_LANES = S5_GROUPS * S5_STATE
D_CONV = 512
CONV_WIDTH = 31
EVEN_IN = D_S5 + 2 * D_CONV

HEAD_DIM = 64
N_HEADS = 16
N_KV_HEADS = 4
GQA = 4
CMP_LEN = 32
CMP_STRIDE = 16
SEL_LEN = 64
N_SELECT = 16
WINDOW = 512
FORCE_BONUS = 1e4
ROPE_THETA = 10000.0
Q_WIDTH = N_HEADS * HEAD_DIM
KV_WIDTH = N_KV_HEADS * HEAD_DIM
GATE_WIDTH = 3 * N_HEADS

N_EXPERTS = 64
TOP_K = 8
D_EXPERT = 256
ROUTE_SCALE = 2.5

LANE = 128
_MM = jnp.bfloat16
_VMEM_LIMIT = 56 * 1024 * 1024

ROW_TILE = 256
SCAN_ROWS = 128
SEQ_TILE = 256
CONV_HALO = 32
Q_TILE = 128
KEY_CHUNK = 512
EXPERT_BLOCK = 256


def _dot(a, b):
    return jnp.dot(a, b, preferred_element_type=jnp.float32)


def _dot_nt(a, b):
    return lax.dot_general(a, b, (((1,), (1,)), ((), ())), preferred_element_type=jnp.float32)


def _layer_norm(x, g, b):
    mu = jnp.mean(x, axis=-1, keepdims=True)
    xc = x - mu
    var = jnp.mean(xc * xc, axis=-1, keepdims=True)
    return xc * lax.rsqrt(var + LN_EPS) * g + b


HALF = D_MODEL // 2


def _pack_rows(x):
    bits = lambda v: lax.bitcast_convert_type(v.astype(jnp.bfloat16).astype(jnp.float32), jnp.uint32)
    return (bits(x[:, HALF:]) & jnp.uint32(0xFFFF0000)) | (bits(x[:, :HALF]) >> 16)


def _unpack_rows(w):
    lo = lax.bitcast_convert_type(w << 16, jnp.float32)
    hi = lax.bitcast_convert_type(w & jnp.uint32(0xFFFF0000), jnp.float32)
    return jnp.concatenate([lo, hi], axis=1)


def _params(*sem):
    return pltpu.CompilerParams(dimension_semantics=sem, vmem_limit_bytes=_VMEM_LIMIT)


def _row_call(body, row_ins, const_ins, outs, name, tm=ROW_TILE):
    m = row_ins[0].shape[0]
    tm = min(tm, m)
    assert m % tm == 0
    in_specs = [pl.BlockSpec((tm, a.shape[1]), lambda i: (i, 0)) for a in row_ins]
    in_specs += [pl.BlockSpec(a.shape, functools.partial(lambda nd, i: (0,) * nd, a.ndim)) for a in const_ins]
    out_specs = [pl.BlockSpec((tm, c), lambda i: (i, 0)) for c, _ in outs]
    out_shape = [jax.ShapeDtypeStruct((m, c), dt) for c, dt in outs]
    return pl.pallas_call(body, grid=(m // tm,), in_specs=in_specs, out_specs=out_specs,
                          out_shape=out_shape, compiler_params=_params("arbitrary"), name=name)(
                              *row_ins, *const_ins)


def _proj_bias_body(x_ref, w_ref, b_ref, o_ref):
    o_ref[...] = _dot(x_ref[...].astype(_MM), w_ref[...]) + b_ref[...]


def _proj_res_ln_body(y_ref, res_ref, w_ref, g_ref, b_ref, o_ref, op_ref):
    mix = _dot(y_ref[...].astype(_MM), w_ref[...])
    o = _layer_norm(ALPHA * res_ref[...] + mix, g_ref[...], b_ref[...])
    o_ref[...] = o
    op_ref[...] = _pack_rows(o)


def _s5_tables(lam_re, lam_im, log_dt, b_re, b_im, c_re, c_im):
    f32 = jnp.float32
    dt = jnp.exp(log_dt.astype(f32))[:, None]
    decay = jnp.exp(lam_re * dt)
    a_re, a_im = decay * jnp.cos(lam_im * dt), decay * jnp.sin(lam_im * dt)
    den = lam_re ** 2 + lam_im ** 2
    f_re = ((a_re - 1.0) * lam_re + a_im * lam_im) / den
    f_im = (a_im * lam_re - (a_re - 1.0) * lam_im) / den
    bb_re = f_re[..., None] * b_re - f_im[..., None] * b_im
    bb_im = f_re[..., None] * b_im + f_im[..., None] * b_re

    gl = LANE // S5_GROUP
    nj = S5_GROUPS // gl
    eye = jnp.eye(gl, dtype=f32)

    def in_blocks(bb):
        t = bb.reshape(nj, gl, S5_STATE, S5_GROUP)
        t = jnp.einsum('jgph,gk->jghkp', t, eye)
        return t.reshape(nj, gl * S5_GROUP, gl * S5_STATE).astype(_MM)

    def out_blocks(c):
        t = c.reshape(nj, gl, S5_GROUP, S5_STATE)
        t = jnp.einsum('jghp,gk->jgpkh', t, eye)
        return t.reshape(nj, gl * S5_STATE, gl * S5_GROUP).astype(_MM)

    ar, ai = a_re.reshape(1, S5_LANES), a_im.reshape(1, S5_LANES)
    pows_r, pows_i = [ar], [ai]
    for _ in range(int(math.log2(SCAN_ROWS)) - 1):
        pr, pi = pows_r[-1], pows_i[-1]
        pows_r.append(pr * pr - pi * pi)
        pows_i.append(2.0 * pr * pi)
    tr, ti = ar, ai
    for k in range(int(math.log2(SCAN_ROWS))):
        pr, pi = pows_r[k], pows_i[k]
        tr, ti = (jnp.concatenate([tr, tr * pr - ti * pi], axis=0),
                  jnp.concatenate([ti, tr * pi + ti * pr], axis=0))
    nlb = S5_LANES // LANE
    to3 = lambda t: t.reshape(t.shape[0], nlb, LANE).transpose(1, 0, 2)
    return (in_blocks(bb_re), in_blocks(bb_im), out_blocks(c_re), out_blocks(c_im),
            to3(jnp.concatenate(pows_r, axis=0)), to3(jnp.concatenate(pows_i, axis=0)), to3(tr), to3(ti))


def _s5conv_body(h_ref, bre_ref, bim_ref, cre_ref, cim_ref, pwr_ref, pwi_ref, tbr_ref, tbi_ref,
                 d_ref, wglu_ref, bglu_ref, cvw_ref, cvb_ref, lng_ref, lnb_ref, o_ref,
                 st_re, st_im, xr_ref, xi_ref, hbuf):
    tile = h_ref.shape[0]
    n_steps = int(math.log2(SCAN_ROWS))
    nlb = S5_LANES // LANE
    nj = bre_ref.shape[0]
    per_j = nlb // nj

    @pl.when(pl.program_id(1) == 0)
    def _():
        st_re[...] = jnp.zeros_like(st_re)
        st_im[...] = jnp.zeros_like(st_im)
        hbuf[0:CONV_HALO, :] = jnp.zeros((CONV_HALO, D_CONV), jnp.float32)

    row = lax.broadcasted_iota(jnp.int32, (SCAN_ROWS, LANE), 0)

    for c in range(tile // SCAN_ROWS):
        r0 = c * SCAN_ROWS
        u = h_ref[r0:r0 + SCAN_ROWS, 0:D_S5]
        ub = u.astype(_MM)
        for j in range(nj):
            uj = ub[:, j * LANE:(j + 1) * LANE]
            br = _dot(uj, bre_ref[j])
            bi = _dot(uj, bim_ref[j])
            for q in range(per_j):
                xr_ref[j * per_j + q] = br[:, q * LANE:(q + 1) * LANE]
                xi_ref[j * per_j + q] = bi[:, q * LANE:(q + 1) * LANE]

        def scan_block(lb, carry):
            xr, xi = xr_ref[lb], xi_ref[lb]
            for k in range(n_steps):
                d = 1 << k
                ar = pwr_ref[lb, k:k + 1, :]
                ai = pwi_ref[lb, k:k + 1, :]
                keep = row >= d
                sr = jnp.where(keep, pltpu.roll(xr, d, 0), 0.0)
                si = jnp.where(keep, pltpu.roll(xi, d, 0), 0.0)
                xr, xi = xr + ar * sr - ai * si, xi + ar * si + ai * sr
            pr, pi = st_re[lb, 0:1, :], st_im[lb, 0:1, :]
            tr, ti = tbr_ref[lb], tbi_ref[lb]
            xr, xi = xr + tr * pr - ti * pi, xi + tr * pi + ti * pr
            xr_ref[lb] = xr
            xi_ref[lb] = xi
            st_re[lb, 0:1, :] = xr[SCAN_ROWS - 1:SCAN_ROWS, :]
            st_im[lb, 0:1, :] = xi[SCAN_ROWS - 1:SCAN_ROWS, :]
            return carry

        lax.fori_loop(0, nlb, scan_block, 0)

        ys = []
        for j in range(nj):
            xr = jnp.concatenate([xr_ref[j * per_j + q] for q in range(per_j)], axis=1).astype(_MM)
            xi = jnp.concatenate([xi_ref[j * per_j + q] for q in range(per_j)], axis=1).astype(_MM)
            ys.append(_dot(xr, cre_ref[j]) - _dot(xi, cim_ref[j]))
        y = jnp.concatenate(ys, axis=1) + d_ref[...] * u
        y = jax.nn.gelu(y)
        y = y * jax.nn.sigmoid(_dot(y.astype(_MM), wglu_ref[...]) + bglu_ref[...])
        o_ref[r0:r0 + SCAN_ROWS, 0:D_S5] = y.astype(o_ref.dtype)

    val = h_ref[:, D_S5:D_S5 + D_CONV]
    gate = h_ref[:, D_S5 + D_CONV:D_S5 + 2 * D_CONV]
    hbuf[CONV_HALO:CONV_HALO + tile, :] = val * jax.nn.sigmoid(gate)
    off = CONV_HALO - (CONV_WIDTH - 1)
    acc = jnp.zeros((tile, D_CONV), jnp.float32)
    for k in range(CONV_WIDTH):
        acc = acc + cvw_ref[k:k + 1, :] * hbuf[off + k:off + k + tile, :]
    acc = acc + cvb_ref[...]
    yb = _layer_norm(acc, lng_ref[...], lnb_ref[...])
    o_ref[:, D_S5:D_S5 + D_CONV] = (yb * jax.nn.sigmoid(yb)).astype(o_ref.dtype)
    hbuf[0:CONV_HALO, :] = hbuf[tile:tile + CONV_HALO, :]


def _s5conv(h, tables, d_skip, w_glu, b_glu, cv_w, cv_b, ln_g, ln_b):
    bsz, seq, _ = h.shape
    tile = min(SEQ_TILE, seq)
    assert seq % tile == 0 and tile % SCAN_ROWS == 0
    bre, bim, cre, cim, pwr, pwi, tbr, tbi = tables
    consts = [bre, bim, cre, cim, pwr, pwi, tbr, tbi,
              d_skip.reshape(1, D_S5), w_glu.astype(_MM), b_glu.reshape(1, D_S5),
              cv_w, cv_b.reshape(1, D_CONV), ln_g.reshape(1, D_CONV), ln_b.reshape(1, D_CONV)]
    nlb = S5_LANES // LANE
    in_specs = [pl.BlockSpec((None, tile, EVEN_IN), lambda b, l: (b, l, 0))]
    in_specs += [pl.BlockSpec(a.shape, functools.partial(lambda nd, b, l: (0,) * nd, a.ndim)) for a in consts]
    return pl.pallas_call(
        _s5conv_body, grid=(bsz, seq // tile), in_specs=in_specs,
        out_specs=pl.BlockSpec((None, tile, D_S5 + D_CONV), lambda b, l: (b, l, 0)),
        out_shape=jax.ShapeDtypeStruct((bsz, seq, D_S5 + D_CONV), _MM),
        scratch_shapes=[pltpu.VMEM((nlb, 8, LANE), jnp.float32), pltpu.VMEM((nlb, 8, LANE), jnp.float32),
                        pltpu.VMEM((nlb, SCAN_ROWS, LANE), jnp.float32),
                        pltpu.VMEM((nlb, SCAN_ROWS, LANE), jnp.float32),
                        pltpu.VMEM((tile + CONV_HALO, D_CONV), jnp.float32)],
        compiler_params=_params("arbitrary", "arbitrary"), name="s5conv")(h, *consts)


def _even_mixer(x2, bsz, seq, w_in, b_in, lam_re, lam_im, log_dt, b_re, b_im, c_re, c_im, d_skip,
                w_glu, b_glu, cv_w, cv_b, cv_ln_g, cv_ln_b):
    (h,) = _row_call(_proj_bias_body, [x2], [w_in.astype(_MM), b_in.reshape(1, EVEN_IN)],
                     [(EVEN_IN, jnp.float32)], "even_in_proj")
    tables = _s5_tables(lam_re, lam_im, log_dt, b_re, b_im, c_re, c_im)
    y = _s5conv(h.reshape(bsz, seq, EVEN_IN), tables, d_skip, w_glu, b_glu, cv_w, cv_b, cv_ln_g, cv_ln_b)
    return y.reshape(bsz * seq, D_S5 + D_CONV)


ROPE_W = Q_WIDTH + 3 * KV_WIDTH
V_W = 3 * KV_WIDTH


def _nsa_proj_body(x_ref, w_ref, cos_ref, sin_ref, q_ref, k_ref, v_ref, g_ref):
    y = _dot(x_ref[...].astype(_MM), w_ref[...])
    r = y[:, :ROPE_W]
    reps = ROPE_W // LANE
    cos = jnp.concatenate([cos_ref[...]] * reps, axis=1)
    sin = jnp.concatenate([sin_ref[...]] * reps, axis=1)
    lane = lax.broadcasted_iota(jnp.int32, r.shape, 1)
    first = (lane % HEAD_DIM) < (HEAD_DIM // 2)
    half = HEAD_DIM // 2
    rot = jnp.where(first, -pltpu.roll(r, ROPE_W - half, 1), pltpu.roll(r, half, 1))
    r = r * cos + rot * sin
    q_ref[...] = (r[:, :Q_WIDTH] * (HEAD_DIM ** -0.5)).astype(q_ref.dtype)
    k_ref[...] = r[:, Q_WIDTH:]
    v_ref[...] = y[:, ROPE_W:ROPE_W + V_W]
    g_ref[...] = y[:, ROPE_W + V_W:]


def _nsa_proj(x2, seq, w_in):
    n = x2.shape[0]
    tm = min(ROW_TILE, seq)
    sizes = [Q_WIDTH] + [KV_WIDTH] * 6 + [GATE_WIDTH]
    offs = np.cumsum([0] + sizes)
    cols = lambda i: w_in[:, offs[i]:offs[i + 1]]
    w = jnp.concatenate([cols(0), cols(1), cols(3), cols(5), cols(2), cols(4), cols(6), cols(7)], axis=1).astype(_MM)
    half = HEAD_DIM // 2
    inv = ROPE_THETA ** (-jnp.arange(half, dtype=jnp.float32) / half)
    ang = jnp.arange(seq, dtype=jnp.float32)[:, None] * inv[None, :]
    cos = jnp.tile(jnp.cos(ang), (1, LANE // half))
    sin = jnp.tile(jnp.sin(ang), (1, LANE // half))
    nt = seq // tm
    return pl.pallas_call(
        _nsa_proj_body, grid=(n // tm,),
        in_specs=[pl.BlockSpec((tm, D_MODEL), lambda i: (i, 0)),
                  pl.BlockSpec(w.shape, lambda i: (0, 0)),
                  pl.BlockSpec((tm, LANE), lambda i: (i % nt, 0)),
                  pl.BlockSpec((tm, LANE), lambda i: (i % nt, 0))],
        out_specs=[pl.BlockSpec((tm, Q_WIDTH), lambda i: (i, 0)),
                   pl.BlockSpec((tm, 3 * KV_WIDTH), lambda i: (i, 0)),
                   pl.BlockSpec((tm, V_W), lambda i: (i, 0)),
                   pl.BlockSpec((tm, GATE_WIDTH), lambda i: (i, 0))],
        out_shape=[jax.ShapeDtypeStruct((n, Q_WIDTH), _MM),
                   jax.ShapeDtypeStruct((n, 3 * KV_WIDTH), jnp.float32),
                   jax.ShapeDtypeStruct((n, V_W), jnp.float32),
                   jax.ShapeDtypeStruct((n, GATE_WIDTH), jnp.float32)],
        compiler_params=_params("arbitrary"), name="nsa_in_proj")(x2, w, cos, sin)


def _compress_body(k_ref, v_ref, pk_ref, pv_ref, wk_ref, wv_ref, ko_ref, vo_ref):
    rows = k_ref.shape[0]

    def one(x_ref, pe_ref, w_ref, o_ref):
        x = x_ref[...]
        lo = _dot((x + pe_ref[0:1, :]).astype(_MM), w_ref[0])
        hi = _dot((x + pe_ref[1:2, :]).astype(_MM), w_ref[1])
        o_ref[...] = (lo + pltpu.roll(hi, rows - 1, 0)).astype(o_ref.dtype)

    one(k_ref, pk_ref, wk_ref, ko_ref)
    one(v_ref, pv_ref, wv_ref, vo_ref)


def _compress(kc, vc, cmp_w_k, cmp_w_v, cmp_pe_k, cmp_pe_v):
    bsz, kvh, seq, _ = kc.shape
    grp = seq // CMP_STRIDE
    flat = CMP_STRIDE * HEAD_DIM
    k2 = kc.reshape(bsz * kvh * grp, flat)
    v2 = vc.reshape(bsz * kvh * grp, flat)
    pe2 = lambda pe: pe.reshape(2, flat)
    w2 = lambda w: w.reshape(2, flat, HEAD_DIM).astype(_MM)
    ko, vo = _row_call(_compress_body, [k2, v2], [pe2(cmp_pe_k), pe2(cmp_pe_v), w2(cmp_w_k), w2(cmp_w_v)],
                       [(HEAD_DIM, _MM), (HEAD_DIM, _MM)], "nsa_compress", tm=grp)
    return ko.reshape(bsz, kvh, grp, HEAD_DIM), vo.reshape(bsz, kvh, grp, HEAD_DIM)


def _nsa_attn_body(q_ref, g_ref, kc_ref, vct_ref, ks_ref, vst_ref, kw_ref, vwt_ref, ov_ref, o_ref, sel_ref):
    qi = pl.program_id(2)
    q0 = qi * Q_TILE
    lanes = GQA * Q_TILE
    n_grp = kc_ref.shape[0]
    n_sb = ov_ref.shape[0]
    seq = ks_ref.shape[0]

    q = q_ref[...].reshape(lanes, HEAD_DIM)
    t_lane = q0 + (lax.broadcasted_iota(jnp.int32, (1, lanes), 1) % Q_TILE)

    s = _dot_nt(kc_ref[...], q)
    n_idx = lax.broadcasted_iota(jnp.int32, (n_grp, lanes), 0)
    vis = (n_idx * CMP_STRIDE + (CMP_LEN - 1)) <= t_lane
    s = jnp.where(vis, s, NEG_INF)
    e = jnp.where(vis, jnp.exp(s - jnp.max(s, axis=0, keepdims=True)), 0.0)
    den = jnp.sum(e, axis=0, keepdims=True)
    p_c = e / jnp.where(den > 0.0, den, 1.0)
    o_c = _dot(vct_ref[...], p_c.astype(_MM))

    psum = p_c[:, 0:Q_TILE]
    for g in range(1, GQA):
        psum = psum + p_c[:, g * Q_TILE:(g + 1) * Q_TILE]
    imp = jnp.dot(ov_ref[...], psum, preferred_element_type=jnp.float32,
                  precision=lax.Precision.HIGHEST)
    blk = lax.broadcasted_iota(jnp.int32, (n_sb, Q_TILE), 0)
    cur = t_lane[:, 0:Q_TILE] // SEL_LEN
    allowed = blk <= cur
    forced = (blk == 0) | (blk == cur) | (blk == cur - 1)
    score = jnp.where(allowed, imp + jnp.where(forced, FORCE_BONUS, 0.0), -1.0)
    rank = jnp.zeros((n_sb, Q_TILE), jnp.float32)
    for j in range(n_sb):
        sj = score[j:j + 1, :]
        rank = rank + jnp.where(sj > score, 1.0, jnp.where((sj == score) & (blk > j), 1.0, 0.0))
    chosen = jnp.where((rank < float(N_SELECT)) & (score >= 0.0), 1.0, 0.0)
    sel_ref[0:n_sb, :] = jnp.concatenate([chosen] * GQA, axis=1)

    kpos_blk = lax.broadcasted_iota(jnp.int32, (SEL_LEN, lanes), 0)
    per_chunk = KEY_CHUNK // SEL_LEN

    def sel_chunk(c, carry):
        m_run, l_run, acc = carry
        k0 = pl.multiple_of(c * KEY_CHUNK, KEY_CHUNK)
        sc = _dot_nt(ks_ref[pl.ds(k0, KEY_CHUNK), :], q)
        parts = []
        sel_rows = sel_ref[pl.ds(pl.multiple_of(c * per_chunk, per_chunk), per_chunk), :]
        for j in range(per_chunk):
            on = sel_rows[j:j + 1, :] > 0.5
            ok = on & ((k0 + j * SEL_LEN + kpos_blk) <= t_lane)
            parts.append(jnp.where(ok, sc[j * SEL_LEN:(j + 1) * SEL_LEN, :], NEG_INF))
        sc = jnp.concatenate(parts, axis=0)
        m_new = jnp.maximum(m_run, jnp.max(sc, axis=0, keepdims=True))
        scale = jnp.exp(m_run - m_new)
        p = jnp.exp(sc - m_new)
        l_new = scale * l_run + jnp.sum(p, axis=0, keepdims=True)
        acc = scale * acc + _dot(vst_ref[:, pl.ds(k0, KEY_CHUNK)], p.astype(_MM))
        return m_new, l_new, acc

    n_chunks = (q0 + Q_TILE + KEY_CHUNK - 1) // KEY_CHUNK
    init = (jnp.full((1, lanes), NEG_INF, jnp.float32), jnp.zeros((1, lanes), jnp.float32),
            jnp.zeros((HEAD_DIM, lanes), jnp.float32))
    _, l_s, acc_s = lax.fori_loop(0, n_chunks, sel_chunk, init)
    o_s = acc_s / l_s

    span = WINDOW + Q_TILE
    w0 = pl.multiple_of(jnp.maximum(q0 - WINDOW, 0), Q_TILE)
    sw = _dot_nt(kw_ref[pl.ds(w0, span), :], q)
    dist = t_lane - (w0 + lax.broadcasted_iota(jnp.int32, (span, lanes), 0))
    okw = (dist >= 0) & (dist < WINDOW)
    sw = jnp.where(okw, sw, NEG_INF)
    pw = jnp.exp(sw - jnp.max(sw, axis=0, keepdims=True))
    o_w = _dot(vwt_ref[:, pl.ds(w0, span)], pw.astype(_MM)) / jnp.sum(pw, axis=0, keepdims=True)

    gts = jax.nn.sigmoid(g_ref[...])
    out = gts[0:1, :] * o_c + gts[1:2, :] * o_s + gts[2:3, :] * o_w
    for g in range(GQA):
        o_ref[g] = out[:, g * Q_TILE:(g + 1) * Q_TILE].astype(o_ref.dtype)


def _nsa_attention(q4, g5, k_cmp, v_cmp_t, ks, vs_t, kw, vw_t):
    bsz, _, seq, _ = q4.shape
    n_grp = k_cmp.shape[2]
    n_sb = seq // SEL_LEN
    assert seq % KEY_CHUNK == 0 and seq >= WINDOW + Q_TILE
    c_start = np.arange(n_grp) * CMP_STRIDE
    s_start = np.arange(n_sb) * SEL_LEN
    overlap = ((c_start[None, :] < s_start[:, None] + SEL_LEN)
               & (c_start[None, :] + CMP_LEN > s_start[:, None])
               & (np.arange(n_grp)[None, :] < n_grp - 1)).astype(np.float32)
    kv_row = lambda last: pl.BlockSpec((None, None, last, HEAD_DIM), lambda b, h, i: (b, h, 0, 0))
    kv_col = lambda last: pl.BlockSpec((None, None, HEAD_DIM, last), lambda b, h, i: (b, h, 0, 0))
    return pl.pallas_call(
        _nsa_attn_body, grid=(bsz, N_KV_HEADS, seq // Q_TILE),
        in_specs=[pl.BlockSpec((None, GQA, Q_TILE, HEAD_DIM), lambda b, h, i: (b, h, i, 0)),
                  pl.BlockSpec((None, None, None, 3, GQA * Q_TILE), lambda b, h, i: (b, h, i, 0, 0)),
                  kv_row(n_grp), kv_col(n_grp), kv_row(seq), kv_col(seq), kv_row(seq), kv_col(seq),
                  pl.BlockSpec(overlap.shape, lambda b, h, i: (0, 0))],
        out_specs=pl.BlockSpec((None, GQA, HEAD_DIM, Q_TILE), lambda b, h, i: (b, h, 0, i)),
        out_shape=jax.ShapeDtypeStruct((bsz, N_HEADS, HEAD_DIM, seq), _MM),
        scratch_shapes=[pltpu.VMEM((max(n_sb, 8), GQA * Q_TILE), jnp.float32)],
        compiler_params=_params("arbitrary", "arbitrary", "arbitrary"), name="nsa_attention")(
            q4, g5, k_cmp, v_cmp_t, ks, vs_t, kw, vw_t, jnp.asarray(overlap))


def _nsa_mixer(x2, bsz, seq, w_in, cmp_w_k, cmp_w_v, cmp_pe_k, cmp_pe_v):
    q, k3, v3, gl = _nsa_proj(x2, seq, w_in)
    q4 = q.reshape(bsz, seq, N_HEADS, HEAD_DIM).transpose(0, 2, 1, 3)
    k3 = k3.reshape(bsz, seq, 3, N_KV_HEADS, HEAD_DIM).transpose(2, 0, 3, 1, 4)
    v3 = v3.reshape(bsz, seq, 3, N_KV_HEADS, HEAD_DIM).transpose(2, 0, 3, 1, 4)
    k_cmp, v_cmp = _compress(k3[0], v3[0], cmp_w_k, cmp_w_v, cmp_pe_k, cmp_pe_v)
    tq = seq // Q_TILE
    g5 = gl.reshape(bsz, tq, Q_TILE, N_KV_HEADS, GQA, 3).transpose(0, 3, 1, 5, 4, 2)
    g5 = g5.reshape(bsz, N_KV_HEADS, tq, 3, GQA * Q_TILE)
    tr = lambda a: a.transpose(0, 1, 3, 2).astype(_MM)
    o = _nsa_attention(q4, g5, k_cmp, tr(v_cmp), k3[1].astype(_MM), tr(v3[1]), k3[2].astype(_MM), tr(v3[2]))
    return o.transpose(0, 3, 1, 2).reshape(bsz * seq, Q_WIDTH)


def _router_body(x_ref, w_ref, b_ref, idx_ref, gate_ref, rank_ref, cnt_ref, run_ref):
    @pl.when(pl.program_id(0) == 0)
    def _():
        run_ref[...] = jnp.zeros_like(run_ref)

    logits = jnp.dot(x_ref[...], w_ref[...], preferred_element_type=jnp.float32,
                     precision=lax.Precision.HIGHEST)
    scores = jax.nn.sigmoid(logits)
    tm = scores.shape[0]
    pick = scores + b_ref[...]
    lane = lax.broadcasted_iota(jnp.int32, (tm, N_EXPERTS), 1).astype(jnp.float32)
    out_lane = lax.broadcasted_iota(jnp.int32, (tm, LANE), 1)
    idx_out = jnp.zeros((tm, LANE), jnp.int32)
    gate_out = jnp.zeros((tm, LANE), jnp.float32)
    total = jnp.zeros((tm, 1), jnp.float32)
    member = jnp.zeros((tm, N_EXPERTS), jnp.float32)
    hits = []
    for k in range(TOP_K):
        best = jnp.max(pick, axis=1, keepdims=True)
        which = jnp.min(jnp.where(pick == best, lane, float(N_EXPERTS)), axis=1, keepdims=True)
        hit = lane == which
        hits.append(hit)
        gk = jnp.sum(jnp.where(hit, scores, 0.0), axis=1, keepdims=True)
        pick = jnp.where(hit, -jnp.inf, pick)
        member = jnp.where(hit, 1.0, member)
        idx_out = jnp.where(out_lane == k, which.astype(jnp.int32), idx_out)
        gate_out = jnp.where(out_lane == k, gk, gate_out)
        total = total + gk
    idx_ref[...] = idx_out
    gate_ref[...] = gate_out / total * ROUTE_SCALE

    r_i = lax.broadcasted_iota(jnp.int32, (tm, tm), 0)
    c_i = lax.broadcasted_iota(jnp.int32, (tm, tm), 1)
    tri = jnp.where(c_i < r_i, 1.0, 0.0).astype(jnp.bfloat16)
    before = _dot(tri, member.astype(jnp.bfloat16)) + run_ref[0:1, :]
    rank_out = jnp.zeros((tm, LANE), jnp.int32)
    for k in range(TOP_K):
        rk = jnp.sum(jnp.where(hits[k], before, 0.0), axis=1, keepdims=True)
        rank_out = jnp.where(out_lane == k, rk.astype(jnp.int32), rank_out)
    rank_ref[...] = rank_out
    run = run_ref[0:1, :] + jnp.sum(member, axis=0, keepdims=True)
    run_ref[...] = jnp.broadcast_to(run, run_ref.shape)
    cnt_ref[...] = jnp.broadcast_to(run, cnt_ref.shape)


def _router(x2, router_w, router_b):
    n = x2.shape[0]
    tm = min(ROW_TILE, n)
    tile = lambda c: pl.BlockSpec((tm, c), lambda i: (i, 0))
    whole = lambda a: pl.BlockSpec(a.shape, lambda i: (0, 0))
    rb = router_b.reshape(1, N_EXPERTS)
    return pl.pallas_call(
        _router_body, grid=(n // tm,),
        in_specs=[tile(D_MODEL), whole(router_w), whole(rb)],
        out_specs=[tile(LANE), tile(LANE), tile(LANE), pl.BlockSpec((8, N_EXPERTS), lambda i: (0, 0))],
        out_shape=[jax.ShapeDtypeStruct((n, LANE), jnp.int32), jax.ShapeDtypeStruct((n, LANE), jnp.float32),
                   jax.ShapeDtypeStruct((n, LANE), jnp.int32), jax.ShapeDtypeStruct((8, N_EXPERTS), jnp.float32)],
        scratch_shapes=[pltpu.VMEM((8, N_EXPERTS), jnp.float32)],
        compiler_params=_params("arbitrary"), name="moe_router")(x2, router_w, rb)


def _dest_body(start_ref, idx_ref, rank_ref, dest_ref):
    idx = idx_ref[...]
    base = jnp.zeros_like(idx)
    for e in range(N_EXPERTS):
        base = jnp.where(idx == e, start_ref[e], base)
    dest_ref[...] = rank_ref[...] + base


def _dest_rows(pad_start, idx128, rank128):
    n = idx128.shape[0]
    tm = min(4 * ROW_TILE, n)
    spec = pl.BlockSpec((tm, LANE), lambda i, ps: (i, 0))
    grid_spec = pltpu.PrefetchScalarGridSpec(num_scalar_prefetch=1, grid=(n // tm,),
                                             in_specs=[spec, spec], out_specs=spec)
    return pl.pallas_call(_dest_body, grid_spec=grid_spec,
                          out_shape=jax.ShapeDtypeStruct((n, LANE), jnp.int32),
                          compiler_params=_params("arbitrary"), name="moe_dest")(pad_start, idx128, rank128)


def _row_copy(src, dst, sem):
    return pltpu.make_async_copy(src, dst, sem)


def _dispatch_body(dest_ref, xp_ref, xs_in_ref, xs_ref, sem):
    del xs_in_ref
    tm = xp_ref.shape[0]

    def start(r, c):
        for k in range(TOP_K):
            d = dest_ref[r * TOP_K + k]
            _row_copy(xp_ref.at[pl.ds(r, 1), :], xs_ref.at[pl.ds(d, 1), :], sem).start()
        return c

    def wait(r, c):
        for k in range(TOP_K):
            _row_copy(xp_ref.at[pl.ds(r, 1), :], xs_ref.at[pl.ds(0, 1), :], sem).wait()
        return c

    lax.fori_loop(0, tm, start, 0)
    lax.fori_loop(0, tm, wait, 0)


def _dispatch(dest_flat, xp, n_rows):
    n = xp.shape[0]
    tm = min(ROW_TILE, n)
    xs0 = jnp.zeros((n_rows, HALF), jnp.uint32)
    return pl.pallas_call(
        _dispatch_body, grid=(n // tm,),
        in_specs=[pl.BlockSpec((tm * TOP_K,), lambda i: (i,), memory_space=pltpu.SMEM),
                  pl.BlockSpec((tm, HALF), lambda i: (i, 0)),
                  pl.BlockSpec(memory_space=pl.ANY)],
        out_specs=pl.BlockSpec(memory_space=pl.ANY),
        out_shape=jax.ShapeDtypeStruct((n_rows, HALF), jnp.uint32),
        scratch_shapes=[pltpu.SemaphoreType.DMA(())],
        input_output_aliases={2: 0},
        compiler_params=_params("arbitrary"), name="moe_dispatch")(dest_flat, xp, xs0)


def _expert_body(blk_exp_ref, n_used_ref, x_ref, wg_ref, wu_ref, wd_ref, o_ref, wgu_s, wd_s):
    i = pl.program_id(0)
    prev = blk_exp_ref[jnp.maximum(i - 1, 0)]

    @pl.when((i == 0) | (blk_exp_ref[i] != prev))
    def _():
        wgu_s[:, :D_EXPERT] = wg_ref[...].astype(_MM)
        wgu_s[:, D_EXPERT:] = wu_ref[...].astype(_MM)
        wd_s[...] = wd_ref[...].astype(_MM)

    @pl.when(i < n_used_ref[0])
    def _():
        gu = _dot(_unpack_rows(x_ref[...]).astype(_MM), wgu_s[...])
        h = jax.nn.silu(gu[:, :D_EXPERT]) * gu[:, D_EXPERT:]
        o_ref[...] = _pack_rows(_dot(h.astype(_MM), wd_s[...]))

    @pl.when(i >= n_used_ref[0])
    def _():
        o_ref[...] = jnp.zeros_like(o_ref)


def _experts(xs, blk_exp, n_used, w_gate, w_up, w_down):
    n_rows = xs.shape[0]
    n_blk = n_rows // EXPERT_BLOCK
    grid_spec = pltpu.PrefetchScalarGridSpec(
        num_scalar_prefetch=2, grid=(n_blk,),
        in_specs=[pl.BlockSpec((EXPERT_BLOCK, HALF), lambda i, be, nu: (i, 0)),
                  pl.BlockSpec((None, D_MODEL, D_EXPERT), lambda i, be, nu: (be[i], 0, 0)),
                  pl.BlockSpec((None, D_MODEL, D_EXPERT), lambda i, be, nu: (be[i], 0, 0)),
                  pl.BlockSpec((None, D_EXPERT, D_MODEL), lambda i, be, nu: (be[i], 0, 0))],
        out_specs=pl.BlockSpec((EXPERT_BLOCK, HALF), lambda i, be, nu: (i, 0)),
        scratch_shapes=[pltpu.VMEM((D_MODEL, 2 * D_EXPERT), _MM), pltpu.VMEM((D_EXPERT, D_MODEL), _MM)])
    return pl.pallas_call(_expert_body, grid_spec=grid_spec,
                          out_shape=jax.ShapeDtypeStruct((n_rows, HALF), jnp.uint32),
                          compiler_params=_params("arbitrary"), name="moe_experts")(
                              blk_exp, n_used, xs, w_gate, w_up, w_down)


def _combine_body(dcur_ref, dnext_ref, x_ref, gate_ref, ys_ref, sgu_ref, sd_ref, g_ref, b_ref, o_ref, buf, sem):
    i = pl.program_id(0)
    slot = i % 2
    tm = x_ref.shape[0]

    def gather(d_ref, s):
        def body(r, c):
            for k in range(TOP_K):
                d = d_ref[r * TOP_K + k]
                _row_copy(ys_ref.at[pl.ds(d, 1), :], buf.at[s, k, pl.ds(r, 1), :], sem.at[s]).start()
            return c
        lax.fori_loop(0, tm, body, 0)

    @pl.when(i == 0)
    def _():
        gather(dcur_ref, 0)

    @pl.when(i + 1 < pl.num_programs(0))
    def _():
        gather(dnext_ref, 1 - slot)

    def wait(r, c):
        for k in range(TOP_K):
            _row_copy(ys_ref.at[pl.ds(0, 1), :], buf.at[slot, k, pl.ds(r, 1), :], sem.at[slot]).wait()
        return c

    lax.fori_loop(0, tm, wait, 0)

    x = x_ref[...]
    gate = gate_ref[...]
    routed = jnp.zeros(x.shape, jnp.float32)
    for k in range(TOP_K):
        routed = routed + gate[:, k:k + 1] * _unpack_rows(buf[slot, k])
    gu = _dot(x.astype(_MM), sgu_ref[...])
    hs = jax.nn.silu(gu[:, :D_EXPERT]) * gu[:, D_EXPERT:]
    shared = _dot(hs.astype(_MM), sd_ref[...])
    o_ref[...] = _layer_norm(ALPHA * x + (routed + shared), g_ref[...], b_ref[...])


def _combine(dest_flat, x2, gate128, ys, s_gu, s_down, ln_g, ln_b):
    n = x2.shape[0]
    tm = min(ROW_TILE, n)
    nt = n // tm
    consts = [s_gu, s_down, ln_g.reshape(1, D_MODEL), ln_b.reshape(1, D_MODEL)]
    return pl.pallas_call(
        _combine_body, grid=(nt,),
        in_specs=[pl.BlockSpec((tm * TOP_K,), lambda i: (i,), memory_space=pltpu.SMEM),
                  pl.BlockSpec((tm * TOP_K,), lambda i: (jnp.minimum(i + 1, nt - 1),), memory_space=pltpu.SMEM),
                  pl.BlockSpec((tm, D_MODEL), lambda i: (i, 0)),
                  pl.BlockSpec((tm, LANE), lambda i: (i, 0)),
                  pl.BlockSpec(memory_space=pl.ANY)]
        + [pl.BlockSpec(a.shape, lambda i: (0, 0)) for a in consts],
        out_specs=pl.BlockSpec((tm, D_MODEL), lambda i: (i, 0)),
        out_shape=jax.ShapeDtypeStruct((n, D_MODEL), jnp.float32),
        scratch_shapes=[pltpu.VMEM((2, TOP_K, tm, HALF), jnp.uint32), pltpu.SemaphoreType.DMA((2,))],
        compiler_params=_params("arbitrary"), name="moe_combine")(dest_flat, dest_flat, x2, gate128, ys, *consts)


def _moe_ln(x2, xp, router_w, router_b, w_gate, w_up, w_down, s_gate, s_up, s_down, ln_g, ln_b):
    n = x2.shape[0]
    idx128, gate128, rank128, cnt = _router(x2, router_w, router_b)
    counts = cnt[0].astype(jnp.int32)
    padded = (counts + EXPERT_BLOCK - 1) // EXPERT_BLOCK * EXPERT_BLOCK
    pad_end = jnp.cumsum(padded)
    pad_start = pad_end - padded
    n_blk = -(-(n * TOP_K) // EXPERT_BLOCK) + N_EXPERTS
    blk_row = jnp.arange(n_blk, dtype=jnp.int32) * EXPERT_BLOCK
    blk_exp = jnp.minimum(jnp.sum((pad_end[None, :] <= blk_row[:, None]).astype(jnp.int32), axis=1), N_EXPERTS - 1)
    n_used = (pad_end[-1:] // EXPERT_BLOCK).astype(jnp.int32)
    dest128 = _dest_rows(pad_start.astype(jnp.int32), idx128, rank128)
    dest_flat = dest128[:, :TOP_K].reshape(n * TOP_K)
    xs = _dispatch(dest_flat, xp, n_blk * EXPERT_BLOCK)
    ys = _experts(xs, blk_exp, n_used, w_gate, w_up, w_down)
    s_gu = jnp.concatenate([s_gate, s_up], axis=-1).astype(_MM)
    return _combine(dest_flat, x2, gate128, ys, s_gu, s_down.astype(_MM), ln_g, ln_b)


def kernel(x, ev_w_in, ev_b_in, s5_lam_re, s5_lam_im, s5_log_dt, s5_b_re, s5_b_im, s5_c_re, s5_c_im, s5_d, s5_w_glu, s5_b_glu, cv_w, cv_b, cv_ln_g, cv_ln_b, ev_w_out, od_w_in, cmp_w_k, cmp_w_v, cmp_pe_k, cmp_pe_v, od_w_out, ln1_g, ln1_b, ln2_g, ln2_b, router_w, router_b, ex_w_gate, ex_w_up, ex_w_down, sh_w_gate, sh_w_up, sh_w_down):
    bsz, seq, _ = x.shape
    h = x.reshape(bsz * seq, D_MODEL)
    for layer in range(DEPTH):
        i = layer // 2
        if layer % 2 == 0:
            mix = _even_mixer(h, bsz, seq, ev_w_in[i], ev_b_in[i], s5_lam_re[i], s5_lam_im[i], s5_log_dt[i],
                              s5_b_re[i], s5_b_im[i], s5_c_re[i], s5_c_im[i], s5_d[i].reshape(-1),
                              s5_w_glu[i], s5_b_glu[i], cv_w[i], cv_b[i], cv_ln_g[i], cv_ln_b[i])
            w_out = ev_w_out[i]
        else:
            mix = _nsa_mixer(h, bsz, seq, od_w_in[i], cmp_w_k[i], cmp_w_v[i], cmp_pe_k[i], cmp_pe_v[i])
            w_out = od_w_out[i]
        h, hp = _row_call(_proj_res_ln_body, [mix, h],
                          [w_out.astype(_MM), ln1_g[layer].reshape(1, D_MODEL), ln1_b[layer].reshape(1, D_MODEL)],
                          [(D_MODEL, jnp.float32), (HALF, jnp.uint32)], "out_proj_ln")
        h = _moe_ln(h, hp, router_w[layer], router_b[layer], ex_w_gate[layer], ex_w_up[layer], ex_w_down[layer],
                    sh_w_gate[layer], sh_w_up[layer], sh_w_down[layer], ln2_g[layer], ln2_b[layer])
    return h.reshape(bsz, seq, D_MODEL)
```

```python
import functools
import math

import numpy as np
import jax
import jax.numpy as jnp
from jax import lax
from jax.experimental import pallas as pl
from jax.experimental.pallas import tpu as pltpu

D_MODEL = 1024
DEPTH = 4
ALPHA = (2.0 * DEPTH) ** 0.25
LN_EPS = 1e-5
NEG_INF = -1e30

D_S5 = 512
S5_GROUP = 16
S5_GROUPS = 32
S5_STATE = 64
S5_LANES = S5_GROUPS * S5_STATE
D_CONV = 512
CONV_WIDTH = 31
EVEN_IN = D_S5 + 2 * D_CONV

HEAD_DIM = 64
N_HEADS = 16
N_KV_HEADS = 4
GQA = 4
CMP_LEN = 32
CMP_STRIDE = 16
SEL_LEN = 64
N_SELECT = 16
WINDOW = 512
FORCE_BONUS = 1e4
ROPE_THETA = 10000.0
Q_WIDTH = N_HEADS * HEAD_DIM
KV_WIDTH = N_KV_HEADS * HEAD_DIM
GATE_WIDTH = 3 * N_HEADS

N_EXPERTS = 64
TOP_K = 8
D_EXPERT = 256
ROUTE_SCALE = 2.5

LANE = 128
_MM = jnp.bfloat16
_VMEM_LIMIT = 56 * 1024 * 1024

ROW_TILE = 256
SCAN_ROWS = 128
SEQ_TILE = 256
CONV_HALO = 32
Q_TILE = 128
KEY_CHUNK = 1024
EXPERT_BLOCK = 256


def _dot(a, b):
    return jnp.dot(a, b, preferred_element_type=jnp.float32)


def _dot_nt(a, b):
    return lax.dot_general(a, b, (((1,), (1,)), ((), ())), preferred_element_type=jnp.float32)


def _layer_norm(x, g, b):
    mu = jnp.mean(x, axis=-1, keepdims=True)
    xc = x - mu
    var = jnp.mean(xc * xc, axis=-1, keepdims=True)
    return xc * lax.rsqrt(var + LN_EPS) * g + b


HALF = D_MODEL // 2


def _pack_rows(x):
    bits = lambda v: lax.bitcast_convert_type(v.astype(jnp.bfloat16).astype(jnp.float32), jnp.uint32)
    return (bits(x[:, HALF:]) & jnp.uint32(0xFFFF0000)) | (bits(x[:, :HALF]) >> 16)


def _unpack_rows(w):
    lo = lax.bitcast_convert_type(w << 16, jnp.float32)
    hi = lax.bitcast_convert_type(w & jnp.uint32(0xFFFF0000), jnp.float32)
    return jnp.concatenate([lo, hi], axis=1)


def _params(*sem):
    return pltpu.CompilerParams(dimension_semantics=sem, vmem_limit_bytes=_VMEM_LIMIT)


def _row_call(body, row_ins, const_ins, outs, name, tm=ROW_TILE):
    m = row_ins[0].shape[0]
    tm = min(tm, m)
    assert m % tm == 0
    in_specs = [pl.BlockSpec((tm, a.shape[1]), lambda i: (i, 0)) for a in row_ins]
    in_specs += [pl.BlockSpec(a.shape, functools.partial(lambda nd, i: (0,) * nd, a.ndim)) for a in const_ins]
    out_specs = [pl.BlockSpec((tm, c), lambda i: (i, 0)) for c, _ in outs]
    out_shape = [jax.ShapeDtypeStruct((m, c), dt) for c, dt in outs]
    return pl.pallas_call(body, grid=(m // tm,), in_specs=in_specs, out_specs=out_specs,
                          out_shape=out_shape, compiler_params=_params("arbitrary"), name=name)(
                              *row_ins, *const_ins)


def _proj_bias_body(x_ref, w_ref, b_ref, o_ref):
    o_ref[...] = _dot(x_ref[...].astype(_MM), w_ref[...]) + b_ref[...]


def _proj_res_ln_body(y_ref, res_ref, w_ref, g_ref, b_ref, o_ref, op_ref):
    mix = _dot(y_ref[...].astype(_MM), w_ref[...])
    o = _layer_norm(ALPHA * res_ref[...] + mix, g_ref[...], b_ref[...])
    o_ref[...] = o
    op_ref[...] = _pack_rows(o)


def _s5_tables(lam_re, lam_im, log_dt, b_re, b_im, c_re, c_im):
    f32 = jnp.float32
    dt = jnp.exp(log_dt.astype(f32))[:, None]
    decay = jnp.exp(lam_re * dt)
    a_re, a_im = decay * jnp.cos(lam_im * dt), decay * jnp.sin(lam_im * dt)
    den = lam_re ** 2 + lam_im ** 2
    f_re = ((a_re - 1.0) * lam_re + a_im * lam_im) / den
    f_im = (a_im * lam_re - (a_re - 1.0) * lam_im) / den
    bb_re = f_re[..., None] * b_re - f_im[..., None] * b_im
    bb_im = f_re[..., None] * b_im + f_im[..., None] * b_re

    gl = LANE // S5_GROUP
    nj = S5_GROUPS // gl
    eye = jnp.eye(gl, dtype=f32)

    def in_blocks(bb):
        t = bb.reshape(nj, gl, S5_STATE, S5_GROUP)
        t = jnp.einsum('jgph,gk->jghkp', t, eye)
        return t.reshape(nj, gl * S5_GROUP, gl * S5_STATE).astype(_MM)

    def out_blocks(c):
        t = c.reshape(nj, gl, S5_GROUP, S5_STATE)
        t = jnp.einsum('jghp,gk->jgpkh', t, eye)
        return t.reshape(nj, gl * S5_STATE, gl * S5_GROUP).astype(_MM)

    ar, ai = a_re.reshape(1, S5_LANES), a_im.reshape(1, S5_LANES)
    pows_r, pows_i = [ar], [ai]
    for _ in range(int(math.log2(SCAN_ROWS)) - 1):
        pr, pi = pows_r[-1], pows_i[-1]
        pows_r.append(pr * pr - pi * pi)
        pows_i.append(2.0 * pr * pi)
    tr, ti = ar, ai
    for k in range(int(math.log2(SCAN_ROWS))):
        pr, pi = pows_r[k], pows_i[k]
        tr, ti = (jnp.concatenate([tr, tr * pr - ti * pi], axis=0),
                  jnp.concatenate([ti, tr * pi + ti * pr], axis=0))
    nlb = S5_LANES // LANE
    to3 = lambda t: t.reshape(t.shape[0], nlb, LANE).transpose(1, 0, 2)
    return (in_blocks(bb_re), in_blocks(bb_im), out_blocks(c_re), out_blocks(c_im),
            to3(jnp.concatenate(pows_r, axis=0)), to3(jnp.concatenate(pows_i, axis=0)), to3(tr), to3(ti))


def _s5conv_body(h_ref, bre_ref, bim_ref, cre_ref, cim_ref, pwr_ref, pwi_ref, tbr_ref, tbi_ref,
                 d_ref, wglu_ref, bglu_ref, cvw_ref, cvb_ref, lng_ref, lnb_ref, o_ref,
                 st_re, st_im, xr_ref, xi_ref, hbuf):
    tile = h_ref.shape[0]
    n_steps = int(math.log2(SCAN_ROWS))
    nlb = S5_LANES // LANE
    nj = bre_ref.shape[0]
    per_j = nlb // nj

    @pl.when(pl.program_id(1) == 0)
    def _():
        st_re[...] = jnp.zeros_like(st_re)
        st_im[...] = jnp.zeros_like(st_im)
        hbuf[0:CONV_HALO, :] = jnp.zeros((CONV_HALO, D_CONV), jnp.float32)

    row = lax.broadcasted_iota(jnp.int32, (SCAN_ROWS, LANE), 0)

    for c in range(tile // SCAN_ROWS):
        r0 = c * SCAN_ROWS
        u = h_ref[r0:r0 + SCAN_ROWS, 0:D_S5]
        ub = u.astype(_MM)
        for j in range(nj):
            uj = ub[:, j * LANE:(j + 1) * LANE]
            br = _dot(uj, bre_ref[j])
            bi = _dot(uj, bim_ref[j])
            for q in range(per_j):
                xr_ref[j * per_j + q] = br[:, q * LANE:(q + 1) * LANE]
                xi_ref[j * per_j + q] = bi[:, q * LANE:(q + 1) * LANE]

        def scan_block(lb, carry):
            xr, xi = xr_ref[lb], xi_ref[lb]
            for k in range(n_steps):
                d = 1 << k
                ar = pwr_ref[lb, k:k + 1, :]
                ai = pwi_ref[lb, k:k + 1, :]
                keep = row >= d
                sr = jnp.where(keep, pltpu.roll(xr, d, 0), 0.0)
                si = jnp.where(keep, pltpu.roll(xi, d, 0), 0.0)
                xr, xi = xr + ar * sr - ai * si, xi + ar * si + ai * sr
            pr, pi = st_re[lb, 0:1, :], st_im[lb, 0:1, :]
            tr, ti = tbr_ref[lb], tbi_ref[lb]
            xr, xi = xr + tr * pr - ti * pi, xi + tr * pi + ti * pr
            xr_ref[lb] = xr
            xi_ref[lb] = xi
            st_re[lb, 0:1, :] = xr[SCAN_ROWS - 1:SCAN_ROWS, :]
            st_im[lb, 0:1, :] = xi[SCAN_ROWS - 1:SCAN_ROWS, :]
            return carry

        lax.fori_loop(0, nlb, scan_block, 0)

        ys = []
        for j in range(nj):
            xr = jnp.concatenate([xr_ref[j * per_j + q] for q in range(per_j)], axis=1).astype(_MM)
            xi = jnp.concatenate([xi_ref[j * per_j + q] for q in range(per_j)], axis=1).astype(_MM)
            ys.append(_dot(xr, cre_ref[j]) - _dot(xi, cim_ref[j]))
        y = jnp.concatenate(ys, axis=1) + d_ref[...] * u
        y = jax.nn.gelu(y)
        y = y * jax.nn.sigmoid(_dot(y.astype(_MM), wglu_ref[...]) + bglu_ref[...])
        o_ref[r0:r0 + SCAN_ROWS, 0:D_S5] = y.astype(o_ref.dtype)

    val = h_ref[:, D_S5:D_S5 + D_CONV]
    gate = h_ref[:, D_S5 + D_CONV:D_S5 + 2 * D_CONV]
    hbuf[CONV_HALO:CONV_HALO + tile, :] = val * jax.nn.sigmoid(gate)
    off = CONV_HALO - (CONV_WIDTH - 1)
    acc = jnp.zeros((tile, D_CONV), jnp.float32)
    for k in range(CONV_WIDTH):
        acc = acc + cvw_ref[k:k + 1, :] * hbuf[off + k:off + k + tile, :]
    acc = acc + cvb_ref[...]
    yb = _layer_norm(acc, lng_ref[...], lnb_ref[...])
    o_ref[:, D_S5:D_S5 + D_CONV] = (yb * jax.nn.sigmoid(yb)).astype(o_ref.dtype)
    hbuf[0:CONV_HALO, :] = hbuf[tile:tile + CONV_HALO, :]


def _s5conv(h, tables, d_skip, w_glu, b_glu, cv_w, cv_b, ln_g, ln_b):
    bsz, seq, _ = h.shape
    tile = min(SEQ_TILE, seq)
    assert seq % tile == 0 and tile % SCAN_ROWS == 0
    bre, bim, cre, cim, pwr, pwi, tbr, tbi = tables
    consts = [bre, bim, cre, cim, pwr, pwi, tbr, tbi,
              d_skip.reshape(1, D_S5), w_glu.astype(_MM), b_glu.reshape(1, D_S5),
              cv_w, cv_b.reshape(1, D_CONV), ln_g.reshape(1, D_CONV), ln_b.reshape(1, D_CONV)]
    nlb = S5_LANES // LANE
    in_specs = [pl.BlockSpec((None, tile, EVEN_IN), lambda b, l: (b, l, 0))]
    in_specs += [pl.BlockSpec(a.shape, functools.partial(lambda nd, b, l: (0,) * nd, a.ndim)) for a in consts]
    return pl.pallas_call(
        _s5conv_body, grid=(bsz, seq // tile), in_specs=in_specs,
        out_specs=pl.BlockSpec((None, tile, D_S5 + D_CONV), lambda b, l: (b, l, 0)),
        out_shape=jax.ShapeDtypeStruct((bsz, seq, D_S5 + D_CONV), _MM),
        scratch_shapes=[pltpu.VMEM((nlb, 8, LANE), jnp.float32), pltpu.VMEM((nlb, 8, LANE), jnp.float32),
                        pltpu.VMEM((nlb, SCAN_ROWS, LANE), jnp.float32),
                        pltpu.VMEM((nlb, SCAN_ROWS, LANE), jnp.float32),
                        pltpu.VMEM((tile + CONV_HALO, D_CONV), jnp.float32)],
        compiler_params=_params("arbitrary", "arbitrary"), name="s5conv")(h, *consts)


def _even_mixer(x2, bsz, seq, w_in, b_in, lam_re, lam_im, log_dt, b_re, b_im, c_re, c_im, d_skip,
                w_glu, b_glu, cv_w, cv_b, cv_ln_g, cv_ln_b):
    (h,) = _row_call(_proj_bias_body, [x2], [w_in.astype(_MM), b_in.reshape(1, EVEN_IN)],
                     [(EVEN_IN, jnp.float32)], "even_in_proj")
    tables = _s5_tables(lam_re, lam_im, log_dt, b_re, b_im, c_re, c_im)
    y = _s5conv(h.reshape(bsz, seq, EVEN_IN), tables, d_skip, w_glu, b_glu, cv_w, cv_b, cv_ln_g, cv_ln_b)
    return y.reshape(bsz * seq, D_S5 + D_CONV)


ROPE_W = Q_WIDTH + 3 * KV_WIDTH
TOK_W = ROPE_W + KV_WIDTH
GATE_ROWS = 16
LOG2E = 1.4426950408889634


def _nsa_proj_body(x_ref, w_ref, wt_ref, cos_ref, sin_ref, q_ref, k_ref, kc_ref, vc_ref, vt_ref, gt_ref):
    xb = x_ref[...].astype(_MM)
    y = _dot(xb, w_ref[...])
    r = y[:, :ROPE_W]
    reps = ROPE_W // LANE
    cos = jnp.concatenate([cos_ref[...]] * reps, axis=1)
    sin = jnp.concatenate([sin_ref[...]] * reps, axis=1)
    lane = lax.broadcasted_iota(jnp.int32, r.shape, 1)
    half = HEAD_DIM // 2
    first = (lane % HEAD_DIM) < half
    rot = jnp.where(first, -pltpu.roll(r, ROPE_W - half, 1), pltpu.roll(r, half, 1))
    r = r * cos + rot * sin
    qs = r[:, :Q_WIDTH] * (HEAD_DIM ** -0.5 * LOG2E)
    for hd in range(N_HEADS):
        q_ref[hd] = qs[:, hd * HEAD_DIM:(hd + 1) * HEAD_DIM].astype(q_ref.dtype)
    for j in range(2 * N_KV_HEADS):
        c0 = Q_WIDTH + j * HEAD_DIM
        k_ref[j] = r[:, c0:c0 + HEAD_DIM].astype(k_ref.dtype)
    for j in range(N_KV_HEADS):
        c0 = Q_WIDTH + 2 * KV_WIDTH + j * HEAD_DIM
        kc_ref[j] = r[:, c0:c0 + HEAD_DIM]
        vc_ref[j] = y[:, ROPE_W + j * HEAD_DIM:ROPE_W + (j + 1) * HEAD_DIM]
    yt = _dot_nt(wt_ref[...], xb)
    vt_ref[...] = yt[:2 * KV_WIDTH, :].astype(vt_ref.dtype)
    gt_ref[...] = yt[2 * KV_WIDTH:, :]


def _nsa_proj(x2, bsz, seq, w_in):
    tm = min(ROW_TILE, seq)
    nt = seq // tm
    sizes = [Q_WIDTH] + [KV_WIDTH] * 6 + [GATE_WIDTH]
    offs = np.cumsum([0] + sizes)
    cols = lambda i: w_in[:, offs[i]:offs[i + 1]]
    w_tok = jnp.concatenate([cols(0), cols(3), cols(5), cols(1), cols(2)], axis=1).astype(_MM)
    gcols = cols(7).reshape(D_MODEL, N_KV_HEADS, GQA, 3).transpose(0, 1, 3, 2).reshape(D_MODEL, N_KV_HEADS, 3 * GQA)
    gcols = jnp.pad(gcols, ((0, 0), (0, 0), (0, GATE_ROWS - 3 * GQA))).reshape(D_MODEL, N_KV_HEADS * GATE_ROWS)
    w_t = jnp.concatenate([cols(4), cols(6), gcols], axis=1).T.astype(_MM)
    half = HEAD_DIM // 2
    inv = ROPE_THETA ** (-jnp.arange(half, dtype=jnp.float32) / half)
    ang = jnp.arange(seq, dtype=jnp.float32)[:, None] * inv[None, :]
    cos = jnp.tile(jnp.cos(ang), (1, LANE // half))
    sin = jnp.tile(jnp.sin(ang), (1, LANE // half))
    heads = lambda nh: pl.BlockSpec((None, nh, tm, HEAD_DIM), lambda b, i: (b, 0, i, 0))
    rows_t = lambda nr: pl.BlockSpec((None, nr, tm), lambda b, i: (b, 0, i))
    n_gate = N_KV_HEADS * GATE_ROWS
    return pl.pallas_call(
        _nsa_proj_body, grid=(bsz, nt),
        in_specs=[pl.BlockSpec((tm, D_MODEL), lambda b, i: (b * nt + i, 0)),
                  pl.BlockSpec(w_tok.shape, lambda b, i: (0, 0)),
                  pl.BlockSpec(w_t.shape, lambda b, i: (0, 0)),
                  pl.BlockSpec((tm, LANE), lambda b, i: (i, 0)),
                  pl.BlockSpec((tm, LANE), lambda b, i: (i, 0))],
        out_specs=[heads(N_HEADS), heads(2 * N_KV_HEADS), heads(N_KV_HEADS), heads(N_KV_HEADS),
                   rows_t(2 * KV_WIDTH), rows_t(n_gate)],
        out_shape=[jax.ShapeDtypeStruct((bsz, N_HEADS, seq, HEAD_DIM), _MM),
                   jax.ShapeDtypeStruct((bsz, 2 * N_KV_HEADS, seq, HEAD_DIM), _MM),
                   jax.ShapeDtypeStruct((bsz, N_KV_HEADS, seq, HEAD_DIM), jnp.float32),
                   jax.ShapeDtypeStruct((bsz, N_KV_HEADS, seq, HEAD_DIM), jnp.float32),
                   jax.ShapeDtypeStruct((bsz, 2 * KV_WIDTH, seq), _MM),
                   jax.ShapeDtypeStruct((bsz, n_gate, seq), jnp.float32)],
        compiler_params=_params("arbitrary", "arbitrary"), name="nsa_in_proj")(x2, w_tok, w_t, cos, sin)


def _compress_body(k_ref, v_ref, pk_ref, pv_ref, wk_ref, wvt_ref, ko_ref, vot_ref):
    rows = k_ref.shape[0]
    xk = k_ref[...]
    lo = _dot((xk + pk_ref[0:1, :]).astype(_MM), wk_ref[0])
    hi = _dot((xk + pk_ref[1:2, :]).astype(_MM), wk_ref[1])
    ko_ref[...] = (lo + pltpu.roll(hi, rows - 1, 0)).astype(ko_ref.dtype)
    xv = v_ref[...]
    lo_t = _dot_nt(wvt_ref[0], (xv + pv_ref[0:1, :]).astype(_MM))
    hi_t = _dot_nt(wvt_ref[1], (xv + pv_ref[1:2, :]).astype(_MM))
    vot_ref[...] = (lo_t + pltpu.roll(hi_t, rows - 1, 1)).astype(vot_ref.dtype)


def _compress(kc, vc, cmp_w_k, cmp_w_v, cmp_pe_k, cmp_pe_v):
    bsz, kvh, seq, _ = kc.shape
    grp = seq // CMP_STRIDE
    flat = CMP_STRIDE * HEAD_DIM
    k2 = kc.reshape(bsz * kvh * grp, flat)
    v2 = vc.reshape(bsz * kvh * grp, flat)
    consts = [cmp_pe_k.reshape(2, flat), cmp_pe_v.reshape(2, flat),
              cmp_w_k.reshape(2, flat, HEAD_DIM).astype(_MM),
              cmp_w_v.reshape(2, flat, HEAD_DIM).transpose(0, 2, 1).astype(_MM)]
    ko, vot = pl.pallas_call(
        _compress_body, grid=(bsz * kvh,),
        in_specs=[pl.BlockSpec((grp, flat), lambda i: (i, 0)), pl.BlockSpec((grp, flat), lambda i: (i, 0))]
        + [pl.BlockSpec(a.shape, functools.partial(lambda nd, i: (0,) * nd, a.ndim)) for a in consts],
        out_specs=[pl.BlockSpec((grp, HEAD_DIM), lambda i: (i, 0)),
                   pl.BlockSpec((None, HEAD_DIM, grp), lambda i: (i, 0, 0))],
        out_shape=[jax.ShapeDtypeStruct((bsz * kvh * grp, HEAD_DIM), _MM),
                   jax.ShapeDtypeStruct((bsz * kvh, HEAD_DIM, grp), _MM)],
        compiler_params=_params("arbitrary"), name="nsa_compress")(k2, v2, *consts)
    return ko.reshape(bsz, kvh, grp, HEAD_DIM), vot.reshape(bsz, kvh, HEAD_DIM, grp)


def _nsa_attn_body(q_ref, g_ref, kc_ref, vct_ref, ks_ref, vst_ref, kw_ref, vwt_ref, ov_ref, wm_ref, o_ref, sel_ref):
    qi = pl.program_id(2)
    q0 = qi * Q_TILE
    lanes = GQA * Q_TILE
    n_grp = kc_ref.shape[0]
    n_sb = ov_ref.shape[0]

    q = q_ref[...].reshape(lanes, HEAD_DIM)
    t_lane = q0 + (lax.broadcasted_iota(jnp.int32, (1, lanes), 1) % Q_TILE)

    s = _dot_nt(kc_ref[...], q)
    n_idx = lax.broadcasted_iota(jnp.int32, (n_grp, lanes), 0)
    vis = (n_idx * CMP_STRIDE + (CMP_LEN - 1)) <= t_lane
    s = jnp.where(vis, s, NEG_INF)
    e = jnp.where(vis, jnp.exp2(s - jnp.max(s, axis=0, keepdims=True)), 0.0)
    den = jnp.sum(e, axis=0, keepdims=True)
    p_c = e / jnp.where(den > 0.0, den, 1.0)
    o_c = _dot(vct_ref[...], p_c.astype(_MM))

    psum = p_c[:, 0:Q_TILE]
    for g in range(1, GQA):
        psum = psum + p_c[:, g * Q_TILE:(g + 1) * Q_TILE]
    imp = jnp.dot(ov_ref[...], psum, preferred_element_type=jnp.float32,
                  precision=lax.Precision.HIGHEST)
    blk = lax.broadcasted_iota(jnp.int32, (n_sb, Q_TILE), 0)
    cur = t_lane[:, 0:Q_TILE] // SEL_LEN
    allowed = blk <= cur
    forced = (blk == 0) | (blk == cur) | (blk == cur - 1)
    score = jnp.where(allowed, imp + jnp.where(forced, FORCE_BONUS, 0.0), -1.0)
    rank = jnp.zeros((n_sb, Q_TILE), jnp.float32)
    for j in range(n_sb):
        sj = score[j:j + 1, :]
        rank = rank + jnp.where(sj > score, 1.0, jnp.where((sj == score) & (blk > j), 1.0, 0.0))
    chosen = jnp.where((rank < float(N_SELECT)) & (score >= 0.0), 1.0, 0.0)
    sel_ref[0:n_sb, :] = jnp.concatenate([chosen] * GQA, axis=1)

    kpos_blk = lax.broadcasted_iota(jnp.int32, (SEL_LEN, lanes), 0)
    per_chunk = KEY_CHUNK // SEL_LEN

    def sel_chunk(c, carry, causal):
        m_run, l_run, acc = carry
        k0 = pl.multiple_of(c * KEY_CHUNK, KEY_CHUNK)
        sc = _dot_nt(ks_ref[pl.ds(k0, KEY_CHUNK), :], q)
        parts = []
        sel_rows = sel_ref[pl.ds(pl.multiple_of(c * per_chunk, per_chunk), per_chunk), :]
        for j in range(per_chunk):
            ok = sel_rows[j:j + 1, :] > 0.5
            if causal:
                ok = ok & ((k0 + j * SEL_LEN + kpos_blk) <= t_lane)
            parts.append(jnp.where(ok, sc[j * SEL_LEN:(j + 1) * SEL_LEN, :], NEG_INF))
        sc = jnp.concatenate(parts, axis=0)
        m_new = jnp.maximum(m_run, jnp.max(sc, axis=0, keepdims=True))
        scale = jnp.exp2(m_run - m_new)
        p = jnp.exp2(sc - m_new)
        l_new = scale * l_run + jnp.sum(p, axis=0, keepdims=True)
        acc = scale * acc + _dot(vst_ref[:, pl.ds(k0, KEY_CHUNK)], p.astype(_MM))
        return m_new, l_new, acc

    n_full = q0 // KEY_CHUNK
    init = (jnp.full((1, lanes), NEG_INF, jnp.float32), jnp.zeros((1, lanes), jnp.float32),
            jnp.zeros((HEAD_DIM, lanes), jnp.float32))
    carry = lax.fori_loop(0, n_full, functools.partial(sel_chunk, causal=False), init)
    _, l_s, acc_s = sel_chunk(n_full, carry, causal=True)
    o_s = acc_s / l_s

    span = WINDOW + Q_TILE
    back = jnp.minimum(q0, WINDOW)
    w0 = pl.multiple_of(q0 - back, Q_TILE)
    m0 = pl.multiple_of(WINDOW - back, Q_TILE)
    sw = _dot_nt(kw_ref[pl.ds(w0, span), :], q) + wm_ref[pl.ds(m0, span), :]
    pw = jnp.exp2(sw - jnp.max(sw, axis=0, keepdims=True))
    o_w = _dot(vwt_ref[:, pl.ds(w0, span)], pw.astype(_MM)) / jnp.sum(pw, axis=0, keepdims=True)

    gs = jax.nn.sigmoid(g_ref[...])
    gate = lambda c: jnp.concatenate([gs[c * GQA + g:c * GQA + g + 1, :] for g in range(GQA)], axis=1)
    out = gate(0) * o_c + gate(1) * o_s + gate(2) * o_w
    for g in range(GQA):
        o_ref[g] = out[:, g * Q_TILE:(g + 1) * Q_TILE].astype(o_ref.dtype)


def _nsa_attention(q4, g_t, k_cmp, v_cmp_t, k8, v_t):
    bsz, _, seq, _ = q4.shape
    n_grp = k_cmp.shape[2]
    n_sb = seq // SEL_LEN
    lanes = GQA * Q_TILE
    assert seq % KEY_CHUNK == 0 and seq >= WINDOW + Q_TILE and n_sb % (KEY_CHUNK // SEL_LEN) == 0
    c_start = np.arange(n_grp) * CMP_STRIDE
    s_start = np.arange(n_sb) * SEL_LEN
    overlap = ((c_start[None, :] < s_start[:, None] + SEL_LEN)
               & (c_start[None, :] + CMP_LEN > s_start[:, None])
               & (np.arange(n_grp)[None, :] < n_grp - 1)).astype(np.float32)
    u = np.arange(2 * WINDOW + Q_TILE)[:, None]
    ql = (np.arange(lanes) % Q_TILE)[None, :]
    wmask = np.where((u > ql) & (u <= WINDOW + ql), 0.0, NEG_INF).astype(np.float32)
    nkv = N_KV_HEADS
    return pl.pallas_call(
        _nsa_attn_body, grid=(bsz, nkv, seq // Q_TILE),
        in_specs=[pl.BlockSpec((None, GQA, Q_TILE, HEAD_DIM), lambda b, h, i: (b, h, i, 0)),
                  pl.BlockSpec((None, GATE_ROWS, Q_TILE), lambda b, h, i: (b, h, i)),
                  pl.BlockSpec((None, None, n_grp, HEAD_DIM), lambda b, h, i: (b, h, 0, 0)),
                  pl.BlockSpec((None, None, HEAD_DIM, n_grp), lambda b, h, i: (b, h, 0, 0)),
                  pl.BlockSpec((None, None, seq, HEAD_DIM), lambda b, h, i: (b, h, 0, 0)),
                  pl.BlockSpec((None, HEAD_DIM, seq), lambda b, h, i: (b, h, 0)),
                  pl.BlockSpec((None, None, seq, HEAD_DIM), lambda b, h, i: (b, nkv + h, 0, 0)),
                  pl.BlockSpec((None, HEAD_DIM, seq), lambda b, h, i: (b, nkv + h, 0)),
                  pl.BlockSpec(overlap.shape, lambda b, h, i: (0, 0)),
                  pl.BlockSpec(wmask.shape, lambda b, h, i: (0, 0))],
        out_specs=pl.BlockSpec((None, GQA, HEAD_DIM, Q_TILE), lambda b, h, i: (b, h, 0, i)),
        out_shape=jax.ShapeDtypeStruct((bsz, N_HEADS, HEAD_DIM, seq), _MM),
        scratch_shapes=[pltpu.VMEM((n_sb, lanes), jnp.float32)],
        compiler_params=_params("arbitrary", "arbitrary", "arbitrary"), name="nsa_attention")(
            q4, g_t, k_cmp, v_cmp_t, k8, v_t, k8, v_t, jnp.asarray(overlap), jnp.asarray(wmask))


def _nsa_mixer(x2, bsz, seq, w_in, cmp_w_k, cmp_w_v, cmp_pe_k, cmp_pe_v):
    q4, k8, kc4, vc4, v_t, g_t = _nsa_proj(x2, bsz, seq, w_in)
    k_cmp, v_cmp_t = _compress(kc4, vc4, cmp_w_k, cmp_w_v, cmp_pe_k, cmp_pe_v)
    o = _nsa_attention(q4, g_t, k_cmp, v_cmp_t, k8, v_t)
    return o.reshape(bsz, Q_WIDTH, seq)


def _proj_t_res_ln_body(yt_ref, res_ref, w_ref, g_ref, b_ref, o_ref, op_ref):
    mix = lax.dot_general(yt_ref[...], w_ref[...], (((0,), (0,)), ((), ())), preferred_element_type=jnp.float32)
    o = _layer_norm(ALPHA * res_ref[...] + mix, g_ref[...], b_ref[...])
    o_ref[...] = o
    op_ref[...] = _pack_rows(o)


def _proj_t_res_ln(y_t, res, w, g, b):
    bsz, _, seq = y_t.shape
    tm = min(ROW_TILE, seq)
    nt = seq // tm
    consts = [w.astype(_MM), g.reshape(1, D_MODEL), b.reshape(1, D_MODEL)]
    row = lambda c: pl.BlockSpec((tm, c), lambda bi, i: (bi * nt + i, 0))
    return pl.pallas_call(
        _proj_t_res_ln_body, grid=(bsz, nt),
        in_specs=[pl.BlockSpec((None, Q_WIDTH, tm), lambda bi, i: (bi, 0, i)), row(D_MODEL)]
        + [pl.BlockSpec(a.shape, lambda bi, i: (0, 0)) for a in consts],
        out_specs=[row(D_MODEL), row(HALF)],
        out_shape=[jax.ShapeDtypeStruct((bsz * seq, D_MODEL), jnp.float32),
                   jax.ShapeDtypeStruct((bsz * seq, HALF), jnp.uint32)],
        compiler_params=_params("arbitrary", "arbitrary"), name="out_proj_t_ln")(y_t, res, *consts)


def _router_body(x_ref, w_ref, b_ref, idx_ref, gate_ref, rank_ref, cnt_ref, run_ref):
    @pl.when(pl.program_id(0) == 0)
    def _():
        run_ref[...] = jnp.zeros_like(run_ref)

    logits = jnp.dot(x_ref[...], w_ref[...], preferred_element_type=jnp.float32,
                     precision=lax.Precision.HIGHEST)
    scores = jax.nn.sigmoid(logits)
    tm = scores.shape[0]
    pick = scores + b_ref[...]
    lane = lax.broadcasted_iota(jnp.int32, (tm, N_EXPERTS), 1).astype(jnp.float32)
    out_lane = lax.broadcasted_iota(jnp.int32, (tm, LANE), 1)
    idx_out = jnp.zeros((tm, LANE), jnp.int32)
    gate_out = jnp.zeros((tm, LANE), jnp.float32)
    total = jnp.zeros((tm, 1), jnp.float32)
    member = jnp.zeros((tm, N_EXPERTS), jnp.float32)
    hits = []
    for k in range(TOP_K):
        best = jnp.max(pick, axis=1, keepdims=True)
        which = jnp.min(jnp.where(pick == best, lane, float(N_EXPERTS)), axis=1, keepdims=True)
        hit = lane == which
        hits.append(hit)
        gk = jnp.sum(jnp.where(hit, scores, 0.0), axis=1, keepdims=True)
        pick = jnp.where(hit, -jnp.inf, pick)
        member = jnp.where(hit, 1.0, member)
        idx_out = jnp.where(out_lane == k, which.astype(jnp.int32), idx_out)
        gate_out = jnp.where(out_lane == k, gk, gate_out)
        total = total + gk
    idx_ref[...] = idx_out
    gate_ref[...] = gate_out / total * ROUTE_SCALE

    r_i = lax.broadcasted_iota(jnp.int32, (tm, tm), 0)
    c_i = lax.broadcasted_iota(jnp.int32, (tm, tm), 1)
    tri = jnp.where(c_i < r_i, 1.0, 0.0).astype(jnp.bfloat16)
    before = _dot(tri, member.astype(jnp.bfloat16)) + run_ref[0:1, :]
    rank_out = jnp.zeros((tm, LANE), jnp.int32)
    for k in range(TOP_K):
        rk = jnp.sum(jnp.where(hits[k], before, 0.0), axis=1, keepdims=True)
        rank_out = jnp.where(out_lane == k, rk.astype(jnp.int32), rank_out)
    rank_ref[...] = rank_out
    run = run_ref[0:1, :] + jnp.sum(member, axis=0, keepdims=True)
    run_ref[...] = jnp.broadcast_to(run, run_ref.shape)
    cnt_ref[...] = jnp.broadcast_to(run, cnt_ref.shape)


def _router(x2, router_w, router_b):
    n = x2.shape[0]
    tm = min(ROW_TILE, n)
    tile = lambda c: pl.BlockSpec((tm, c), lambda i: (i, 0))
    whole = lambda a: pl.BlockSpec(a.shape, lambda i: (0, 0))
    rb = router_b.reshape(1, N_EXPERTS)
    return pl.pallas_call(
        _router_body, grid=(n // tm,),
        in_specs=[tile(D_MODEL), whole(router_w), whole(rb)],
        out_specs=[tile(LANE), tile(LANE), tile(LANE), pl.BlockSpec((8, N_EXPERTS), lambda i: (0, 0))],
        out_shape=[jax.ShapeDtypeStruct((n, LANE), jnp.int32), jax.ShapeDtypeStruct((n, LANE), jnp.float32),
                   jax.ShapeDtypeStruct((n, LANE), jnp.int32), jax.ShapeDtypeStruct((8, N_EXPERTS), jnp.float32)],
        scratch_shapes=[pltpu.VMEM((8, N_EXPERTS), jnp.float32)],
        compiler_params=_params("arbitrary"), name="moe_router")(x2, router_w, rb)


def _dest_body(start_ref, idx_ref, rank_ref, dest_ref):
    idx = idx_ref[...]
    base = jnp.zeros_like(idx)
    for e in range(N_EXPERTS):
        base = jnp.where(idx == e, start_ref[e], base)
    dest_ref[...] = rank_ref[...] + base


def _dest_rows(pad_start, idx128, rank128):
    n = idx128.shape[0]
    tm = min(4 * ROW_TILE, n)
    spec = pl.BlockSpec((tm, LANE), lambda i, ps: (i, 0))
    grid_spec = pltpu.PrefetchScalarGridSpec(num_scalar_prefetch=1, grid=(n // tm,),
                                             in_specs=[spec, spec], out_specs=spec)
    return pl.pallas_call(_dest_body, grid_spec=grid_spec,
                          out_shape=jax.ShapeDtypeStruct((n, LANE), jnp.int32),
                          compiler_params=_params("arbitrary"), name="moe_dest")(pad_start, idx128, rank128)


def _row_copy(src, dst, sem):
    return pltpu.make_async_copy(src, dst, sem)


def _dispatch_body(dest_ref, xp_ref, xs_in_ref, xs_ref, sem):
    del xs_in_ref
    tm = xp_ref.shape[0]

    def start(r, c):
        for k in range(TOP_K):
            d = dest_ref[r * TOP_K + k]
            _row_copy(xp_ref.at[pl.ds(r, 1), :], xs_ref.at[pl.ds(d, 1), :], sem).start()
        return c

    def wait(r, c):
        for k in range(TOP_K):
            _row_copy(xp_ref.at[pl.ds(r, 1), :], xs_ref.at[pl.ds(0, 1), :], sem).wait()
        return c

    lax.fori_loop(0, tm, start, 0)
    lax.fori_loop(0, tm, wait, 0)


def _dispatch(dest_flat, xp, n_rows):
    n = xp.shape[0]
    tm = min(ROW_TILE, n)
    xs0 = jnp.zeros((n_rows, HALF), jnp.uint32)
    return pl.pallas_call(
        _dispatch_body, grid=(n // tm,),
        in_specs=[pl.BlockSpec((tm * TOP_K,), lambda i: (i,), memory_space=pltpu.SMEM),
                  pl.BlockSpec((tm, HALF), lambda i: (i, 0)),
                  pl.BlockSpec(memory_space=pl.ANY)],
        out_specs=pl.BlockSpec(memory_space=pl.ANY),
        out_shape=jax.ShapeDtypeStruct((n_rows, HALF), jnp.uint32),
        scratch_shapes=[pltpu.SemaphoreType.DMA(())],
        input_output_aliases={2: 0},
        compiler_params=_params("arbitrary"), name="moe_dispatch")(dest_flat, xp, xs0)


def _expert_body(blk_exp_ref, n_used_ref, x_ref, wg_ref, wu_ref, wd_ref, o_ref, wgu_s, wd_s):
    i = pl.program_id(0)
    prev = blk_exp_ref[jnp.maximum(i - 1, 0)]

    @pl.when((i == 0) | (blk_exp_ref[i] != prev))
    def _():
        wgu_s[:, :D_EXPERT] = wg_ref[...].astype(_MM)
        wgu_s[:, D_EXPERT:] = wu_ref[...].astype(_MM)
        wd_s[...] = wd_ref[...].astype(_MM)

    @pl.when(i < n_used_ref[0])
    def _():
        gu = _dot(_unpack_rows(x_ref[...]).astype(_MM), wgu_s[...])
        h = jax.nn.silu(gu[:, :D_EXPERT]) * gu[:, D_EXPERT:]
        o_ref[...] = _pack_rows(_dot(h.astype(_MM), wd_s[...]))

    @pl.when(i >= n_used_ref[0])
    def _():
        o_ref[...] = jnp.zeros_like(o_ref)


def _experts(xs, blk_exp, n_used, w_gate, w_up, w_down):
    n_rows = xs.shape[0]
    n_blk = n_rows // EXPERT_BLOCK
    grid_spec = pltpu.PrefetchScalarGridSpec(
        num_scalar_prefetch=2, grid=(n_blk,),
        in_specs=[pl.BlockSpec((EXPERT_BLOCK, HALF), lambda i, be, nu: (i, 0)),
                  pl.BlockSpec((None, D_MODEL, D_EXPERT), lambda i, be, nu: (be[i], 0, 0)),
                  pl.BlockSpec((None, D_MODEL, D_EXPERT), lambda i, be, nu: (be[i], 0, 0)),
                  pl.BlockSpec((None, D_EXPERT, D_MODEL), lambda i, be, nu: (be[i], 0, 0))],
        out_specs=pl.BlockSpec((EXPERT_BLOCK, HALF), lambda i, be, nu: (i, 0)),
        scratch_shapes=[pltpu.VMEM((D_MODEL, 2 * D_EXPERT), _MM), pltpu.VMEM((D_EXPERT, D_MODEL), _MM)])
    return pl.pallas_call(_expert_body, grid_spec=grid_spec,
                          out_shape=jax.ShapeDtypeStruct((n_rows, HALF), jnp.uint32),
                          compiler_params=_params("arbitrary"), name="moe_experts")(
                              blk_exp, n_used, xs, w_gate, w_up, w_down)


def _combine_body(dcur_ref, dnext_ref, x_ref, gate_ref, ys_ref, sgu_ref, sd_ref, g_ref, b_ref, o_ref, buf, sem):
    i = pl.program_id(0)
    slot = i % 2
    tm = x_ref.shape[0]

    def gather(d_ref, s):
        def body(r, c):
            for k in range(TOP_K):
                d = d_ref[r * TOP_K + k]
                _row_copy(ys_ref.at[pl.ds(d, 1), :], buf.at[s, k, pl.ds(r, 1), :], sem.at[s]).start()
            return c
        lax.fori_loop(0, tm, body, 0)

    @pl.when(i == 0)
    def _():
        gather(dcur_ref, 0)

    @pl.when(i + 1 < pl.num_programs(0))
    def _():
        gather(dnext_ref, 1 - slot)

    def wait(r, c):
        for k in range(TOP_K):
            _row_copy(ys_ref.at[pl.ds(0, 1), :], buf.at[slot, k, pl.ds(r, 1), :], sem.at[slot]).wait()
        return c

    lax.fori_loop(0, tm, wait, 0)

    x = x_ref[...]
    gate = gate_ref[...]
    routed = jnp.zeros(x.shape, jnp.float32)
    for k in range(TOP_K):
        routed = routed + gate[:, k:k + 1] * _unpack_rows(buf[slot, k])
    gu = _dot(x.astype(_MM), sgu_ref[...])
    hs = jax.nn.silu(gu[:, :D_EXPERT]) * gu[:, D_EXPERT:]
    shared = _dot(hs.astype(_MM), sd_ref[...])
    o_ref[...] = _layer_norm(ALPHA * x + (routed + shared), g_ref[...], b_ref[...])


def _combine(dest_flat, x2, gate128, ys, s_gu, s_down, ln_g, ln_b):
    n = x2.shape[0]
    tm = min(ROW_TILE, n)
    nt = n // tm
    consts = [s_gu, s_down, ln_g.reshape(1, D_MODEL), ln_b.reshape(1, D_MODEL)]
    return pl.pallas_call(
        _combine_body, grid=(nt,),
        in_specs=[pl.BlockSpec((tm * TOP_K,), lambda i: (i,), memory_space=pltpu.SMEM),
                  pl.BlockSpec((tm * TOP_K,), lambda i: (jnp.minimum(i + 1, nt - 1),), memory_space=pltpu.SMEM),
                  pl.BlockSpec((tm, D_MODEL), lambda i: (i, 0)),
                  pl.BlockSpec((tm, LANE), lambda i: (i, 0)),
                  pl.BlockSpec(memory_space=pl.ANY)]
        + [pl.BlockSpec(a.shape, lambda i: (0, 0)) for a in consts],
        out_specs=pl.BlockSpec((tm, D_MODEL), lambda i: (i, 0)),
        out_shape=jax.ShapeDtypeStruct((n, D_MODEL), jnp.float32),
        scratch_shapes=[pltpu.VMEM((2, TOP_K, tm, HALF), jnp.uint32), pltpu.SemaphoreType.DMA((2,))],
        compiler_params=_params("arbitrary"), name="moe_combine")(dest_flat, dest_flat, x2, gate128, ys, *consts)


def _moe_ln(x2, xp, router_w, router_b, w_gate, w_up, w_down, s_gate, s_up, s_down, ln_g, ln_b):
    n = x2.shape[0]
    idx128, gate128, rank128, cnt = _router(x2, router_w, router_b)
    counts = cnt[0].astype(jnp.int32)
    padded = (counts + EXPERT_BLOCK - 1) // EXPERT_BLOCK * EXPERT_BLOCK
    pad_end = jnp.cumsum(padded)
    pad_start = pad_end - padded
    n_blk = -(-(n * TOP_K) // EXPERT_BLOCK) + N_EXPERTS
    blk_row = jnp.arange(n_blk, dtype=jnp.int32) * EXPERT_BLOCK
    blk_exp = jnp.minimum(jnp.sum((pad_end[None, :] <= blk_row[:, None]).astype(jnp.int32), axis=1), N_EXPERTS - 1)
    n_used = (pad_end[-1:] // EXPERT_BLOCK).astype(jnp.int32)
    dest128 = _dest_rows(pad_start.astype(jnp.int32), idx128, rank128)
    dest_flat = dest128[:, :TOP_K].reshape(n * TOP_K)
    xs = _dispatch(dest_flat, xp, n_blk * EXPERT_BLOCK)
    ys = _experts(xs, blk_exp, n_used, w_gate, w_up, w_down)
    s_gu = jnp.concatenate([s_gate, s_up], axis=-1).astype(_MM)
    return _combine(dest_flat, x2, gate128, ys, s_gu, s_down.astype(_MM), ln_g, ln_b)


def kernel(x, ev_w_in, ev_b_in, s5_lam_re, s5_lam_im, s5_log_dt, s5_b_re, s5_b_im, s5_c_re, s5_c_im, s5_d, s5_w_glu, s5_b_glu, cv_w, cv_b, cv_ln_g, cv_ln_b, ev_w_out, od_w_in, cmp_w_k, cmp_w_v, cmp_pe_k, cmp_pe_v, od_w_out, ln1_g, ln1_b, ln2_g, ln2_b, router_w, router_b, ex_w_gate, ex_w_up, ex_w_down, sh_w_gate, sh_w_up, sh_w_down):
    bsz, seq, _ = x.shape
    h = x.reshape(bsz * seq, D_MODEL)
    for layer in range(DEPTH):
        i = layer // 2
        if layer % 2 == 0:
            mix = _even_mixer(h, bsz, seq, ev_w_in[i], ev_b_in[i], s5_lam_re[i], s5_lam_im[i], s5_log_dt[i],
                              s5_b_re[i], s5_b_im[i], s5_c_re[i], s5_c_im[i], s5_d[i].reshape(-1),
                              s5_w_glu[i], s5_b_glu[i], cv_w[i], cv_b[i], cv_ln_g[i], cv_ln_b[i])
            h, hp = _row_call(_proj_res_ln_body, [mix, h],
                              [ev_w_out[i].astype(_MM), ln1_g[layer].reshape(1, D_MODEL),
                               ln1_b[layer].reshape(1, D_MODEL)],
                              [(D_MODEL, jnp.float32), (HALF, jnp.uint32)], "out_proj_ln")
        else:
            mix_t = _nsa_mixer(h, bsz, seq, od_w_in[i], cmp_w_k[i], cmp_w_v[i], cmp_pe_k[i], cmp_pe_v[i])
            h, hp = _proj_t_res_ln(mix_t, h, od_w_out[i], ln1_g[layer], ln1_b[layer])
        h = _moe_ln(h, hp, router_w[layer], router_b[layer], ex_w_gate[layer], ex_w_up[layer], ex_w_down[layer],
                    sh_w_gate[layer], sh_w_up[layer], sh_w_down[layer], ln2_g[layer], ln2_b[layer])
    return h.reshape(bsz, seq, D_MODEL)
```

```python
import functools
import math

import numpy as np
import jax
import jax.numpy as jnp
from jax import lax
from jax.experimental import pallas as pl
from jax.experimental.pallas import tpu as pltpu

D_MODEL = 1024
DEPTH = 4
ALPHA = (2.0 * DEPTH) ** 0.25
LN_EPS = 1e-5
NEG_INF = -1e30

D_S5 = 512
S5_GROUP = 16
S5_GROUPS = 32
S5_STATE = 64
S5_LANES = S5_GROUPS * S5_STATE
D_CONV = 512
CONV_WIDTH = 31
EVEN_IN = D_S5 + 2 * D_CONV

HEAD_DIM = 64
N_HEADS = 16
N_KV_HEADS = 4
GQA = 4
CMP_LEN = 32
CMP_STRIDE = 16
SEL_LEN = 64
N_SELECT = 16
WINDOW = 512
FORCE_BONUS = 1e4
ROPE_THETA = 10000.0
Q_WIDTH = N_HEADS * HEAD_DIM
KV_WIDTH = N_KV_HEADS * HEAD_DIM
GATE_WIDTH = 3 * N_HEADS

N_EXPERTS = 64
TOP_K = 8
D_EXPERT = 256
ROUTE_SCALE = 2.5

LANE = 128
_MM = jnp.bfloat16
_VMEM_LIMIT = 56 * 1024 * 1024

ROW_TILE = 256
SCAN_ROWS = 128
SEQ_TILE = 256
CONV_HALO = 32
Q_TILE = 128
KEY_CHUNK = 1024
EXPERT_BLOCK = 256


def _dot(a, b):
    return jnp.dot(a, b, preferred_element_type=jnp.float32)


def _dot_nt(a, b):
    return lax.dot_general(a, b, (((1,), (1,)), ((), ())), preferred_element_type=jnp.float32)


def _layer_norm(x, g, b):
    mu = jnp.mean(x, axis=-1, keepdims=True)
    xc = x - mu
    var = jnp.mean(xc * xc, axis=-1, keepdims=True)
    return xc * lax.rsqrt(var + LN_EPS) * g + b


HALF = D_MODEL // 2


def _pack_rows(x):
    bits = lambda v: lax.bitcast_convert_type(v.astype(jnp.bfloat16).astype(jnp.float32), jnp.uint32)
    return (bits(x[:, HALF:]) & jnp.uint32(0xFFFF0000)) | (bits(x[:, :HALF]) >> 16)


def _unpack_rows(w):
    lo = lax.bitcast_convert_type(w << 16, jnp.float32)
    hi = lax.bitcast_convert_type(w & jnp.uint32(0xFFFF0000), jnp.float32)
    return jnp.concatenate([lo, hi], axis=1)


def _params(*sem):
    return pltpu.CompilerParams(dimension_semantics=sem, vmem_limit_bytes=_VMEM_LIMIT)


def _row_call(body, row_ins, const_ins, outs, name, tm=ROW_TILE):
    m = row_ins[0].shape[0]
    tm = min(tm, m)
    assert m % tm == 0
    in_specs = [pl.BlockSpec((tm, a.shape[1]), lambda i: (i, 0)) for a in row_ins]
    in_specs += [pl.BlockSpec(a.shape, functools.partial(lambda nd, i: (0,) * nd, a.ndim)) for a in const_ins]
    out_specs = [pl.BlockSpec((tm, c), lambda i: (i, 0)) for c, _ in outs]
    out_shape = [jax.ShapeDtypeStruct((m, c), dt) for c, dt in outs]
    return pl.pallas_call(body, grid=(m // tm,), in_specs=in_specs, out_specs=out_specs,
                          out_shape=out_shape, compiler_params=_params("arbitrary"), name=name)(
                              *row_ins, *const_ins)


def _proj_bias_body(x_ref, w_ref, b_ref, o_ref):
    o_ref[...] = _dot(x_ref[...].astype(_MM), w_ref[...]) + b_ref[...]


def _proj_res_ln_body(y_ref, res_ref, w_ref, g_ref, b_ref, o_ref):
    mix = _dot(y_ref[...].astype(_MM), w_ref[...])
    o_ref[...] = _layer_norm(ALPHA * res_ref[...] + mix, g_ref[...], b_ref[...])


def _s5_tables(lam_re, lam_im, log_dt, b_re, b_im, c_re, c_im):
    f32 = jnp.float32
    dt = jnp.exp(log_dt.astype(f32))[:, None]
    decay = jnp.exp(lam_re * dt)
    a_re, a_im = decay * jnp.cos(lam_im * dt), decay * jnp.sin(lam_im * dt)
    den = lam_re ** 2 + lam_im ** 2
    f_re = ((a_re - 1.0) * lam_re + a_im * lam_im) / den
    f_im = (a_im * lam_re - (a_re - 1.0) * lam_im) / den
    bb_re = f_re[..., None] * b_re - f_im[..., None] * b_im
    bb_im = f_re[..., None] * b_im + f_im[..., None] * b_re

    gl = LANE // S5_GROUP
    nj = S5_GROUPS // gl
    eye = jnp.eye(gl, dtype=f32)

    def in_blocks(bb):
        t = bb.reshape(nj, gl, S5_STATE, S5_GROUP)
        t = jnp.einsum('jgph,gk->jghkp', t, eye)
        return t.reshape(nj, gl * S5_GROUP, gl * S5_STATE).astype(_MM)

    def out_blocks(c):
        t = c.reshape(nj, gl, S5_GROUP, S5_STATE)
        t = jnp.einsum('jghp,gk->jgpkh', t, eye)
        return t.reshape(nj, gl * S5_STATE, gl * S5_GROUP).astype(_MM)

    ar, ai = a_re.reshape(1, S5_LANES), a_im.reshape(1, S5_LANES)
    pows_r, pows_i = [ar], [ai]
    for _ in range(int(math.log2(SCAN_ROWS)) - 1):
        pr, pi = pows_r[-1], pows_i[-1]
        pows_r.append(pr * pr - pi * pi)
        pows_i.append(2.0 * pr * pi)
    tr, ti = ar, ai
    for k in range(int(math.log2(SCAN_ROWS))):
        pr, pi = pows_r[k], pows_i[k]
        tr, ti = (jnp.concatenate([tr, tr * pr - ti * pi], axis=0),
                  jnp.concatenate([ti, tr * pi + ti * pr], axis=0))
    nlb = S5_LANES // LANE
    to3 = lambda t: t.reshape(t.shape[0], nlb, LANE).transpose(1, 0, 2)
    return (in_blocks(bb_re), in_blocks(bb_im), out_blocks(c_re), out_blocks(c_im),
            to3(jnp.concatenate(pows_r, axis=0)), to3(jnp.concatenate(pows_i, axis=0)), to3(tr), to3(ti))


def _s5conv_body(h_ref, bre_ref, bim_ref, cre_ref, cim_ref, pwr_ref, pwi_ref, tbr_ref, tbi_ref,
                 d_ref, wglu_ref, bglu_ref, cvw_ref, cvb_ref, lng_ref, lnb_ref, o_ref,
                 st_re, st_im, xr_ref, xi_ref, hbuf):
    tile = h_ref.shape[0]
    n_steps = int(math.log2(SCAN_ROWS))
    nlb = S5_LANES // LANE
    nj = bre_ref.shape[0]
    per_j = nlb // nj

    @pl.when(pl.program_id(1) == 0)
    def _():
        st_re[...] = jnp.zeros_like(st_re)
        st_im[...] = jnp.zeros_like(st_im)
        hbuf[0:CONV_HALO, :] = jnp.zeros((CONV_HALO, D_CONV), jnp.float32)

    row = lax.broadcasted_iota(jnp.int32, (SCAN_ROWS, LANE), 0)

    for c in range(tile // SCAN_ROWS):
        r0 = c * SCAN_ROWS
        u = h_ref[r0:r0 + SCAN_ROWS, 0:D_S5]
        ub = u.astype(_MM)
        for j in range(nj):
            uj = ub[:, j * LANE:(j + 1) * LANE]
            br = _dot(uj, bre_ref[j])
            bi = _dot(uj, bim_ref[j])
            for q in range(per_j):
                xr_ref[j * per_j + q] = br[:, q * LANE:(q + 1) * LANE]
                xi_ref[j * per_j + q] = bi[:, q * LANE:(q + 1) * LANE]

        def scan_block(lb, carry):
            xr, xi = xr_ref[lb], xi_ref[lb]
            for k in range(n_steps):
                d = 1 << k
                ar = pwr_ref[lb, k:k + 1, :]
                ai = pwi_ref[lb, k:k + 1, :]
                keep = row >= d
                sr = jnp.where(keep, pltpu.roll(xr, d, 0), 0.0)
                si = jnp.where(keep, pltpu.roll(xi, d, 0), 0.0)
                xr, xi = xr + ar * sr - ai * si, xi + ar * si + ai * sr
            pr, pi = st_re[lb, 0:1, :], st_im[lb, 0:1, :]
            tr, ti = tbr_ref[lb], tbi_ref[lb]
            xr, xi = xr + tr * pr - ti * pi, xi + tr * pi + ti * pr
            xr_ref[lb] = xr
            xi_ref[lb] = xi
            st_re[lb, 0:1, :] = xr[SCAN_ROWS - 1:SCAN_ROWS, :]
            st_im[lb, 0:1, :] = xi[SCAN_ROWS - 1:SCAN_ROWS, :]
            return carry

        lax.fori_loop(0, nlb, scan_block, 0)

        ys = []
        for j in range(nj):
            xr = jnp.concatenate([xr_ref[j * per_j + q] for q in range(per_j)], axis=1).astype(_MM)
            xi = jnp.concatenate([xi_ref[j * per_j + q] for q in range(per_j)], axis=1).astype(_MM)
            ys.append(_dot(xr, cre_ref[j]) - _dot(xi, cim_ref[j]))
        y = jnp.concatenate(ys, axis=1) + d_ref[...] * u
        y = jax.nn.gelu(y)
        y = y * jax.nn.sigmoid(_dot(y.astype(_MM), wglu_ref[...]) + bglu_ref[...])
        o_ref[r0:r0 + SCAN_ROWS, 0:D_S5] = y.astype(o_ref.dtype)

    val = h_ref[:, D_S5:D_S5 + D_CONV]
    gate = h_ref[:, D_S5 + D_CONV:D_S5 + 2 * D_CONV]
    hbuf[CONV_HALO:CONV_HALO + tile, :] = val * jax.nn.sigmoid(gate)
    off = CONV_HALO - (CONV_WIDTH - 1)
    acc = jnp.zeros((tile, D_CONV), jnp.float32)
    for k in range(CONV_WIDTH):
        acc = acc + cvw_ref[k:k + 1, :] * hbuf[off + k:off + k + tile, :]
    acc = acc + cvb_ref[...]
    yb = _layer_norm(acc, lng_ref[...], lnb_ref[...])
    o_ref[:, D_S5:D_S5 + D_CONV] = (yb * jax.nn.sigmoid(yb)).astype(o_ref.dtype)
    hbuf[0:CONV_HALO, :] = hbuf[tile:tile + CONV_HALO, :]


def _s5conv(h, tables, d_skip, w_glu, b_glu, cv_w, cv_b, ln_g, ln_b):
    bsz, seq, _ = h.shape
    tile = min(SEQ_TILE, seq)
    assert seq % tile == 0 and tile % SCAN_ROWS == 0
    bre, bim, cre, cim, pwr, pwi, tbr, tbi = tables
    consts = [bre, bim, cre, cim, pwr, pwi, tbr, tbi,
              d_skip.reshape(1, D_S5), w_glu.astype(_MM), b_glu.reshape(1, D_S5),
              cv_w, cv_b.reshape(1, D_CONV), ln_g.reshape(1, D_CONV), ln_b.reshape(1, D_CONV)]
    nlb = S5_LANES // LANE
    in_specs = [pl.BlockSpec((None, tile, EVEN_IN), lambda b, l: (b, l, 0))]
    in_specs += [pl.BlockSpec(a.shape, functools.partial(lambda nd, b, l: (0,) * nd, a.ndim)) for a in consts]
    return pl.pallas_call(
        _s5conv_body, grid=(bsz, seq // tile), in_specs=in_specs,
        out_specs=pl.BlockSpec((None, tile, D_S5 + D_CONV), lambda b, l: (b, l, 0)),
        out_shape=jax.ShapeDtypeStruct((bsz, seq, D_S5 + D_CONV), _MM),
        scratch_shapes=[pltpu.VMEM((nlb, 8, LANE), jnp.float32), pltpu.VMEM((nlb, 8, LANE), jnp.float32),
                        pltpu.VMEM((nlb, SCAN_ROWS, LANE), jnp.float32),
                        pltpu.VMEM((nlb, SCAN_ROWS, LANE), jnp.float32),
                        pltpu.VMEM((tile + CONV_HALO, D_CONV), jnp.float32)],
        compiler_params=_params("arbitrary", "arbitrary"), name="s5conv")(h, *consts)


def _even_mixer(x2, bsz, seq, w_in, b_in, lam_re, lam_im, log_dt, b_re, b_im, c_re, c_im, d_skip,
                w_glu, b_glu, cv_w, cv_b, cv_ln_g, cv_ln_b):
    (h,) = _row_call(_proj_bias_body, [x2], [w_in.astype(_MM), b_in.reshape(1, EVEN_IN)],
                     [(EVEN_IN, jnp.float32)], "even_in_proj")
    tables = _s5_tables(lam_re, lam_im, log_dt, b_re, b_im, c_re, c_im)
    y = _s5conv(h.reshape(bsz, seq, EVEN_IN), tables, d_skip, w_glu, b_glu, cv_w, cv_b, cv_ln_g, cv_ln_b)
    return y.reshape(bsz * seq, D_S5 + D_CONV)


ROPE_W = Q_WIDTH + 3 * KV_WIDTH
TOK_W = ROPE_W + KV_WIDTH
GATE_ROWS = 16
LOG2E = 1.4426950408889634


def _nsa_proj_body(x_ref, w_ref, wt_ref, cos_ref, sin_ref, q_ref, k_ref, kc_ref, vc_ref, vt_ref, gt_ref):
    xb = x_ref[...].astype(_MM)
    y = _dot(xb, w_ref[...])
    r = y[:, :ROPE_W]
    reps = ROPE_W // LANE
    cos = jnp.concatenate([cos_ref[...]] * reps, axis=1)
    sin = jnp.concatenate([sin_ref[...]] * reps, axis=1)
    lane = lax.broadcasted_iota(jnp.int32, r.shape, 1)
    half = HEAD_DIM // 2
    first = (lane % HEAD_DIM) < half
    rot = jnp.where(first, -pltpu.roll(r, ROPE_W - half, 1), pltpu.roll(r, half, 1))
    r = r * cos + rot * sin
    qs = r[:, :Q_WIDTH] * (HEAD_DIM ** -0.5 * LOG2E)
    for hd in range(N_HEADS):
        q_ref[hd] = qs[:, hd * HEAD_DIM:(hd + 1) * HEAD_DIM].astype(q_ref.dtype)
    for j in range(2 * N_KV_HEADS):
        c0 = Q_WIDTH + j * HEAD_DIM
        k_ref[j] = r[:, c0:c0 + HEAD_DIM].astype(k_ref.dtype)
    for j in range(N_KV_HEADS):
        c0 = Q_WIDTH + 2 * KV_WIDTH + j * HEAD_DIM
        kc_ref[j] = r[:, c0:c0 + HEAD_DIM]
        vc_ref[j] = y[:, ROPE_W + j * HEAD_DIM:ROPE_W + (j + 1) * HEAD_DIM]
    yt = _dot_nt(wt_ref[...], xb)
    vt_ref[...] = yt[:2 * KV_WIDTH, :].astype(vt_ref.dtype)
    gt_ref[...] = yt[2 * KV_WIDTH:, :]


def _nsa_proj(x2, bsz, seq, w_in):
    tm = min(ROW_TILE, seq)
    nt = seq // tm
    sizes = [Q_WIDTH] + [KV_WIDTH] * 6 + [GATE_WIDTH]
    offs = np.cumsum([0] + sizes)
    cols = lambda i: w_in[:, offs[i]:offs[i + 1]]
    w_tok = jnp.concatenate([cols(0), cols(3), cols(5), cols(1), cols(2)], axis=1).astype(_MM)
    gcols = cols(7).reshape(D_MODEL, N_KV_HEADS, GQA, 3).transpose(0, 1, 3, 2).reshape(D_MODEL, N_KV_HEADS, 3 * GQA)
    gcols = jnp.pad(gcols, ((0, 0), (0, 0), (0, GATE_ROWS - 3 * GQA))).reshape(D_MODEL, N_KV_HEADS * GATE_ROWS)
    w_t = jnp.concatenate([cols(4), cols(6), gcols], axis=1).T.astype(_MM)
    half = HEAD_DIM // 2
    inv = ROPE_THETA ** (-jnp.arange(half, dtype=jnp.float32) / half)
    ang = jnp.arange(seq, dtype=jnp.float32)[:, None] * inv[None, :]
    cos = jnp.tile(jnp.cos(ang), (1, LANE // half))
    sin = jnp.tile(jnp.sin(ang), (1, LANE // half))
    heads = lambda nh: pl.BlockSpec((None, nh, tm, HEAD_DIM), lambda b, i: (b, 0, i, 0))
    rows_t = lambda nr: pl.BlockSpec((None, nr, tm), lambda b, i: (b, 0, i))
    n_gate = N_KV_HEADS * GATE_ROWS
    return pl.pallas_call(
        _nsa_proj_body, grid=(bsz, nt),
        in_specs=[pl.BlockSpec((tm, D_MODEL), lambda b, i: (b * nt + i, 0)),
                  pl.BlockSpec(w_tok.shape, lambda b, i: (0, 0)),
                  pl.BlockSpec(w_t.shape, lambda b, i: (0, 0)),
                  pl.BlockSpec((tm, LANE), lambda b, i: (i, 0)),
                  pl.BlockSpec((tm, LANE), lambda b, i: (i, 0))],
        out_specs=[heads(N_HEADS), heads(2 * N_KV_HEADS), heads(N_KV_HEADS), heads(N_KV_HEADS),
                   rows_t(2 * KV_WIDTH), rows_t(n_gate)],
        out_shape=[jax.ShapeDtypeStruct((bsz, N_HEADS, seq, HEAD_DIM), _MM),
                   jax.ShapeDtypeStruct((bsz, 2 * N_KV_HEADS, seq, HEAD_DIM), _MM),
                   jax.ShapeDtypeStruct((bsz, N_KV_HEADS, seq, HEAD_DIM), jnp.float32),
                   jax.ShapeDtypeStruct((bsz, N_KV_HEADS, seq, HEAD_DIM), jnp.float32),
                   jax.ShapeDtypeStruct((bsz, 2 * KV_WIDTH, seq), _MM),
                   jax.ShapeDtypeStruct((bsz, n_gate, seq), jnp.float32)],
        compiler_params=_params("arbitrary", "arbitrary"), name="nsa_in_proj")(x2, w_tok, w_t, cos, sin)


def _compress_body(k_ref, v_ref, pk_ref, pv_ref, wk_ref, wvt_ref, ko_ref, vot_ref):
    rows = k_ref.shape[0]
    xk = k_ref[...]
    lo = _dot((xk + pk_ref[0:1, :]).astype(_MM), wk_ref[0])
    hi = _dot((xk + pk_ref[1:2, :]).astype(_MM), wk_ref[1])
    ko_ref[...] = (lo + pltpu.roll(hi, rows - 1, 0)).astype(ko_ref.dtype)
    xv = v_ref[...]
    lo_t = _dot_nt(wvt_ref[0], (xv + pv_ref[0:1, :]).astype(_MM))
    hi_t = _dot_nt(wvt_ref[1], (xv + pv_ref[1:2, :]).astype(_MM))
    vot_ref[...] = (lo_t + pltpu.roll(hi_t, rows - 1, 1)).astype(vot_ref.dtype)


def _compress(kc, vc, cmp_w_k, cmp_w_v, cmp_pe_k, cmp_pe_v):
    bsz, kvh, seq, _ = kc.shape
    grp = seq // CMP_STRIDE
    flat = CMP_STRIDE * HEAD_DIM
    k2 = kc.reshape(bsz * kvh * grp, flat)
    v2 = vc.reshape(bsz * kvh * grp, flat)
    consts = [cmp_pe_k.reshape(2, flat), cmp_pe_v.reshape(2, flat),
              cmp_w_k.reshape(2, flat, HEAD_DIM).astype(_MM),
              cmp_w_v.reshape(2, flat, HEAD_DIM).transpose(0, 2, 1).astype(_MM)]
    ko, vot = pl.pallas_call(
        _compress_body, grid=(bsz * kvh,),
        in_specs=[pl.BlockSpec((grp, flat), lambda i: (i, 0)), pl.BlockSpec((grp, flat), lambda i: (i, 0))]
        + [pl.BlockSpec(a.shape, functools.partial(lambda nd, i: (0,) * nd, a.ndim)) for a in consts],
        out_specs=[pl.BlockSpec((grp, HEAD_DIM), lambda i: (i, 0)),
                   pl.BlockSpec((None, HEAD_DIM, grp), lambda i: (i, 0, 0))],
        out_shape=[jax.ShapeDtypeStruct((bsz * kvh * grp, HEAD_DIM), _MM),
                   jax.ShapeDtypeStruct((bsz * kvh, HEAD_DIM, grp), _MM)],
        compiler_params=_params("arbitrary"), name="nsa_compress")(k2, v2, *consts)
    return ko.reshape(bsz, kvh, grp, HEAD_DIM), vot.reshape(bsz, kvh, HEAD_DIM, grp)


def _nsa_attn_body(q_ref, g_ref, kc_ref, vct_ref, ks_ref, vst_ref, kw_ref, vwt_ref, ov_ref, wm_ref, o_ref, sel_ref):
    qi = pl.program_id(2)
    q0 = qi * Q_TILE
    lanes = GQA * Q_TILE
    n_grp = kc_ref.shape[0]
    n_sb = ov_ref.shape[0]

    q = q_ref[...].reshape(lanes, HEAD_DIM)
    t_lane = q0 + (lax.broadcasted_iota(jnp.int32, (1, lanes), 1) % Q_TILE)

    s = _dot_nt(kc_ref[...], q)
    n_idx = lax.broadcasted_iota(jnp.int32, (n_grp, lanes), 0)
    vis = (n_idx * CMP_STRIDE + (CMP_LEN - 1)) <= t_lane
    s = jnp.where(vis, s, NEG_INF)
    e = jnp.where(vis, jnp.exp2(s - jnp.max(s, axis=0, keepdims=True)), 0.0)
    den = jnp.sum(e, axis=0, keepdims=True)
    p_c = e / jnp.where(den > 0.0, den, 1.0)
    o_c = _dot(vct_ref[...], p_c.astype(_MM))

    psum = p_c[:, 0:Q_TILE]
    for g in range(1, GQA):
        psum = psum + p_c[:, g * Q_TILE:(g + 1) * Q_TILE]
    imp = jnp.dot(ov_ref[...], psum, preferred_element_type=jnp.float32,
                  precision=lax.Precision.HIGHEST)
    blk = lax.broadcasted_iota(jnp.int32, (n_sb, Q_TILE), 0)
    cur = t_lane[:, 0:Q_TILE] // SEL_LEN
    allowed = blk <= cur
    forced = (blk == 0) | (blk == cur) | (blk == cur - 1)
    score = jnp.where(allowed, imp + jnp.where(forced, FORCE_BONUS, 0.0), -1.0)
    rank = jnp.zeros((n_sb, Q_TILE), jnp.float32)
    for j in range(n_sb):
        sj = score[j:j + 1, :]
        rank = rank + jnp.where(sj > score, 1.0, jnp.where((sj == score) & (blk > j), 1.0, 0.0))
    chosen = jnp.where((rank < float(N_SELECT)) & (score >= 0.0), 1.0, 0.0)
    sel_ref[0:n_sb, :] = jnp.concatenate([chosen] * GQA, axis=1)

    kpos_blk = lax.broadcasted_iota(jnp.int32, (SEL_LEN, lanes), 0)
    per_chunk = KEY_CHUNK // SEL_LEN

    def sel_chunk(c, carry, causal):
        m_run, l_run, acc = carry
        k0 = pl.multiple_of(c * KEY_CHUNK, KEY_CHUNK)
        sc = _dot_nt(ks_ref[pl.ds(k0, KEY_CHUNK), :], q)
        parts = []
        sel_rows = sel_ref[pl.ds(pl.multiple_of(c * per_chunk, per_chunk), per_chunk), :]
        for j in range(per_chunk):
            ok = sel_rows[j:j + 1, :] > 0.5
            if causal:
                ok = ok & ((k0 + j * SEL_LEN + kpos_blk) <= t_lane)
            parts.append(jnp.where(ok, sc[j * SEL_LEN:(j + 1) * SEL_LEN, :], NEG_INF))
        sc = jnp.concatenate(parts, axis=0)
        m_new = jnp.maximum(m_run, jnp.max(sc, axis=0, keepdims=True))
        scale = jnp.exp2(m_run - m_new)
        p = jnp.exp2(sc - m_new)
        l_new = scale * l_run + jnp.sum(p, axis=0, keepdims=True)
        acc = scale * acc + _dot(vst_ref[:, pl.ds(k0, KEY_CHUNK)], p.astype(_MM))
        return m_new, l_new, acc

    n_full = q0 // KEY_CHUNK
    init = (jnp.full((1, lanes), NEG_INF, jnp.float32), jnp.zeros((1, lanes), jnp.float32),
            jnp.zeros((HEAD_DIM, lanes), jnp.float32))
    carry = lax.fori_loop(0, n_full, functools.partial(sel_chunk, causal=False), init)
    _, l_s, acc_s = sel_chunk(n_full, carry, causal=True)
    o_s = acc_s / l_s

    span = WINDOW + Q_TILE
    back = jnp.minimum(q0, WINDOW)
    w0 = pl.multiple_of(q0 - back, Q_TILE)
    m0 = pl.multiple_of(WINDOW - back, Q_TILE)
    sw = _dot_nt(kw_ref[pl.ds(w0, span), :], q) + wm_ref[pl.ds(m0, span), :]
    pw = jnp.exp2(sw - jnp.max(sw, axis=0, keepdims=True))
    o_w = _dot(vwt_ref[:, pl.ds(w0, span)], pw.astype(_MM)) / jnp.sum(pw, axis=0, keepdims=True)

    gs = jax.nn.sigmoid(g_ref[...])
    gate = lambda c: jnp.concatenate([gs[c * GQA + g:c * GQA + g + 1, :] for g in range(GQA)], axis=1)
    out = gate(0) * o_c + gate(1) * o_s + gate(2) * o_w
    for g in range(GQA):
        o_ref[g] = out[:, g * Q_TILE:(g + 1) * Q_TILE].astype(o_ref.dtype)


def _nsa_attention(q4, g_t, k_cmp, v_cmp_t, k8, v_t):
    bsz, _, seq, _ = q4.shape
    n_grp = k_cmp.shape[2]
    n_sb = seq // SEL_LEN
    lanes = GQA * Q_TILE
    assert seq % KEY_CHUNK == 0 and seq >= WINDOW + Q_TILE and n_sb % (KEY_CHUNK // SEL_LEN) == 0
    c_start = np.arange(n_grp) * CMP_STRIDE
    s_start = np.arange(n_sb) * SEL_LEN
    overlap = ((c_start[None, :] < s_start[:, None] + SEL_LEN)
               & (c_start[None, :] + CMP_LEN > s_start[:, None])
               & (np.arange(n_grp)[None, :] < n_grp - 1)).astype(np.float32)
    u = np.arange(2 * WINDOW + Q_TILE)[:, None]
    ql = (np.arange(lanes) % Q_TILE)[None, :]
    wmask = np.where((u > ql) & (u <= WINDOW + ql), 0.0, NEG_INF).astype(np.float32)
    nkv = N_KV_HEADS
    return pl.pallas_call(
        _nsa_attn_body, grid=(bsz, nkv, seq // Q_TILE),
        in_specs=[pl.BlockSpec((None, GQA, Q_TILE, HEAD_DIM), lambda b, h, i: (b, h, i, 0)),
                  pl.BlockSpec((None, GATE_ROWS, Q_TILE), lambda b, h, i: (b, h, i)),
                  pl.BlockSpec((None, None, n_grp, HEAD_DIM), lambda b, h, i: (b, h, 0, 0)),
                  pl.BlockSpec((None, None, HEAD_DIM, n_grp), lambda b, h, i: (b, h, 0, 0)),
                  pl.BlockSpec((None, None, seq, HEAD_DIM), lambda b, h, i: (b, h, 0, 0)),
                  pl.BlockSpec((None, HEAD_DIM, seq), lambda b, h, i: (b, h, 0)),
                  pl.BlockSpec((None, None, seq, HEAD_DIM), lambda b, h, i: (b, nkv + h, 0, 0)),
                  pl.BlockSpec((None, HEAD_DIM, seq), lambda b, h, i: (b, nkv + h, 0)),
                  pl.BlockSpec(overlap.shape, lambda b, h, i: (0, 0)),
                  pl.BlockSpec(wmask.shape, lambda b, h, i: (0, 0))],
        out_specs=pl.BlockSpec((None, GQA, HEAD_DIM, Q_TILE), lambda b, h, i: (b, h, 0, i)),
        out_shape=jax.ShapeDtypeStruct((bsz, N_HEADS, HEAD_DIM, seq), _MM),
        scratch_shapes=[pltpu.VMEM((n_sb, lanes), jnp.float32)],
        compiler_params=_params("arbitrary", "arbitrary", "arbitrary"), name="nsa_attention")(
            q4, g_t, k_cmp, v_cmp_t, k8, v_t, k8, v_t, jnp.asarray(overlap), jnp.asarray(wmask))


def _nsa_mixer(x2, bsz, seq, w_in, cmp_w_k, cmp_w_v, cmp_pe_k, cmp_pe_v):
    q4, k8, kc4, vc4, v_t, g_t = _nsa_proj(x2, bsz, seq, w_in)
    k_cmp, v_cmp_t = _compress(kc4, vc4, cmp_w_k, cmp_w_v, cmp_pe_k, cmp_pe_v)
    o = _nsa_attention(q4, g_t, k_cmp, v_cmp_t, k8, v_t)
    return o.reshape(bsz, Q_WIDTH, seq)


def _proj_t_res_ln_body(yt_ref, res_ref, w_ref, g_ref, b_ref, o_ref):
    mix = lax.dot_general(yt_ref[...], w_ref[...], (((0,), (0,)), ((), ())), preferred_element_type=jnp.float32)
    o_ref[...] = _layer_norm(ALPHA * res_ref[...] + mix, g_ref[...], b_ref[...])


def _proj_t_res_ln(y_t, res, w, g, b):
    bsz, _, seq = y_t.shape
    tm = min(ROW_TILE, seq)
    nt = seq // tm
    consts = [w.astype(_MM), g.reshape(1, D_MODEL), b.reshape(1, D_MODEL)]
    row = lambda c: pl.BlockSpec((tm, c), lambda bi, i: (bi * nt + i, 0))
    return pl.pallas_call(
        _proj_t_res_ln_body, grid=(bsz, nt),
        in_specs=[pl.BlockSpec((None, Q_WIDTH, tm), lambda bi, i: (bi, 0, i)), row(D_MODEL)]
        + [pl.BlockSpec(a.shape, lambda bi, i: (0, 0)) for a in consts],
        out_specs=row(D_MODEL),
        out_shape=jax.ShapeDtypeStruct((bsz * seq, D_MODEL), jnp.float32),
        compiler_params=_params("arbitrary", "arbitrary"), name="out_proj_t_ln")(y_t, res, *consts)


CHUNK = 8
SORT_ROWS = ROW_TILE * TOP_K + N_EXPERTS * CHUNK
CHUNK_SLOTS = 1024


SLOT_RADIX = 64


def _segment_bounds(cnt):
    padded = (((cnt.astype(jnp.int32) + (CHUNK - 1)) // CHUNK) * CHUNK).astype(jnp.float32)
    e_r = lax.broadcasted_iota(jnp.int32, (N_EXPERTS, N_EXPERTS), 0)
    e_c = lax.broadcasted_iota(jnp.int32, (N_EXPERTS, N_EXPERTS), 1)
    upper = jnp.where(e_r < e_c, 1.0, 0.0).astype(jnp.bfloat16)
    start = _dot(jnp.broadcast_to(padded, (8, N_EXPERTS)).astype(jnp.bfloat16), upper)[0:1, :]
    return start, start + padded


def _router_body(x_ref, w_ref, b_ref, gate_ref, slot_ref, cnt_ref):
    logits = jnp.dot(x_ref[...], w_ref[...], preferred_element_type=jnp.float32,
                     precision=lax.Precision.HIGHEST)
    scores = jax.nn.sigmoid(logits)
    tm = scores.shape[0]
    pick = scores + b_ref[...]
    lane = lax.broadcasted_iota(jnp.int32, (tm, N_EXPERTS), 1).astype(jnp.float32)
    member = jnp.zeros((tm, N_EXPERTS), jnp.float32)
    for _ in range(TOP_K):
        best = jnp.max(pick, axis=1, keepdims=True)
        which = jnp.min(jnp.where(pick == best, lane, float(N_EXPERTS)), axis=1, keepdims=True)
        hit = lane == which
        pick = jnp.where(hit, -jnp.inf, pick)
        member = jnp.where(hit, 1.0, member)
    chosen = member > 0.0
    picked = jnp.where(chosen, scores, 0.0)
    gate_ref[...] = picked / jnp.sum(picked, axis=1, keepdims=True) * ROUTE_SCALE

    r_i = lax.broadcasted_iota(jnp.int32, (tm, tm), 0)
    c_i = lax.broadcasted_iota(jnp.int32, (tm, tm), 1)
    tri = jnp.where(c_i < r_i, 1.0, 0.0).astype(jnp.bfloat16)
    ahead = _dot(tri, member.astype(jnp.bfloat16))
    cnt = jnp.sum(member, axis=0, keepdims=True)
    start, _ = _segment_bounds(cnt)
    slot_ref[...] = jnp.where(chosen, start + ahead + 1.0, 0.0)
    cnt_ref[...] = jnp.broadcast_to(cnt, cnt_ref.shape)


def _router(x2, router_w, router_b):
    n = x2.shape[0]
    tm = min(ROW_TILE, n)
    tile = lambda c: pl.BlockSpec((tm, c), lambda i: (i, 0))
    whole = lambda a: pl.BlockSpec(a.shape, lambda i: (0, 0))
    rb = router_b.reshape(1, N_EXPERTS)
    return pl.pallas_call(
        _router_body, grid=(n // tm,),
        in_specs=[tile(D_MODEL), whole(router_w), whole(rb)],
        out_specs=[tile(N_EXPERTS), tile(N_EXPERTS), pl.BlockSpec((8, N_EXPERTS), lambda i: (i, 0))],
        out_shape=[jax.ShapeDtypeStruct((n, N_EXPERTS), jnp.float32), jax.ShapeDtypeStruct((n, N_EXPERTS), jnp.float32),
                   jax.ShapeDtypeStruct((n // tm * 8, N_EXPERTS), jnp.float32)],
        compiler_params=_params("arbitrary"), name="moe_router")(x2, router_w, rb)


def _chunk_copy(src, dst, sem):
    return pltpu.make_async_copy(src, dst, sem)


def _chunk(ref, c):
    return ref.at[pl.ds(pl.multiple_of(c * CHUNK, CHUNK), CHUNK), :]


CHUNK_UNROLL = 4


def _chunk_groups(n_chunks):
    return lax.shift_right_logical(n_chunks, CHUNK_UNROLL.bit_length() - 1)


def _expert_of_row(cnt):
    start, end = _segment_bounds(cnt)
    j = lax.broadcasted_iota(jnp.int32, (SORT_ROWS, N_EXPERTS), 0).astype(jnp.float32)
    return jnp.where((j >= start) & (j < end), 1.0, 0.0).astype(jnp.bfloat16)


def _spread(per_expert, owner):
    return _dot_nt(per_expert.astype(jnp.bfloat16), owner)


def _row_matches(slot, owner):
    hi = jnp.floor(slot * (1.0 / SLOT_RADIX))
    lo = slot - hi * SLOT_RADIX
    place = _spread(hi, owner) * SLOT_RADIX + _spread(lo, owner)
    j1 = lax.broadcasted_iota(jnp.int32, place.shape, 1).astype(jnp.float32) + 1.0
    return place == j1


def _dispatch_body(nch_ref, dst_ref, x_ref, slot_ref, cnt_ref, xs_in_ref, xs_ref, sbuf, sem):
    del xs_in_ref
    at = _row_matches(slot_ref[...], _expert_of_row(cnt_ref[0:1, :]))
    perm_t = jnp.where(at, 1.0, 0.0).astype(jnp.bfloat16)
    rows = lax.dot_general(perm_t, x_ref[...].astype(jnp.bfloat16),
                           (((0,), (0,)), ((), ())), preferred_element_type=jnp.float32)
    sbuf[...] = _pack_rows(rows)
    groups = _chunk_groups(nch_ref[pl.program_id(0)])

    def start(g, carry):
        for u in range(CHUNK_UNROLL):
            c = g * CHUNK_UNROLL + u
            _chunk_copy(_chunk(sbuf, c), _chunk(xs_ref, dst_ref[c]), sem).start()
        return carry

    def wait(g, carry):
        for u in range(CHUNK_UNROLL):
            _chunk_copy(_chunk(sbuf, g * CHUNK_UNROLL + u), _chunk(xs_ref, 0), sem).wait()
        return carry

    lax.fori_loop(0, groups, start, 0)
    lax.fori_loop(0, groups, wait, 0)


def _dispatch(n_chunks, dst_flat, x2, slot, cnt, n_rows):
    n = x2.shape[0]
    tm = min(ROW_TILE, n)
    xs0 = jnp.zeros((n_rows, HALF), jnp.uint32)
    grid_spec = pltpu.PrefetchScalarGridSpec(
        num_scalar_prefetch=1, grid=(n // tm,),
        in_specs=[pl.BlockSpec((CHUNK_SLOTS,), lambda i, nc: (i,), memory_space=pltpu.SMEM),
                  pl.BlockSpec((tm, D_MODEL), lambda i, nc: (i, 0)),
                  pl.BlockSpec((tm, N_EXPERTS), lambda i, nc: (i, 0)),
                  pl.BlockSpec((8, N_EXPERTS), lambda i, nc: (i, 0)),
                  pl.BlockSpec(memory_space=pl.ANY)],
        out_specs=pl.BlockSpec(memory_space=pl.ANY),
        scratch_shapes=[pltpu.VMEM((SORT_ROWS, HALF), jnp.uint32), pltpu.SemaphoreType.DMA(())])
    return pl.pallas_call(
        _dispatch_body, grid_spec=grid_spec,
        out_shape=jax.ShapeDtypeStruct((n_rows, HALF), jnp.uint32),
        input_output_aliases={5: 0},
        compiler_params=_params("arbitrary"), name="moe_dispatch")(n_chunks, dst_flat, x2, slot, cnt, xs0)


def _expert_body(blk_exp_ref, n_used_ref, x_ref, wg_ref, wu_ref, wd_ref, o_ref, wgu_s, wd_s):
    i = pl.program_id(0)
    prev = blk_exp_ref[jnp.maximum(i - 1, 0)]

    @pl.when((i == 0) | (blk_exp_ref[i] != prev))
    def _():
        wgu_s[:, :D_EXPERT] = wg_ref[...].astype(_MM)
        wgu_s[:, D_EXPERT:] = wu_ref[...].astype(_MM)
        wd_s[...] = wd_ref[...].astype(_MM)

    @pl.when(i < n_used_ref[0])
    def _():
        gu = _dot(_unpack_rows(x_ref[...]).astype(_MM), wgu_s[...])
        h = jax.nn.silu(gu[:, :D_EXPERT]) * gu[:, D_EXPERT:]
        o_ref[...] = _pack_rows(_dot(h.astype(_MM), wd_s[...]))

    @pl.when(i >= n_used_ref[0])
    def _():
        o_ref[...] = jnp.zeros_like(o_ref)


def _experts(xs, blk_exp, n_used, w_gate, w_up, w_down):
    n_rows = xs.shape[0]
    n_blk = n_rows // EXPERT_BLOCK
    grid_spec = pltpu.PrefetchScalarGridSpec(
        num_scalar_prefetch=2, grid=(n_blk,),
        in_specs=[pl.BlockSpec((EXPERT_BLOCK, HALF), lambda i, be, nu: (i, 0)),
                  pl.BlockSpec((None, D_MODEL, D_EXPERT), lambda i, be, nu: (be[i], 0, 0)),
                  pl.BlockSpec((None, D_MODEL, D_EXPERT), lambda i, be, nu: (be[i], 0, 0)),
                  pl.BlockSpec((None, D_EXPERT, D_MODEL), lambda i, be, nu: (be[i], 0, 0))],
        out_specs=pl.BlockSpec((EXPERT_BLOCK, HALF), lambda i, be, nu: (i, 0)),
        scratch_shapes=[pltpu.VMEM((D_MODEL, 2 * D_EXPERT), _MM), pltpu.VMEM((D_EXPERT, D_MODEL), _MM)])
    return pl.pallas_call(_expert_body, grid_spec=grid_spec,
                          out_shape=jax.ShapeDtypeStruct((n_rows, HALF), jnp.uint32),
                          compiler_params=_params("arbitrary"), name="moe_experts")(
                              blk_exp, n_used, xs, w_gate, w_up, w_down)


def _combine_body(nch_ref, dcur_ref, dnext_ref, x_ref, slot_ref, cnt_ref, gate_ref, ys_ref, sgu_ref, sd_ref,
                  g_ref, b_ref, o_ref, ybuf, sem):
    i = pl.program_id(0)
    slot = i % 2

    @pl.when(i == 0)
    def _():
        ybuf[...] = jnp.zeros_like(ybuf)

    def gather(d_ref, n, s):
        def body(g, carry):
            for u in range(CHUNK_UNROLL):
                c = g * CHUNK_UNROLL + u
                _chunk_copy(_chunk(ys_ref, d_ref[c]), _chunk(ybuf.at[s], c), sem.at[s]).start()
            return carry
        lax.fori_loop(0, _chunk_groups(n), body, 0)

    @pl.when(i == 0)
    def _():
        gather(dcur_ref, nch_ref[0], 0)

    @pl.when(i + 1 < pl.num_programs(0))
    def _():
        gather(dnext_ref, nch_ref[jnp.minimum(i + 1, pl.num_programs(0) - 1)], 1 - slot)

    def wait(g, carry):
        for u in range(CHUNK_UNROLL):
            _chunk_copy(_chunk(ys_ref, 0), _chunk(ybuf.at[slot], g * CHUNK_UNROLL + u), sem.at[slot]).wait()
        return carry

    lax.fori_loop(0, _chunk_groups(nch_ref[i]), wait, 0)

    x = x_ref[...]
    owner = _expert_of_row(cnt_ref[0:1, :])
    at = _row_matches(slot_ref[...], owner)
    gate = gate_ref[...]
    g_hi = gate.astype(jnp.bfloat16).astype(jnp.float32)
    g_lo = gate - g_hi
    p_hi = jnp.where(at, _spread(g_hi, owner), 0.0).astype(jnp.bfloat16)
    p_lo = jnp.where(at, _spread(g_lo, owner), 0.0).astype(jnp.bfloat16)
    y = _unpack_rows(ybuf[slot]).astype(jnp.bfloat16)
    routed = _dot(p_hi, y) + _dot(p_lo, y)
    gu = _dot(x.astype(_MM), sgu_ref[...])
    hs = jax.nn.silu(gu[:, :D_EXPERT]) * gu[:, D_EXPERT:]
    shared = _dot(hs.astype(_MM), sd_ref[...])
    o_ref[...] = _layer_norm(ALPHA * x + (routed + shared), g_ref[...], b_ref[...])


def _combine(n_chunks, dst_flat, x2, slot, cnt, gate, ys, s_gu, s_down, ln_g, ln_b):
    n = x2.shape[0]
    tm = min(ROW_TILE, n)
    nt = n // tm
    consts = [s_gu, s_down, ln_g.reshape(1, D_MODEL), ln_b.reshape(1, D_MODEL)]
    grid_spec = pltpu.PrefetchScalarGridSpec(
        num_scalar_prefetch=1, grid=(nt,),
        in_specs=[pl.BlockSpec((CHUNK_SLOTS,), lambda i, nc: (i,), memory_space=pltpu.SMEM),
                  pl.BlockSpec((CHUNK_SLOTS,), lambda i, nc: (jnp.minimum(i + 1, nt - 1),), memory_space=pltpu.SMEM),
                  pl.BlockSpec((tm, D_MODEL), lambda i, nc: (i, 0)),
                  pl.BlockSpec((tm, N_EXPERTS), lambda i, nc: (i, 0)),
                  pl.BlockSpec((8, N_EXPERTS), lambda i, nc: (i, 0)),
                  pl.BlockSpec((tm, N_EXPERTS), lambda i, nc: (i, 0)),
                  pl.BlockSpec(memory_space=pl.ANY)]
        + [pl.BlockSpec(a.shape, lambda i, nc: (0, 0)) for a in consts],
        out_specs=pl.BlockSpec((tm, D_MODEL), lambda i, nc: (i, 0)),
        scratch_shapes=[pltpu.VMEM((2, SORT_ROWS, HALF), jnp.uint32), pltpu.SemaphoreType.DMA((2,))])
    return pl.pallas_call(
        _combine_body, grid_spec=grid_spec,
        out_shape=jax.ShapeDtypeStruct((n, D_MODEL), jnp.float32),
        compiler_params=_params("arbitrary"), name="moe_combine")(
            n_chunks, dst_flat, dst_flat, x2, slot, cnt, gate, ys, *consts)


def _moe_ln(x2, router_w, router_b, w_gate, w_up, w_down, s_gate, s_up, s_down, ln_g, ln_b):
    n = x2.shape[0]
    tm = min(ROW_TILE, n)
    nt = n // tm
    gate, slot, cnt8 = _router(x2, router_w, router_b)
    cnt = cnt8[::8].astype(jnp.int32)
    seg = (cnt + CHUNK - 1) // CHUNK * CHUNK
    loc_end = jnp.cumsum(seg, axis=1)
    loc_start = loc_end - seg
    exp_rows = jnp.sum(seg, axis=0)
    padded = (exp_rows + EXPERT_BLOCK - 1) // EXPERT_BLOCK * EXPERT_BLOCK
    pad_end = jnp.cumsum(padded)
    seg_dst = (pad_end - padded)[None, :] + jnp.cumsum(seg, axis=0) - seg
    max_rows = n * TOP_K + nt * N_EXPERTS * (CHUNK - 1)
    n_blk = -(-max_rows // EXPERT_BLOCK) + N_EXPERTS + 1
    n_real = (loc_end[:, -1] // CHUNK).astype(jnp.int32)
    n_chunks = (n_real + CHUNK_UNROLL - 1) // CHUNK_UNROLL * CHUNK_UNROLL
    c_idx = jnp.arange(SORT_ROWS // CHUNK, dtype=jnp.int32)
    c_exp = jnp.minimum(jnp.sum((loc_end[:, None, :] <= c_idx[None, :, None] * CHUNK).astype(jnp.int32), axis=2),
                        N_EXPERTS - 1)
    shift = jnp.take_along_axis(seg_dst - loc_start, c_exp, axis=1)
    dst = (shift + c_idx[None, :] * CHUNK) // CHUNK
    spare = (n_blk - 1) * (EXPERT_BLOCK // CHUNK)
    dst = jnp.where(c_idx[None, :] < n_real[:, None], dst, spare + c_idx[None, :] - n_real[:, None])
    dst_flat = jnp.pad(dst, ((0, 0), (0, CHUNK_SLOTS - dst.shape[1]))).reshape(nt * CHUNK_SLOTS).astype(jnp.int32)
    blk_row = jnp.arange(n_blk, dtype=jnp.int32) * EXPERT_BLOCK
    blk_exp = jnp.minimum(jnp.sum((pad_end[None, :] <= blk_row[:, None]).astype(jnp.int32), axis=1), N_EXPERTS - 1)
    n_used = (pad_end[-1:] // EXPERT_BLOCK).astype(jnp.int32)
    xs = _dispatch(n_chunks, dst_flat, x2, slot, cnt8, n_blk * EXPERT_BLOCK)
    ys = _experts(xs, blk_exp, n_used, w_gate, w_up, w_down)
    s_gu = jnp.concatenate([s_gate, s_up], axis=-1).astype(_MM)
    return _combine(n_chunks, dst_flat, x2, slot, cnt8, gate, ys, s_gu, s_down.astype(_MM), ln_g, ln_b)


def kernel(x, ev_w_in, ev_b_in, s5_lam_re, s5_lam_im, s5_log_dt, s5_b_re, s5_b_im, s5_c_re, s5_c_im, s5_d, s5_w_glu, s5_b_glu, cv_w, cv_b, cv_ln_g, cv_ln_b, ev_w_out, od_w_in, cmp_w_k, cmp_w_v, cmp_pe_k, cmp_pe_v, od_w_out, ln1_g, ln1_b, ln2_g, ln2_b, router_w, router_b, ex_w_gate, ex_w_up, ex_w_down, sh_w_gate, sh_w_up, sh_w_down):
    bsz, seq, _ = x.shape
    h = x.reshape(bsz * seq, D_MODEL)
    for layer in range(DEPTH):
        i = layer // 2
        if layer % 2 == 0:
            mix = _even_mixer(h, bsz, seq, ev_w_in[i], ev_b_in[i], s5_lam_re[i], s5_lam_im[i], s5_log_dt[i],
                              s5_b_re[i], s5_b_im[i], s5_c_re[i], s5_c_im[i], s5_d[i].reshape(-1),
                              s5_w_glu[i], s5_b_glu[i], cv_w[i], cv_b[i], cv_ln_g[i], cv_ln_b[i])
            (h,) = _row_call(_proj_res_ln_body, [mix, h],
                             [ev_w_out[i].astype(_MM), ln1_g[layer].reshape(1, D_MODEL),
                              ln1_b[layer].reshape(1, D_MODEL)],
                             [(D_MODEL, jnp.float32)], "out_proj_ln")
        else:
            mix_t = _nsa_mixer(h, bsz, seq, od_w_in[i], cmp_w_k[i], cmp_w_v[i], cmp_pe_k[i], cmp_pe_v[i])
            h = _proj_t_res_ln(mix_t, h, od_w_out[i], ln1_g[layer], ln1_b[layer])
        h = _moe_ln(h, router_w[layer], router_b[layer], ex_w_gate[layer], ex_w_up[layer], ex_w_down[layer],
                    sh_w_gate[layer], sh_w_up[layer], sh_w_down[layer], ln2_g[layer], ln2_b[layer])
    return h.reshape(bsz, seq, D_MODEL)
```

```python
import functools
import math

import numpy as np
import jax
import jax.numpy as jnp
from jax import lax
from jax.experimental import pallas as pl
from jax.experimental.pallas import tpu as pltpu

D_MODEL = 1024
DEPTH = 4
ALPHA = (2.0 * DEPTH) ** 0.25
LN_EPS = 1e-5
NEG_INF = -1e30

D_S5 = 512
S5_GROUP = 16
S5_GROUPS = 32
S5_STATE = 64
S5_LANES = S5_GROUPS * S5_STATE
D_CONV = 512
CONV_WIDTH = 31
EVEN_IN = D_S5 + 2 * D_CONV

HEAD_DIM = 64
N_HEADS = 16
N_KV_HEADS = 4
GQA = 4
CMP_LEN = 32
CMP_STRIDE = 16
SEL_LEN = 64
N_SELECT = 16
WINDOW = 512
FORCE_BONUS = 1e4
ROPE_THETA = 10000.0
Q_WIDTH = N_HEADS * HEAD_DIM
KV_WIDTH = N_KV_HEADS * HEAD_DIM
GATE_WIDTH = 3 * N_HEADS

N_EXPERTS = 64
TOP_K = 8
D_EXPERT = 256
ROUTE_SCALE = 2.5

LANE = 128
_MM = jnp.bfloat16
_VMEM_LIMIT = 56 * 1024 * 1024

ROW_TILE = 256
SCAN_ROWS = 128
SEQ_TILE = 256
CONV_HALO = 32
Q_TILE = 128
KEY_CHUNK = 1024
EXPERT_BLOCK = 512


def _dot(a, b):
    return jnp.dot(a, b, preferred_element_type=jnp.float32)


def _dot_nt(a, b):
    return lax.dot_general(a, b, (((1,), (1,)), ((), ())), preferred_element_type=jnp.float32)


def _layer_norm(x, g, b):
    mu = jnp.mean(x, axis=-1, keepdims=True)
    xc = x - mu
    var = jnp.mean(xc * xc, axis=-1, keepdims=True)
    return xc * lax.rsqrt(var + LN_EPS) * g + b


HALF = D_MODEL // 2


def _pack_rows(x):
    bits = lambda v: lax.bitcast_convert_type(v.astype(jnp.bfloat16).astype(jnp.float32), jnp.uint32)
    return (bits(x[:, HALF:]) & jnp.uint32(0xFFFF0000)) | (bits(x[:, :HALF]) >> 16)


def _unpack_rows(w):
    lo = lax.bitcast_convert_type(w << 16, jnp.float32)
    hi = lax.bitcast_convert_type(w & jnp.uint32(0xFFFF0000), jnp.float32)
    return jnp.concatenate([lo, hi], axis=1)


def _params(*sem):
    return pltpu.CompilerParams(dimension_semantics=sem, vmem_limit_bytes=_VMEM_LIMIT)


def _row_call(body, row_ins, const_ins, outs, name, tm=ROW_TILE):
    m = row_ins[0].shape[0]
    tm = min(tm, m)
    assert m % tm == 0
    in_specs = [pl.BlockSpec((tm, a.shape[1]), lambda i: (i, 0)) for a in row_ins]
    in_specs += [pl.BlockSpec(a.shape, functools.partial(lambda nd, i: (0,) * nd, a.ndim)) for a in const_ins]
    out_specs = [pl.BlockSpec((tm, c), lambda i: (i, 0)) for c, _ in outs]
    out_shape = [jax.ShapeDtypeStruct((m, c), dt) for c, dt in outs]
    return pl.pallas_call(body, grid=(m // tm,), in_specs=in_specs, out_specs=out_specs,
                          out_shape=out_shape, compiler_params=_params("arbitrary"), name=name)(
                              *row_ins, *const_ins)


def _proj_bias_body(x_ref, w_ref, b_ref, o_ref):
    o_ref[...] = _dot(x_ref[...].astype(_MM), w_ref[...]) + b_ref[...]


def _proj_res_ln_body(y_ref, res_ref, w_ref, g_ref, b_ref, o_ref):
    mix = _dot(y_ref[...].astype(_MM), w_ref[...])
    o_ref[...] = _layer_norm(ALPHA * res_ref[...] + mix, g_ref[...], b_ref[...])


def _s5_tables(lam_re, lam_im, log_dt, b_re, b_im, c_re, c_im):
    f32 = jnp.float32
    dt = jnp.exp(log_dt.astype(f32))[:, None]
    decay = jnp.exp(lam_re * dt)
    a_re, a_im = decay * jnp.cos(lam_im * dt), decay * jnp.sin(lam_im * dt)
    den = lam_re ** 2 + lam_im ** 2
    f_re = ((a_re - 1.0) * lam_re + a_im * lam_im) / den
    f_im = (a_im * lam_re - (a_re - 1.0) * lam_im) / den
    bb_re = f_re[..., None] * b_re - f_im[..., None] * b_im
    bb_im = f_re[..., None] * b_im + f_im[..., None] * b_re

    gl = LANE // S5_GROUP
    nj = S5_GROUPS // gl
    eye = jnp.eye(gl, dtype=f32)

    def in_blocks(bb):
        t = bb.reshape(nj, gl, S5_STATE, S5_GROUP)
        t = jnp.einsum('jgph,gk->jghkp', t, eye)
        return t.reshape(nj, gl * S5_GROUP, gl * S5_STATE).astype(_MM)

    def out_blocks(c):
        t = c.reshape(nj, gl, S5_GROUP, S5_STATE)
        t = jnp.einsum('jghp,gk->jgpkh', t, eye)
        return t.reshape(nj, gl * S5_STATE, gl * S5_GROUP).astype(_MM)

    ar, ai = a_re.reshape(1, S5_LANES), a_im.reshape(1, S5_LANES)
    pows_r, pows_i = [ar], [ai]
    for _ in range(int(math.log2(SCAN_ROWS)) - 1):
        pr, pi = pows_r[-1], pows_i[-1]
        pows_r.append(pr * pr - pi * pi)
        pows_i.append(2.0 * pr * pi)
    tr, ti = ar, ai
    for k in range(int(math.log2(SCAN_ROWS))):
        pr, pi = pows_r[k], pows_i[k]
        tr, ti = (jnp.concatenate([tr, tr * pr - ti * pi], axis=0),
                  jnp.concatenate([ti, tr * pi + ti * pr], axis=0))
    nlb = S5_LANES // LANE
    to3 = lambda t: t.reshape(t.shape[0], nlb, LANE).transpose(1, 0, 2)
    return (in_blocks(bb_re), in_blocks(bb_im), out_blocks(c_re), out_blocks(c_im),
            to3(jnp.concatenate(pows_r, axis=0)), to3(jnp.concatenate(pows_i, axis=0)), to3(tr), to3(ti))


def _s5conv_body(h_ref, bre_ref, bim_ref, cre_ref, cim_ref, pwr_ref, pwi_ref, tbr_ref, tbi_ref,
                 d_ref, wglu_ref, bglu_ref, cvw_ref, cvb_ref, lng_ref, lnb_ref, o_ref,
                 st_re, st_im, xr_ref, xi_ref, hbuf):
    tile = h_ref.shape[0]
    n_steps = int(math.log2(SCAN_ROWS))
    nlb = S5_LANES // LANE
    nj = bre_ref.shape[0]
    per_j = nlb // nj

    @pl.when(pl.program_id(1) == 0)
    def _():
        st_re[...] = jnp.zeros_like(st_re)
        st_im[...] = jnp.zeros_like(st_im)
        hbuf[0:CONV_HALO, :] = jnp.zeros((CONV_HALO, D_CONV), jnp.float32)

    row = lax.broadcasted_iota(jnp.int32, (SCAN_ROWS, LANE), 0)

    for c in range(tile // SCAN_ROWS):
        r0 = c * SCAN_ROWS
        u = h_ref[r0:r0 + SCAN_ROWS, 0:D_S5]
        ub = u.astype(_MM)
        for j in range(nj):
            uj = ub[:, j * LANE:(j + 1) * LANE]
            br = _dot(uj, bre_ref[j])
            bi = _dot(uj, bim_ref[j])
            for q in range(per_j):
                xr_ref[j * per_j + q] = br[:, q * LANE:(q + 1) * LANE]
                xi_ref[j * per_j + q] = bi[:, q * LANE:(q + 1) * LANE]

        def scan_block(lb, carry):
            xr, xi = xr_ref[lb], xi_ref[lb]
            for k in range(n_steps):
                d = 1 << k
                ar = pwr_ref[lb, k:k + 1, :]
                ai = pwi_ref[lb, k:k + 1, :]
                keep = row >= d
                sr = jnp.where(keep, pltpu.roll(xr, d, 0), 0.0)
                si = jnp.where(keep, pltpu.roll(xi, d, 0), 0.0)
                xr, xi = xr + ar * sr - ai * si, xi + ar * si + ai * sr
            pr, pi = st_re[lb, 0:1, :], st_im[lb, 0:1, :]
            tr, ti = tbr_ref[lb], tbi_ref[lb]
            xr, xi = xr + tr * pr - ti * pi, xi + tr * pi + ti * pr
            xr_ref[lb] = xr
            xi_ref[lb] = xi
            st_re[lb, 0:1, :] = xr[SCAN_ROWS - 1:SCAN_ROWS, :]
            st_im[lb, 0:1, :] = xi[SCAN_ROWS - 1:SCAN_ROWS, :]
            return carry

        lax.fori_loop(0, nlb, scan_block, 0)

        ys = []
        for j in range(nj):
            xr = jnp.concatenate([xr_ref[j * per_j + q] for q in range(per_j)], axis=1).astype(_MM)
            xi = jnp.concatenate([xi_ref[j * per_j + q] for q in range(per_j)], axis=1).astype(_MM)
            ys.append(_dot(xr, cre_ref[j]) - _dot(xi, cim_ref[j]))
        y = jnp.concatenate(ys, axis=1) + d_ref[...] * u
        y = jax.nn.gelu(y)
        y = y * jax.nn.sigmoid(_dot(y.astype(_MM), wglu_ref[...]) + bglu_ref[...])
        o_ref[r0:r0 + SCAN_ROWS, 0:D_S5] = y.astype(o_ref.dtype)

    val = h_ref[:, D_S5:D_S5 + D_CONV]
    gate = h_ref[:, D_S5 + D_CONV:D_S5 + 2 * D_CONV]
    hbuf[CONV_HALO:CONV_HALO + tile, :] = val * jax.nn.sigmoid(gate)
    off = CONV_HALO - (CONV_WIDTH - 1)
    acc = jnp.zeros((tile, D_CONV), jnp.float32)
    for k in range(CONV_WIDTH):
        acc = acc + cvw_ref[k:k + 1, :] * hbuf[off + k:off + k + tile, :]
    acc = acc + cvb_ref[...]
    yb = _layer_norm(acc, lng_ref[...], lnb_ref[...])
    o_ref[:, D_S5:D_S5 + D_CONV] = (yb * jax.nn.sigmoid(yb)).astype(o_ref.dtype)
    hbuf[0:CONV_HALO, :] = hbuf[tile:tile + CONV_HALO, :]


def _s5conv(h, tables, d_skip, w_glu, b_glu, cv_w, cv_b, ln_g, ln_b):
    bsz, seq, _ = h.shape
    tile = min(SEQ_TILE, seq)
    assert seq % tile == 0 and tile % SCAN_ROWS == 0
    bre, bim, cre, cim, pwr, pwi, tbr, tbi = tables
    consts = [bre, bim, cre, cim, pwr, pwi, tbr, tbi,
              d_skip.reshape(1, D_S5), w_glu.astype(_MM), b_glu.reshape(1, D_S5),
              cv_w, cv_b.reshape(1, D_CONV), ln_g.reshape(1, D_CONV), ln_b.reshape(1, D_CONV)]
    nlb = S5_LANES // LANE
    in_specs = [pl.BlockSpec((None, tile, EVEN_IN), lambda b, l: (b, l, 0))]
    in_specs += [pl.BlockSpec(a.shape, functools.partial(lambda nd, b, l: (0,) * nd, a.ndim)) for a in consts]
    return pl.pallas_call(
        _s5conv_body, grid=(bsz, seq // tile), in_specs=in_specs,
        out_specs=pl.BlockSpec((None, tile, D_S5 + D_CONV), lambda b, l: (b, l, 0)),
        out_shape=jax.ShapeDtypeStruct((bsz, seq, D_S5 + D_CONV), _MM),
        scratch_shapes=[pltpu.VMEM((nlb, 8, LANE), jnp.float32), pltpu.VMEM((nlb, 8, LANE), jnp.float32),
                        pltpu.VMEM((nlb, SCAN_ROWS, LANE), jnp.float32),
                        pltpu.VMEM((nlb, SCAN_ROWS, LANE), jnp.float32),
                        pltpu.VMEM((tile + CONV_HALO, D_CONV), jnp.float32)],
        compiler_params=_params("arbitrary", "arbitrary"), name="s5conv")(h, *consts)


def _even_mixer(x2, bsz, seq, w_in, b_in, lam_re, lam_im, log_dt, b_re, b_im, c_re, c_im, d_skip,
                w_glu, b_glu, cv_w, cv_b, cv_ln_g, cv_ln_b):
    (h,) = _row_call(_proj_bias_body, [x2], [w_in.astype(_MM), b_in.reshape(1, EVEN_IN)],
                     [(EVEN_IN, jnp.float32)], "even_in_proj")
    tables = _s5_tables(lam_re, lam_im, log_dt, b_re, b_im, c_re, c_im)
    y = _s5conv(h.reshape(bsz, seq, EVEN_IN), tables, d_skip, w_glu, b_glu, cv_w, cv_b, cv_ln_g, cv_ln_b)
    return y.reshape(bsz * seq, D_S5 + D_CONV)


ROPE_W = Q_WIDTH + 3 * KV_WIDTH
TOK_W = ROPE_W + KV_WIDTH
GATE_ROWS = 16
LOG2E = 1.4426950408889634


def _nsa_proj_body(x_ref, w_ref, wt_ref, cos_ref, sin_ref, q_ref, k_ref, kc_ref, vc_ref, vt_ref, gt_ref):
    xb = x_ref[...].astype(_MM)
    y = _dot(xb, w_ref[...])
    r = y[:, :ROPE_W]
    reps = ROPE_W // LANE
    cos = jnp.concatenate([cos_ref[...]] * reps, axis=1)
    sin = jnp.concatenate([sin_ref[...]] * reps, axis=1)
    lane = lax.broadcasted_iota(jnp.int32, r.shape, 1)
    half = HEAD_DIM // 2
    first = (lane % HEAD_DIM) < half
    rot = jnp.where(first, -pltpu.roll(r, ROPE_W - half, 1), pltpu.roll(r, half, 1))
    r = r * cos + rot * sin
    qs = r[:, :Q_WIDTH] * (HEAD_DIM ** -0.5 * LOG2E)
    for hd in range(N_HEADS):
        q_ref[hd] = qs[:, hd * HEAD_DIM:(hd + 1) * HEAD_DIM].astype(q_ref.dtype)
    for j in range(2 * N_KV_HEADS):
        c0 = Q_WIDTH + j * HEAD_DIM
        k_ref[j] = r[:, c0:c0 + HEAD_DIM].astype(k_ref.dtype)
    for j in range(N_KV_HEADS):
        c0 = Q_WIDTH + 2 * KV_WIDTH + j * HEAD_DIM
        kc_ref[j] = r[:, c0:c0 + HEAD_DIM]
        vc_ref[j] = y[:, ROPE_W + j * HEAD_DIM:ROPE_W + (j + 1) * HEAD_DIM]
    yt = _dot_nt(wt_ref[...], xb)
    vt_ref[...] = yt[:2 * KV_WIDTH, :].astype(vt_ref.dtype)
    gt_ref[...] = yt[2 * KV_WIDTH:, :]


def _nsa_proj(x2, bsz, seq, w_in):
    tm = min(ROW_TILE, seq)
    nt = seq // tm
    sizes = [Q_WIDTH] + [KV_WIDTH] * 6 + [GATE_WIDTH]
    offs = np.cumsum([0] + sizes)
    cols = lambda i: w_in[:, offs[i]:offs[i + 1]]
    w_tok = jnp.concatenate([cols(0), cols(3), cols(5), cols(1), cols(2)], axis=1).astype(_MM)
    gcols = cols(7).reshape(D_MODEL, N_KV_HEADS, GQA, 3).transpose(0, 1, 3, 2).reshape(D_MODEL, N_KV_HEADS, 3 * GQA)
    gcols = jnp.pad(gcols, ((0, 0), (0, 0), (0, GATE_ROWS - 3 * GQA))).reshape(D_MODEL, N_KV_HEADS * GATE_ROWS)
    w_t = jnp.concatenate([cols(4), cols(6), gcols], axis=1).T.astype(_MM)
    half = HEAD_DIM // 2
    inv = ROPE_THETA ** (-jnp.arange(half, dtype=jnp.float32) / half)
    ang = jnp.arange(seq, dtype=jnp.float32)[:, None] * inv[None, :]
    cos = jnp.tile(jnp.cos(ang), (1, LANE // half))
    sin = jnp.tile(jnp.sin(ang), (1, LANE // half))
    heads = lambda nh: pl.BlockSpec((None, nh, tm, HEAD_DIM), lambda b, i: (b, 0, i, 0))
    rows_t = lambda nr: pl.BlockSpec((None, nr, tm), lambda b, i: (b, 0, i))
    n_gate = N_KV_HEADS * GATE_ROWS
    return pl.pallas_call(
        _nsa_proj_body, grid=(bsz, nt),
        in_specs=[pl.BlockSpec((tm, D_MODEL), lambda b, i: (b * nt + i, 0)),
                  pl.BlockSpec(w_tok.shape, lambda b, i: (0, 0)),
                  pl.BlockSpec(w_t.shape, lambda b, i: (0, 0)),
                  pl.BlockSpec((tm, LANE), lambda b, i: (i, 0)),
                  pl.BlockSpec((tm, LANE), lambda b, i: (i, 0))],
        out_specs=[heads(N_HEADS), heads(2 * N_KV_HEADS), heads(N_KV_HEADS), heads(N_KV_HEADS),
                   rows_t(2 * KV_WIDTH), rows_t(n_gate)],
        out_shape=[jax.ShapeDtypeStruct((bsz, N_HEADS, seq, HEAD_DIM), _MM),
                   jax.ShapeDtypeStruct((bsz, 2 * N_KV_HEADS, seq, HEAD_DIM), _MM),
                   jax.ShapeDtypeStruct((bsz, N_KV_HEADS, seq, HEAD_DIM), jnp.float32),
                   jax.ShapeDtypeStruct((bsz, N_KV_HEADS, seq, HEAD_DIM), jnp.float32),
                   jax.ShapeDtypeStruct((bsz, 2 * KV_WIDTH, seq), _MM),
                   jax.ShapeDtypeStruct((bsz, n_gate, seq), jnp.float32)],
        compiler_params=_params("arbitrary", "arbitrary"), name="nsa_in_proj")(x2, w_tok, w_t, cos, sin)


def _compress_body(k_ref, v_ref, pk_ref, pv_ref, wk_ref, wvt_ref, ko_ref, vot_ref):
    rows = k_ref.shape[0]
    xk = k_ref[...]
    lo = _dot((xk + pk_ref[0:1, :]).astype(_MM), wk_ref[0])
    hi = _dot((xk + pk_ref[1:2, :]).astype(_MM), wk_ref[1])
    ko_ref[...] = (lo + pltpu.roll(hi, rows - 1, 0)).astype(ko_ref.dtype)
    xv = v_ref[...]
    lo_t = _dot_nt(wvt_ref[0], (xv + pv_ref[0:1, :]).astype(_MM))
    hi_t = _dot_nt(wvt_ref[1], (xv + pv_ref[1:2, :]).astype(_MM))
    vot_ref[...] = (lo_t + pltpu.roll(hi_t, rows - 1, 1)).astype(vot_ref.dtype)


def _compress(kc, vc, cmp_w_k, cmp_w_v, cmp_pe_k, cmp_pe_v):
    bsz, kvh, seq, _ = kc.shape
    grp = seq // CMP_STRIDE
    flat = CMP_STRIDE * HEAD_DIM
    k2 = kc.reshape(bsz * kvh * grp, flat)
    v2 = vc.reshape(bsz * kvh * grp, flat)
    consts = [cmp_pe_k.reshape(2, flat), cmp_pe_v.reshape(2, flat),
              cmp_w_k.reshape(2, flat, HEAD_DIM).astype(_MM),
              cmp_w_v.reshape(2, flat, HEAD_DIM).transpose(0, 2, 1).astype(_MM)]
    ko, vot = pl.pallas_call(
        _compress_body, grid=(bsz * kvh,),
        in_specs=[pl.BlockSpec((grp, flat), lambda i: (i, 0)), pl.BlockSpec((grp, flat), lambda i: (i, 0))]
        + [pl.BlockSpec(a.shape, functools.partial(lambda nd, i: (0,) * nd, a.ndim)) for a in consts],
        out_specs=[pl.BlockSpec((grp, HEAD_DIM), lambda i: (i, 0)),
                   pl.BlockSpec((None, HEAD_DIM, grp), lambda i: (i, 0, 0))],
        out_shape=[jax.ShapeDtypeStruct((bsz * kvh * grp, HEAD_DIM), _MM),
                   jax.ShapeDtypeStruct((bsz * kvh, HEAD_DIM, grp), _MM)],
        compiler_params=_params("arbitrary"), name="nsa_compress")(k2, v2, *consts)
    return ko.reshape(bsz, kvh, grp, HEAD_DIM), vot.reshape(bsz, kvh, HEAD_DIM, grp)


def _nsa_attn_body(q_ref, g_ref, kc_ref, vct_ref, ks_ref, vst_ref, kw_ref, vwt_ref, ov_ref, wm_ref, o_ref, sel_ref):
    qi = pl.program_id(2)
    q0 = qi * Q_TILE
    lanes = GQA * Q_TILE
    n_grp = kc_ref.shape[0]
    n_sb = ov_ref.shape[0]

    q = q_ref[...].reshape(lanes, HEAD_DIM)
    t_lane = q0 + (lax.broadcasted_iota(jnp.int32, (1, lanes), 1) % Q_TILE)

    s = _dot_nt(kc_ref[...], q)
    n_idx = lax.broadcasted_iota(jnp.int32, (n_grp, lanes), 0)
    vis = (n_idx * CMP_STRIDE + (CMP_LEN - 1)) <= t_lane
    s = jnp.where(vis, s, NEG_INF)
    e = jnp.where(vis, jnp.exp2(s - jnp.max(s, axis=0, keepdims=True)), 0.0)
    den = jnp.sum(e, axis=0, keepdims=True)
    p_c = e / jnp.where(den > 0.0, den, 1.0)
    o_c = _dot(vct_ref[...], p_c.astype(_MM))

    psum = p_c[:, 0:Q_TILE]
    for g in range(1, GQA):
        psum = psum + p_c[:, g * Q_TILE:(g + 1) * Q_TILE]
    imp = jnp.dot(ov_ref[...], psum, preferred_element_type=jnp.float32,
                  precision=lax.Precision.HIGHEST)
    blk = lax.broadcasted_iota(jnp.int32, (n_sb, Q_TILE), 0)
    cur = t_lane[:, 0:Q_TILE] // SEL_LEN
    allowed = blk <= cur
    forced = (blk == 0) | (blk == cur) | (blk == cur - 1)
    score = jnp.where(allowed, imp + jnp.where(forced, FORCE_BONUS, 0.0), -1.0)
    rank = jnp.zeros((n_sb, Q_TILE), jnp.float32)
    for j in range(n_sb):
        sj = score[j:j + 1, :]
        rank = rank + jnp.where(sj > score, 1.0, jnp.where((sj == score) & (blk > j), 1.0, 0.0))
    chosen = jnp.where((rank < float(N_SELECT)) & (score >= 0.0), 1.0, 0.0)
    sel_ref[0:n_sb, :] = jnp.concatenate([chosen] * GQA, axis=1)

    kpos_blk = lax.broadcasted_iota(jnp.int32, (SEL_LEN, lanes), 0)
    per_chunk = KEY_CHUNK // SEL_LEN

    def sel_chunk(c, carry, causal):
        m_run, l_run, acc = carry
        k0 = pl.multiple_of(c * KEY_CHUNK, KEY_CHUNK)
        sc = _dot_nt(ks_ref[pl.ds(k0, KEY_CHUNK), :], q)
        parts = []
        sel_rows = sel_ref[pl.ds(pl.multiple_of(c * per_chunk, per_chunk), per_chunk), :]
        for j in range(per_chunk):
            ok = sel_rows[j:j + 1, :] > 0.5
            if causal:
                ok = ok & ((k0 + j * SEL_LEN + kpos_blk) <= t_lane)
            parts.append(jnp.where(ok, sc[j * SEL_LEN:(j + 1) * SEL_LEN, :], NEG_INF))
        sc = jnp.concatenate(parts, axis=0)
        m_new = jnp.maximum(m_run, jnp.max(sc, axis=0, keepdims=True))
        scale = jnp.exp2(m_run - m_new)
        p = jnp.exp2(sc - m_new)
        l_new = scale * l_run + jnp.sum(p, axis=0, keepdims=True)
        acc = scale * acc + _dot(vst_ref[:, pl.ds(k0, KEY_CHUNK)], p.astype(_MM))
        return m_new, l_new, acc

    n_full = q0 // KEY_CHUNK
    init = (jnp.full((1, lanes), NEG_INF, jnp.float32), jnp.zeros((1, lanes), jnp.float32),
            jnp.zeros((HEAD_DIM, lanes), jnp.float32))
    carry = lax.fori_loop(0, n_full, functools.partial(sel_chunk, causal=False), init)
    _, l_s, acc_s = sel_chunk(n_full, carry, causal=True)
    o_s = acc_s / l_s

    span = WINDOW + Q_TILE
    back = jnp.minimum(q0, WINDOW)
    w0 = pl.multiple_of(q0 - back, Q_TILE)
    m0 = pl.multiple_of(WINDOW - back, Q_TILE)
    sw = _dot_nt(kw_ref[pl.ds(w0, span), :], q) + wm_ref[pl.ds(m0, span), :]
    pw = jnp.exp2(sw - jnp.max(sw, axis=0, keepdims=True))
    o_w = _dot(vwt_ref[:, pl.ds(w0, span)], pw.astype(_MM)) / jnp.sum(pw, axis=0, keepdims=True)

    gs = jax.nn.sigmoid(g_ref[...])
    gate = lambda c: jnp.concatenate([gs[c * GQA + g:c * GQA + g + 1, :] for g in range(GQA)], axis=1)
    out = gate(0) * o_c + gate(1) * o_s + gate(2) * o_w
    for g in range(GQA):
        o_ref[g] = out[:, g * Q_TILE:(g + 1) * Q_TILE].astype(o_ref.dtype)


def _nsa_attention(q4, g_t, k_cmp, v_cmp_t, k8, v_t):
    bsz, _, seq, _ = q4.shape
    n_grp = k_cmp.shape[2]
    n_sb = seq // SEL_LEN
    lanes = GQA * Q_TILE
    assert seq % KEY_CHUNK == 0 and seq >= WINDOW + Q_TILE and n_sb % (KEY_CHUNK // SEL_LEN) == 0
    c_start = np.arange(n_grp) * CMP_STRIDE
    s_start = np.arange(n_sb) * SEL_LEN
    overlap = ((c_start[None, :] < s_start[:, None] + SEL_LEN)
               & (c_start[None, :] + CMP_LEN > s_start[:, None])
               & (np.arange(n_grp)[None, :] < n_grp - 1)).astype(np.float32)
    u = np.arange(2 * WINDOW + Q_TILE)[:, None]
    ql = (np.arange(lanes) % Q_TILE)[None, :]
    wmask = np.where((u > ql) & (u <= WINDOW + ql), 0.0, NEG_INF).astype(np.float32)
    nkv = N_KV_HEADS
    return pl.pallas_call(
        _nsa_attn_body, grid=(bsz, nkv, seq // Q_TILE),
        in_specs=[pl.BlockSpec((None, GQA, Q_TILE, HEAD_DIM), lambda b, h, i: (b, h, i, 0)),
                  pl.BlockSpec((None, GATE_ROWS, Q_TILE), lambda b, h, i: (b, h, i)),
                  pl.BlockSpec((None, None, n_grp, HEAD_DIM), lambda b, h, i: (b, h, 0, 0)),
                  pl.BlockSpec((None, None, HEAD_DIM, n_grp), lambda b, h, i: (b, h, 0, 0)),
                  pl.BlockSpec((None, None, seq, HEAD_DIM), lambda b, h, i: (b, h, 0, 0)),
                  pl.BlockSpec((None, HEAD_DIM, seq), lambda b, h, i: (b, h, 0)),
                  pl.BlockSpec((None, None, seq, HEAD_DIM), lambda b, h, i: (b, nkv + h, 0, 0)),
                  pl.BlockSpec((None, HEAD_DIM, seq), lambda b, h, i: (b, nkv + h, 0)),
                  pl.BlockSpec(overlap.shape, lambda b, h, i: (0, 0)),
                  pl.BlockSpec(wmask.shape, lambda b, h, i: (0, 0))],
        out_specs=pl.BlockSpec((None, GQA, HEAD_DIM, Q_TILE), lambda b, h, i: (b, h, 0, i)),
        out_shape=jax.ShapeDtypeStruct((bsz, N_HEADS, HEAD_DIM, seq), _MM),
        scratch_shapes=[pltpu.VMEM((n_sb, lanes), jnp.float32)],
        compiler_params=_params("arbitrary", "arbitrary", "arbitrary"), name="nsa_attention")(
            q4, g_t, k_cmp, v_cmp_t, k8, v_t, k8, v_t, jnp.asarray(overlap), jnp.asarray(wmask))


def _nsa_mixer(x2, bsz, seq, w_in, cmp_w_k, cmp_w_v, cmp_pe_k, cmp_pe_v):
    q4, k8, kc4, vc4, v_t, g_t = _nsa_proj(x2, bsz, seq, w_in)
    k_cmp, v_cmp_t = _compress(kc4, vc4, cmp_w_k, cmp_w_v, cmp_pe_k, cmp_pe_v)
    o = _nsa_attention(q4, g_t, k_cmp, v_cmp_t, k8, v_t)
    return o.reshape(bsz, Q_WIDTH, seq)


def _proj_t_res_ln_body(yt_ref, res_ref, w_ref, g_ref, b_ref, o_ref):
    mix = lax.dot_general(yt_ref[...], w_ref[...], (((0,), (0,)), ((), ())), preferred_element_type=jnp.float32)
    o_ref[...] = _layer_norm(ALPHA * res_ref[...] + mix, g_ref[...], b_ref[...])


def _proj_t_res_ln(y_t, res, w, g, b):
    bsz, _, seq = y_t.shape
    tm = min(ROW_TILE, seq)
    nt = seq // tm
    consts = [w.astype(_MM), g.reshape(1, D_MODEL), b.reshape(1, D_MODEL)]
    row = lambda c: pl.BlockSpec((tm, c), lambda bi, i: (bi * nt + i, 0))
    return pl.pallas_call(
        _proj_t_res_ln_body, grid=(bsz, nt),
        in_specs=[pl.BlockSpec((None, Q_WIDTH, tm), lambda bi, i: (bi, 0, i)), row(D_MODEL)]
        + [pl.BlockSpec(a.shape, lambda bi, i: (0, 0)) for a in consts],
        out_specs=row(D_MODEL),
        out_shape=jax.ShapeDtypeStruct((bsz * seq, D_MODEL), jnp.float32),
        compiler_params=_params("arbitrary", "arbitrary"), name="out_proj_t_ln")(y_t, res, *consts)


CHUNK = 8
SORT_ROWS = ROW_TILE * TOP_K + N_EXPERTS * CHUNK
CHUNK_SLOTS = 1024


SLOT_RADIX = 64


def _segment_bounds(cnt):
    padded = (((cnt.astype(jnp.int32) + (CHUNK - 1)) // CHUNK) * CHUNK).astype(jnp.float32)
    e_r = lax.broadcasted_iota(jnp.int32, (N_EXPERTS, N_EXPERTS), 0)
    e_c = lax.broadcasted_iota(jnp.int32, (N_EXPERTS, N_EXPERTS), 1)
    upper = jnp.where(e_r < e_c, 1.0, 0.0).astype(jnp.bfloat16)
    start = _dot(jnp.broadcast_to(padded, (8, N_EXPERTS)).astype(jnp.bfloat16), upper)[0:1, :]
    return start, start + padded


def _router_body(x_ref, w_ref, b_ref, gate_ref, slot_ref, cnt_ref):
    logits = jnp.dot(x_ref[...], w_ref[...], preferred_element_type=jnp.float32,
                     precision=lax.Precision.HIGHEST)
    scores = jax.nn.sigmoid(logits)
    tm = scores.shape[0]
    pick = scores + b_ref[...]
    lane = lax.broadcasted_iota(jnp.int32, (tm, N_EXPERTS), 1).astype(jnp.float32)
    member = jnp.zeros((tm, N_EXPERTS), jnp.float32)
    for _ in range(TOP_K):
        best = jnp.max(pick, axis=1, keepdims=True)
        which = jnp.min(jnp.where(pick == best, lane, float(N_EXPERTS)), axis=1, keepdims=True)
        hit = lane == which
        pick = jnp.where(hit, -jnp.inf, pick)
        member = jnp.where(hit, 1.0, member)
    chosen = member > 0.0
    picked = jnp.where(chosen, scores, 0.0)
    gate_ref[...] = picked / jnp.sum(picked, axis=1, keepdims=True) * ROUTE_SCALE

    r_i = lax.broadcasted_iota(jnp.int32, (tm, tm), 0)
    c_i = lax.broadcasted_iota(jnp.int32, (tm, tm), 1)
    tri = jnp.where(c_i < r_i, 1.0, 0.0).astype(jnp.bfloat16)
    ahead = _dot(tri, member.astype(jnp.bfloat16))
    cnt = jnp.sum(member, axis=0, keepdims=True)
    start, _ = _segment_bounds(cnt)
    slot_ref[...] = jnp.where(chosen, start + ahead + 1.0, 0.0)
    cnt_ref[...] = jnp.broadcast_to(cnt, cnt_ref.shape)


def _router(x2, router_w, router_b):
    n = x2.shape[0]
    tm = min(ROW_TILE, n)
    tile = lambda c: pl.BlockSpec((tm, c), lambda i: (i, 0))
    whole = lambda a: pl.BlockSpec(a.shape, lambda i: (0, 0))
    rb = router_b.reshape(1, N_EXPERTS)
    return pl.pallas_call(
        _router_body, grid=(n // tm,),
        in_specs=[tile(D_MODEL), whole(router_w), whole(rb)],
        out_specs=[tile(N_EXPERTS), tile(N_EXPERTS), pl.BlockSpec((8, N_EXPERTS), lambda i: (i, 0))],
        out_shape=[jax.ShapeDtypeStruct((n, N_EXPERTS), jnp.float32), jax.ShapeDtypeStruct((n, N_EXPERTS), jnp.float32),
                   jax.ShapeDtypeStruct((n // tm * 8, N_EXPERTS), jnp.float32)],
        compiler_params=_params("arbitrary"), name="moe_router")(x2, router_w, rb)


def _chunk_copy(src, dst, sem):
    return pltpu.make_async_copy(src, dst, sem)


def _chunk(ref, c):
    return ref.at[pl.ds(pl.multiple_of(c * CHUNK, CHUNK), CHUNK), :]


CHUNK_UNROLL = 4


def _chunk_groups(n_chunks):
    return lax.shift_right_logical(n_chunks, CHUNK_UNROLL.bit_length() - 1)


def _expert_of_row(cnt):
    start, end = _segment_bounds(cnt)
    j = lax.broadcasted_iota(jnp.int32, (SORT_ROWS, N_EXPERTS), 0).astype(jnp.float32)
    return jnp.where((j >= start) & (j < end), 1.0, 0.0).astype(jnp.bfloat16)


def _spread(per_expert, owner):
    return _dot_nt(per_expert.astype(jnp.bfloat16), owner)


def _row_matches(slot, owner):
    hi = jnp.floor(slot * (1.0 / SLOT_RADIX))
    lo = slot - hi * SLOT_RADIX
    place = _spread(hi, owner) * SLOT_RADIX + _spread(lo, owner)
    j1 = lax.broadcasted_iota(jnp.int32, place.shape, 1).astype(jnp.float32) + 1.0
    return place == j1


def _dispatch_body(nch_ref, dst_ref, x_ref, slot_ref, cnt_ref, xs_in_ref, xs_ref, sbuf, sem):
    del xs_in_ref
    at = _row_matches(slot_ref[...], _expert_of_row(cnt_ref[0:1, :]))
    perm_t = jnp.where(at, 1.0, 0.0).astype(jnp.bfloat16)
    rows = lax.dot_general(perm_t, x_ref[...].astype(jnp.bfloat16),
                           (((0,), (0,)), ((), ())), preferred_element_type=jnp.float32)
    sbuf[...] = _pack_rows(rows)
    groups = _chunk_groups(nch_ref[pl.program_id(0)])

    def start(g, carry):
        for u in range(CHUNK_UNROLL):
            c = g * CHUNK_UNROLL + u
            _chunk_copy(_chunk(sbuf, c), _chunk(xs_ref, dst_ref[c]), sem).start()
        return carry

    def wait(g, carry):
        for u in range(CHUNK_UNROLL):
            _chunk_copy(_chunk(sbuf, g * CHUNK_UNROLL + u), _chunk(xs_ref, 0), sem).wait()
        return carry

    lax.fori_loop(0, groups, start, 0)
    lax.fori_loop(0, groups, wait, 0)


def _dispatch(n_chunks, dst_flat, x2, slot, cnt, n_rows):
    n = x2.shape[0]
    tm = min(ROW_TILE, n)
    xs0 = jnp.zeros((n_rows, HALF), jnp.uint32)
    grid_spec = pltpu.PrefetchScalarGridSpec(
        num_scalar_prefetch=1, grid=(n // tm,),
        in_specs=[pl.BlockSpec((CHUNK_SLOTS,), lambda i, nc: (i,), memory_space=pltpu.SMEM),
                  pl.BlockSpec((tm, D_MODEL), lambda i, nc: (i, 0)),
                  pl.BlockSpec((tm, N_EXPERTS), lambda i, nc: (i, 0)),
                  pl.BlockSpec((8, N_EXPERTS), lambda i, nc: (i, 0)),
                  pl.BlockSpec(memory_space=pl.ANY)],
        out_specs=pl.BlockSpec(memory_space=pl.ANY),
        scratch_shapes=[pltpu.VMEM((SORT_ROWS, HALF), jnp.uint32), pltpu.SemaphoreType.DMA(())])
    return pl.pallas_call(
        _dispatch_body, grid_spec=grid_spec,
        out_shape=jax.ShapeDtypeStruct((n_rows, HALF), jnp.uint32),
        input_output_aliases={5: 0},
        compiler_params=_params("arbitrary"), name="moe_dispatch")(n_chunks, dst_flat, x2, slot, cnt, xs0)


def _expert_body(blk_exp_ref, n_used_ref, x_ref, wg_ref, wu_ref, wd_ref, o_ref, wgu_s, wd_s):
    i = pl.program_id(0)
    prev = blk_exp_ref[jnp.maximum(i - 1, 0)]

    @pl.when((i == 0) | (blk_exp_ref[i] != prev))
    def _():
        wgu_s[:, :D_EXPERT] = wg_ref[...].astype(_MM)
        wgu_s[:, D_EXPERT:] = wu_ref[...].astype(_MM)
        wd_s[...] = wd_ref[...].astype(_MM)

    @pl.when(i < n_used_ref[0])
    def _():
        gu = _dot(_unpack_rows(x_ref[...]).astype(_MM), wgu_s[...])
        h = jax.nn.silu(gu[:, :D_EXPERT]) * gu[:, D_EXPERT:]
        o_ref[...] = _pack_rows(_dot(h.astype(_MM), wd_s[...]))

    @pl.when(i >= n_used_ref[0])
    def _():
        o_ref[...] = jnp.zeros_like(o_ref)


def _experts(xs, blk_exp, n_used, w_gate, w_up, w_down):
    n_rows = xs.shape[0]
    n_blk = n_rows // EXPERT_BLOCK
    grid_spec = pltpu.PrefetchScalarGridSpec(
        num_scalar_prefetch=2, grid=(n_blk,),
        in_specs=[pl.BlockSpec((EXPERT_BLOCK, HALF), lambda i, be, nu: (i, 0)),
                  pl.BlockSpec((None, D_MODEL, D_EXPERT), lambda i, be, nu: (be[i], 0, 0)),
                  pl.BlockSpec((None, D_MODEL, D_EXPERT), lambda i, be, nu: (be[i], 0, 0)),
                  pl.BlockSpec((None, D_EXPERT, D_MODEL), lambda i, be, nu: (be[i], 0, 0))],
        out_specs=pl.BlockSpec((EXPERT_BLOCK, HALF), lambda i, be, nu: (i, 0)),
        scratch_shapes=[pltpu.VMEM((D_MODEL, 2 * D_EXPERT), _MM), pltpu.VMEM((D_EXPERT, D_MODEL), _MM)])
    return pl.pallas_call(_expert_body, grid_spec=grid_spec,
                          out_shape=jax.ShapeDtypeStruct((n_rows, HALF), jnp.uint32),
                          compiler_params=_params("arbitrary"), name="moe_experts")(
                              blk_exp, n_used, xs, w_gate, w_up, w_down)


def _combine_body(nch_ref, dcur_ref, dnext_ref, x_ref, slot_ref, cnt_ref, gate_ref, ys_ref, sgu_ref, sd_ref,
                  g_ref, b_ref, o_ref, ybuf, sem):
    i = pl.program_id(0)
    slot = i % 2

    @pl.when(i == 0)
    def _():
        ybuf[...] = jnp.zeros_like(ybuf)

    def gather(d_ref, n, s):
        def body(g, carry):
            for u in range(CHUNK_UNROLL):
                c = g * CHUNK_UNROLL + u
                _chunk_copy(_chunk(ys_ref, d_ref[c]), _chunk(ybuf.at[s], c), sem.at[s]).start()
            return carry
        lax.fori_loop(0, _chunk_groups(n), body, 0)

    @pl.when(i == 0)
    def _():
        gather(dcur_ref, nch_ref[0], 0)

    @pl.when(i + 1 < pl.num_programs(0))
    def _():
        gather(dnext_ref, nch_ref[jnp.minimum(i + 1, pl.num_programs(0) - 1)], 1 - slot)

    def wait(g, carry):
        for u in range(CHUNK_UNROLL):
            _chunk_copy(_chunk(ys_ref, 0), _chunk(ybuf.at[slot], g * CHUNK_UNROLL + u), sem.at[slot]).wait()
        return carry

    lax.fori_loop(0, _chunk_groups(nch_ref[i]), wait, 0)

    x = x_ref[...]
    owner = _expert_of_row(cnt_ref[0:1, :])
    at = _row_matches(slot_ref[...], owner)
    gate = gate_ref[...]
    g_hi = gate.astype(jnp.bfloat16).astype(jnp.float32)
    g_lo = gate - g_hi
    p_hi = jnp.where(at, _spread(g_hi, owner), 0.0).astype(jnp.bfloat16)
    p_lo = jnp.where(at, _spread(g_lo, owner), 0.0).astype(jnp.bfloat16)
    y = _unpack_rows(ybuf[slot]).astype(jnp.bfloat16)
    routed = _dot(p_hi, y) + _dot(p_lo, y)
    gu = _dot(x.astype(_MM), sgu_ref[...])
    hs = jax.nn.silu(gu[:, :D_EXPERT]) * gu[:, D_EXPERT:]
    shared = _dot(hs.astype(_MM), sd_ref[...])
    o_ref[...] = _layer_norm(ALPHA * x + (routed + shared), g_ref[...], b_ref[...])


def _combine(n_chunks, dst_flat, x2, slot, cnt, gate, ys, s_gu, s_down, ln_g, ln_b):
    n = x2.shape[0]
    tm = min(ROW_TILE, n)
    nt = n // tm
    consts = [s_gu, s_down, ln_g.reshape(1, D_MODEL), ln_b.reshape(1, D_MODEL)]
    grid_spec = pltpu.PrefetchScalarGridSpec(
        num_scalar_prefetch=1, grid=(nt,),
        in_specs=[pl.BlockSpec((CHUNK_SLOTS,), lambda i, nc: (i,), memory_space=pltpu.SMEM),
                  pl.BlockSpec((CHUNK_SLOTS,), lambda i, nc: (jnp.minimum(i + 1, nt - 1),), memory_space=pltpu.SMEM),
                  pl.BlockSpec((tm, D_MODEL), lambda i, nc: (i, 0)),
                  pl.BlockSpec((tm, N_EXPERTS), lambda i, nc: (i, 0)),
                  pl.BlockSpec((8, N_EXPERTS), lambda i, nc: (i, 0)),
                  pl.BlockSpec((tm, N_EXPERTS), lambda i, nc: (i, 0)),
                  pl.BlockSpec(memory_space=pl.ANY)]
        + [pl.BlockSpec(a.shape, lambda i, nc: (0, 0)) for a in consts],
        out_specs=pl.BlockSpec((tm, D_MODEL), lambda i, nc: (i, 0)),
        scratch_shapes=[pltpu.VMEM((2, SORT_ROWS, HALF), jnp.uint32), pltpu.SemaphoreType.DMA((2,))])
    return pl.pallas_call(
        _combine_body, grid_spec=grid_spec,
        out_shape=jax.ShapeDtypeStruct((n, D_MODEL), jnp.float32),
        compiler_params=_params("arbitrary"), name="moe_combine")(
            n_chunks, dst_flat, dst_flat, x2, slot, cnt, gate, ys, *consts)


def _moe_ln(x2, router_w, router_b, w_gate, w_up, w_down, s_gate, s_up, s_down, ln_g, ln_b):
    n = x2.shape[0]
    tm = min(ROW_TILE, n)
    nt = n // tm
    gate, slot, cnt8 = _router(x2, router_w, router_b)
    cnt = cnt8[::8].astype(jnp.int32)
    seg = (cnt + CHUNK - 1) // CHUNK * CHUNK
    loc_end = jnp.cumsum(seg, axis=1)
    loc_start = loc_end - seg
    exp_rows = jnp.sum(seg, axis=0)
    padded = (exp_rows + EXPERT_BLOCK - 1) // EXPERT_BLOCK * EXPERT_BLOCK
    pad_end = jnp.cumsum(padded)
    seg_dst = (pad_end - padded)[None, :] + jnp.cumsum(seg, axis=0) - seg
    max_rows = n * TOP_K + nt * N_EXPERTS * (CHUNK - 1)
    n_blk = -(-max_rows // EXPERT_BLOCK) + N_EXPERTS + 1
    n_real = (loc_end[:, -1] // CHUNK).astype(jnp.int32)
    n_chunks = (n_real + CHUNK_UNROLL - 1) // CHUNK_UNROLL * CHUNK_UNROLL
    c_idx = jnp.arange(SORT_ROWS // CHUNK, dtype=jnp.int32)
    c_row = c_idx[None, :, None] * CHUNK
    in_seg = (loc_start[:, None, :] <= c_row) & (c_row < loc_end[:, None, :])
    shift = jnp.sum(jnp.where(in_seg, (seg_dst - loc_start)[:, None, :], 0), axis=2)
    dst = (shift + c_idx[None, :] * CHUNK) // CHUNK
    spare = (n_blk - 1) * (EXPERT_BLOCK // CHUNK)
    dst = jnp.where(c_idx[None, :] < n_real[:, None], dst, spare + c_idx[None, :] - n_real[:, None])
    dst_flat = jnp.pad(dst, ((0, 0), (0, CHUNK_SLOTS - dst.shape[1]))).reshape(nt * CHUNK_SLOTS).astype(jnp.int32)
    blk_row = jnp.arange(n_blk, dtype=jnp.int32) * EXPERT_BLOCK
    blk_exp = jnp.minimum(jnp.sum((pad_end[None, :] <= blk_row[:, None]).astype(jnp.int32), axis=1), N_EXPERTS - 1)
    n_used = (pad_end[-1:] // EXPERT_BLOCK).astype(jnp.int32)
    xs = _dispatch(n_chunks, dst_flat, x2, slot, cnt8, n_blk * EXPERT_BLOCK)
    ys = _experts(xs, blk_exp, n_used, w_gate, w_up, w_down)
    s_gu = jnp.concatenate([s_gate, s_up], axis=-1).astype(_MM)
    return _combine(n_chunks, dst_flat, x2, slot, cnt8, gate, ys, s_gu, s_down.astype(_MM), ln_g, ln_b)


def kernel(x, ev_w_in, ev_b_in, s5_lam_re, s5_lam_im, s5_log_dt, s5_b_re, s5_b_im, s5_c_re, s5_c_im, s5_d, s5_w_glu, s5_b_glu, cv_w, cv_b, cv_ln_g, cv_ln_b, ev_w_out, od_w_in, cmp_w_k, cmp_w_v, cmp_pe_k, cmp_pe_v, od_w_out, ln1_g, ln1_b, ln2_g, ln2_b, router_w, router_b, ex_w_gate, ex_w_up, ex_w_down, sh_w_gate, sh_w_up, sh_w_down):
    bsz, seq, _ = x.shape
    h = x.reshape(bsz * seq, D_MODEL)
    for layer in range(DEPTH):
        i = layer // 2
        if layer % 2 == 0:
            mix = _even_mixer(h, bsz, seq, ev_w_in[i], ev_b_in[i], s5_lam_re[i], s5_lam_im[i], s5_log_dt[i],
                              s5_b_re[i], s5_b_im[i], s5_c_re[i], s5_c_im[i], s5_d[i].reshape(-1),
                              s5_w_glu[i], s5_b_glu[i], cv_w[i], cv_b[i], cv_ln_g[i], cv_ln_b[i])
            (h,) = _row_call(_proj_res_ln_body, [mix, h],
                             [ev_w_out[i].astype(_MM), ln1_g[layer].reshape(1, D_MODEL),
                              ln1_b[layer].reshape(1, D_MODEL)],
                             [(D_MODEL, jnp.float32)], "out_proj_ln")
        else:
            mix_t = _nsa_mixer(h, bsz, seq, od_w_in[i], cmp_w_k[i], cmp_w_v[i], cmp_pe_k[i], cmp_pe_v[i])
            h = _proj_t_res_ln(mix_t, h, od_w_out[i], ln1_g[layer], ln1_b[layer])
        h = _moe_ln(h, router_w[layer], router_b[layer], ex_w_gate[layer], ex_w_up[layer], ex_w_down[layer],
                    sh_w_gate[layer], sh_w_up[layer], sh_w_down[layer], ln2_g[layer], ln2_b[layer])
    return h.reshape(bsz, seq, D_MODEL)
```

```python
import functools
import math

import numpy as np
import jax
import jax.numpy as jnp
from jax import lax
from jax.experimental import pallas as pl
from jax.experimental.pallas import tpu as pltpu

D_MODEL = 1024
DEPTH = 4
ALPHA = (2.0 * DEPTH) ** 0.25
LN_EPS = 1e-5
NEG_INF = -1e30

D_S5 = 512
S5_GROUP = 16
S5_GROUPS = 32
S5_STATE = 64
S5_LANES = S5_GROUPS * S5_STATE
D_CONV = 512
CONV_WIDTH = 31
EVEN_IN = D_S5 + 2 * D_CONV

HEAD_DIM = 64
N_HEADS = 16
N_KV_HEADS = 4
GQA = 4
CMP_LEN = 32
CMP_STRIDE = 16
SEL_LEN = 64
N_SELECT = 16
WINDOW = 512
FORCE_BONUS = 1e4
ROPE_THETA = 10000.0
Q_WIDTH = N_HEADS * HEAD_DIM
KV_WIDTH = N_KV_HEADS * HEAD_DIM
GATE_WIDTH = 3 * N_HEADS

N_EXPERTS = 64
TOP_K = 8
D_EXPERT = 256
ROUTE_SCALE = 2.5

LANE = 128
_MM = jnp.bfloat16
_VMEM_LIMIT = 56 * 1024 * 1024

ROW_TILE = 256
SCAN_ROWS = 128
SEQ_TILE = 256
CONV_HALO = 32
Q_TILE = 256
KEY_CHUNK = 1024
EXPERT_BLOCK = 512


def _dot(a, b):
    return jnp.dot(a, b, preferred_element_type=jnp.float32)


def _dot_nt(a, b):
    return lax.dot_general(a, b, (((1,), (1,)), ((), ())), preferred_element_type=jnp.float32)


def _layer_norm(x, g, b):
    mu = jnp.mean(x, axis=-1, keepdims=True)
    xc = x - mu
    var = jnp.mean(xc * xc, axis=-1, keepdims=True)
    return xc * lax.rsqrt(var + LN_EPS) * g + b


HALF = D_MODEL // 2


def _pack_rows(x):
    bits = lambda v: lax.bitcast_convert_type(v.astype(jnp.bfloat16).astype(jnp.float32), jnp.uint32)
    return (bits(x[:, HALF:]) & jnp.uint32(0xFFFF0000)) | (bits(x[:, :HALF]) >> 16)


def _unpack_rows(w):
    lo = lax.bitcast_convert_type(w << 16, jnp.float32)
    hi = lax.bitcast_convert_type(w & jnp.uint32(0xFFFF0000), jnp.float32)
    return jnp.concatenate([lo, hi], axis=1)


def _params(*sem):
    return pltpu.CompilerParams(dimension_semantics=sem, vmem_limit_bytes=_VMEM_LIMIT)


def _row_call(body, row_ins, const_ins, outs, name, tm=ROW_TILE):
    m = row_ins[0].shape[0]
    tm = min(tm, m)
    assert m % tm == 0
    in_specs = [pl.BlockSpec((tm, a.shape[1]), lambda i: (i, 0)) for a in row_ins]
    in_specs += [pl.BlockSpec(a.shape, functools.partial(lambda nd, i: (0,) * nd, a.ndim)) for a in const_ins]
    out_specs = [pl.BlockSpec((tm, c), lambda i: (i, 0)) for c, _ in outs]
    out_shape = [jax.ShapeDtypeStruct((m, c), dt) for c, dt in outs]
    return pl.pallas_call(body, grid=(m // tm,), in_specs=in_specs, out_specs=out_specs,
                          out_shape=out_shape, compiler_params=_params("arbitrary"), name=name)(
                              *row_ins, *const_ins)


def _proj_bias_body(x_ref, w_ref, b_ref, o_ref):
    o_ref[...] = _dot(x_ref[...].astype(_MM), w_ref[...]) + b_ref[...]


def _proj_res_ln_body(y_ref, res_ref, w_ref, g_ref, b_ref, o_ref):
    mix = _dot(y_ref[...].astype(_MM), w_ref[...])
    o_ref[...] = _layer_norm(ALPHA * res_ref[...] + mix, g_ref[...], b_ref[...])


def _s5_tables(lam_re, lam_im, log_dt, b_re, b_im, c_re, c_im):
    f32 = jnp.float32
    dt = jnp.exp(log_dt.astype(f32))[:, None]
    decay = jnp.exp(lam_re * dt)
    a_re, a_im = decay * jnp.cos(lam_im * dt), decay * jnp.sin(lam_im * dt)
    den = lam_re ** 2 + lam_im ** 2
    f_re = ((a_re - 1.0) * lam_re + a_im * lam_im) / den
    f_im = (a_im * lam_re - (a_re - 1.0) * lam_im) / den
    bb_re = f_re[..., None] * b_re - f_im[..., None] * b_im
    bb_im = f_re[..., None] * b_im + f_im[..., None] * b_re

    gl = LANE // S5_GROUP
    nj = S5_GROUPS // gl
    eye = jnp.eye(gl, dtype=f32)

    def in_blocks(bb):
        t = bb.reshape(nj, gl, S5_STATE, S5_GROUP)
        t = jnp.einsum('jgph,gk->jghkp', t, eye)
        return t.reshape(nj, gl * S5_GROUP, gl * S5_STATE).astype(_MM)

    def out_blocks(c):
        t = c.reshape(nj, gl, S5_GROUP, S5_STATE)
        t = jnp.einsum('jghp,gk->jgpkh', t, eye)
        return t.reshape(nj, gl * S5_STATE, gl * S5_GROUP).astype(_MM)

    ar, ai = a_re.reshape(1, S5_LANES), a_im.reshape(1, S5_LANES)
    pows_r, pows_i = [ar], [ai]
    for _ in range(int(math.log2(SCAN_ROWS)) - 1):
        pr, pi = pows_r[-1], pows_i[-1]
        pows_r.append(pr * pr - pi * pi)
        pows_i.append(2.0 * pr * pi)
    tr, ti = ar, ai
    for k in range(int(math.log2(SCAN_ROWS))):
        pr, pi = pows_r[k], pows_i[k]
        tr, ti = (jnp.concatenate([tr, tr * pr - ti * pi], axis=0),
                  jnp.concatenate([ti, tr * pi + ti * pr], axis=0))
    nlb = S5_LANES // LANE
    to3 = lambda t: t.reshape(t.shape[0], nlb, LANE).transpose(1, 0, 2)
    return (in_blocks(bb_re), in_blocks(bb_im), out_blocks(c_re), out_blocks(c_im),
            to3(jnp.concatenate(pows_r, axis=0)), to3(jnp.concatenate(pows_i, axis=0)), to3(tr), to3(ti))


def _s5conv_body(h_ref, bre_ref, bim_ref, cre_ref, cim_ref, pwr_ref, pwi_ref, tbr_ref, tbi_ref,
                 d_ref, wglu_ref, bglu_ref, cvw_ref, cvb_ref, lng_ref, lnb_ref, o_ref,
                 st_re, st_im, xr_ref, xi_ref, hbuf):
    tile = h_ref.shape[0]
    n_steps = int(math.log2(SCAN_ROWS))
    nlb = S5_LANES // LANE
    nj = bre_ref.shape[0]
    per_j = nlb // nj

    @pl.when(pl.program_id(1) == 0)
    def _():
        st_re[...] = jnp.zeros_like(st_re)
        st_im[...] = jnp.zeros_like(st_im)
        hbuf[0:CONV_HALO, :] = jnp.zeros((CONV_HALO, D_CONV), jnp.float32)

    row = lax.broadcasted_iota(jnp.int32, (SCAN_ROWS, LANE), 0)

    for c in range(tile // SCAN_ROWS):
        r0 = c * SCAN_ROWS
        u = h_ref[r0:r0 + SCAN_ROWS, 0:D_S5]
        ub = u.astype(_MM)
        for j in range(nj):
            uj = ub[:, j * LANE:(j + 1) * LANE]
            br = _dot(uj, bre_ref[j])
            bi = _dot(uj, bim_ref[j])
            for q in range(per_j):
                xr_ref[j * per_j + q] = br[:, q * LANE:(q + 1) * LANE]
                xi_ref[j * per_j + q] = bi[:, q * LANE:(q + 1) * LANE]

        def scan_block(lb, carry):
            xr, xi = xr_ref[lb], xi_ref[lb]
            for k in range(n_steps):
                d = 1 << k
                ar = pwr_ref[lb, k:k + 1, :]
                ai = pwi_ref[lb, k:k + 1, :]
                keep = row >= d
                sr = jnp.where(keep, pltpu.roll(xr, d, 0), 0.0)
                si = jnp.where(keep, pltpu.roll(xi, d, 0), 0.0)
                xr, xi = xr + ar * sr - ai * si, xi + ar * si + ai * sr
            pr, pi = st_re[lb, 0:1, :], st_im[lb, 0:1, :]
            tr, ti = tbr_ref[lb], tbi_ref[lb]
            xr, xi = xr + tr * pr - ti * pi, xi + tr * pi + ti * pr
            xr_ref[lb] = xr
            xi_ref[lb] = xi
            st_re[lb, 0:1, :] = xr[SCAN_ROWS - 1:SCAN_ROWS, :]
            st_im[lb, 0:1, :] = xi[SCAN_ROWS - 1:SCAN_ROWS, :]
            return carry

        lax.fori_loop(0, nlb, scan_block, 0)

        ys = []
        for j in range(nj):
            xr = jnp.concatenate([xr_ref[j * per_j + q] for q in range(per_j)], axis=1).astype(_MM)
            xi = jnp.concatenate([xi_ref[j * per_j + q] for q in range(per_j)], axis=1).astype(_MM)
            ys.append(_dot(xr, cre_ref[j]) - _dot(xi, cim_ref[j]))
        y = jnp.concatenate(ys, axis=1) + d_ref[...] * u
        y = jax.nn.gelu(y)
        y = y * jax.nn.sigmoid(_dot(y.astype(_MM), wglu_ref[...]) + bglu_ref[...])
        o_ref[r0:r0 + SCAN_ROWS, 0:D_S5] = y.astype(o_ref.dtype)

    val = h_ref[:, D_S5:D_S5 + D_CONV]
    gate = h_ref[:, D_S5 + D_CONV:D_S5 + 2 * D_CONV]
    hbuf[CONV_HALO:CONV_HALO + tile, :] = val * jax.nn.sigmoid(gate)
    off = CONV_HALO - (CONV_WIDTH - 1)
    acc = jnp.zeros((tile, D_CONV), jnp.float32)
    for k in range(CONV_WIDTH):
        acc = acc + cvw_ref[k:k + 1, :] * hbuf[off + k:off + k + tile, :]
    acc = acc + cvb_ref[...]
    yb = _layer_norm(acc, lng_ref[...], lnb_ref[...])
    o_ref[:, D_S5:D_S5 + D_CONV] = (yb * jax.nn.sigmoid(yb)).astype(o_ref.dtype)
    hbuf[0:CONV_HALO, :] = hbuf[tile:tile + CONV_HALO, :]


def _s5conv(h, tables, d_skip, w_glu, b_glu, cv_w, cv_b, ln_g, ln_b):
    bsz, seq, _ = h.shape
    tile = min(SEQ_TILE, seq)
    assert seq % tile == 0 and tile % SCAN_ROWS == 0
    bre, bim, cre, cim, pwr, pwi, tbr, tbi = tables
    consts = [bre, bim, cre, cim, pwr, pwi, tbr, tbi,
              d_skip.reshape(1, D_S5), w_glu.astype(_MM), b_glu.reshape(1, D_S5),
              cv_w, cv_b.reshape(1, D_CONV), ln_g.reshape(1, D_CONV), ln_b.reshape(1, D_CONV)]
    nlb = S5_LANES // LANE
    in_specs = [pl.BlockSpec((None, tile, EVEN_IN), lambda b, l: (b, l, 0))]
    in_specs += [pl.BlockSpec(a.shape, functools.partial(lambda nd, b, l: (0,) * nd, a.ndim)) for a in consts]
    return pl.pallas_call(
        _s5conv_body, grid=(bsz, seq // tile), in_specs=in_specs,
        out_specs=pl.BlockSpec((None, tile, D_S5 + D_CONV), lambda b, l: (b, l, 0)),
        out_shape=jax.ShapeDtypeStruct((bsz, seq, D_S5 + D_CONV), _MM),
        scratch_shapes=[pltpu.VMEM((nlb, 8, LANE), jnp.float32), pltpu.VMEM((nlb, 8, LANE), jnp.float32),
                        pltpu.VMEM((nlb, SCAN_ROWS, LANE), jnp.float32),
                        pltpu.VMEM((nlb, SCAN_ROWS, LANE), jnp.float32),
                        pltpu.VMEM((tile + CONV_HALO, D_CONV), jnp.float32)],
        compiler_params=_params("arbitrary", "arbitrary"), name="s5conv")(h, *consts)


def _even_mixer(x2, bsz, seq, w_in, b_in, lam_re, lam_im, log_dt, b_re, b_im, c_re, c_im, d_skip,
                w_glu, b_glu, cv_w, cv_b, cv_ln_g, cv_ln_b):
    (h,) = _row_call(_proj_bias_body, [x2], [w_in.astype(_MM), b_in.reshape(1, EVEN_IN)],
                     [(EVEN_IN, jnp.float32)], "even_in_proj")
    tables = _s5_tables(lam_re, lam_im, log_dt, b_re, b_im, c_re, c_im)
    y = _s5conv(h.reshape(bsz, seq, EVEN_IN), tables, d_skip, w_glu, b_glu, cv_w, cv_b, cv_ln_g, cv_ln_b)
    return y.reshape(bsz * seq, D_S5 + D_CONV)


ROPE_W = Q_WIDTH + 3 * KV_WIDTH
TOK_W = ROPE_W + KV_WIDTH
GATE_ROWS = 16
LOG2E = 1.4426950408889634


def _nsa_proj_body(x_ref, w_ref, wt_ref, cos_ref, sin_ref, q_ref, k_ref, kc_ref, vc_ref, vt_ref, gt_ref):
    xb = x_ref[...].astype(_MM)
    y = _dot(xb, w_ref[...])
    r = y[:, :ROPE_W]
    reps = ROPE_W // LANE
    cos = jnp.concatenate([cos_ref[...]] * reps, axis=1)
    sin = jnp.concatenate([sin_ref[...]] * reps, axis=1)
    lane = lax.broadcasted_iota(jnp.int32, r.shape, 1)
    half = HEAD_DIM // 2
    first = (lane % HEAD_DIM) < half
    rot = jnp.where(first, -pltpu.roll(r, ROPE_W - half, 1), pltpu.roll(r, half, 1))
    r = r * cos + rot * sin
    qs = r[:, :Q_WIDTH] * (HEAD_DIM ** -0.5 * LOG2E)
    for hd in range(N_HEADS):
        q_ref[hd] = qs[:, hd * HEAD_DIM:(hd + 1) * HEAD_DIM].astype(q_ref.dtype)
    for j in range(2 * N_KV_HEADS):
        c0 = Q_WIDTH + j * HEAD_DIM
        k_ref[j] = r[:, c0:c0 + HEAD_DIM].astype(k_ref.dtype)
    for j in range(N_KV_HEADS):
        c0 = Q_WIDTH + 2 * KV_WIDTH + j * HEAD_DIM
        kc_ref[j] = r[:, c0:c0 + HEAD_DIM]
        vc_ref[j] = y[:, ROPE_W + j * HEAD_DIM:ROPE_W + (j + 1) * HEAD_DIM]
    yt = _dot_nt(wt_ref[...], xb)
    vt_ref[...] = yt[:2 * KV_WIDTH, :].astype(vt_ref.dtype)
    gt_ref[...] = yt[2 * KV_WIDTH:, :]


def _nsa_proj(x2, bsz, seq, w_in):
    tm = min(ROW_TILE, seq)
    nt = seq // tm
    sizes = [Q_WIDTH] + [KV_WIDTH] * 6 + [GATE_WIDTH]
    offs = np.cumsum([0] + sizes)
    cols = lambda i: w_in[:, offs[i]:offs[i + 1]]
    w_tok = jnp.concatenate([cols(0), cols(3), cols(5), cols(1), cols(2)], axis=1).astype(_MM)
    gcols = cols(7).reshape(D_MODEL, N_KV_HEADS, GQA, 3).transpose(0, 1, 3, 2).reshape(D_MODEL, N_KV_HEADS, 3 * GQA)
    gcols = jnp.pad(gcols, ((0, 0), (0, 0), (0, GATE_ROWS - 3 * GQA))).reshape(D_MODEL, N_KV_HEADS * GATE_ROWS)
    w_t = jnp.concatenate([cols(4), cols(6), gcols], axis=1).T.astype(_MM)
    half = HEAD_DIM // 2
    inv = ROPE_THETA ** (-jnp.arange(half, dtype=jnp.float32) / half)
    ang = jnp.arange(seq, dtype=jnp.float32)[:, None] * inv[None, :]
    cos = jnp.tile(jnp.cos(ang), (1, LANE // half))
    sin = jnp.tile(jnp.sin(ang), (1, LANE // half))
    heads = lambda nh: pl.BlockSpec((None, nh, tm, HEAD_DIM), lambda b, i: (b, 0, i, 0))
    rows_t = lambda nr: pl.BlockSpec((None, nr, tm), lambda b, i: (b, 0, i))
    n_gate = N_KV_HEADS * GATE_ROWS
    return pl.pallas_call(
        _nsa_proj_body, grid=(bsz, nt),
        in_specs=[pl.BlockSpec((tm, D_MODEL), lambda b, i: (b * nt + i, 0)),
                  pl.BlockSpec(w_tok.shape, lambda b, i: (0, 0)),
                  pl.BlockSpec(w_t.shape, lambda b, i: (0, 0)),
                  pl.BlockSpec((tm, LANE), lambda b, i: (i, 0)),
                  pl.BlockSpec((tm, LANE), lambda b, i: (i, 0))],
        out_specs=[heads(N_HEADS), heads(2 * N_KV_HEADS), heads(N_KV_HEADS), heads(N_KV_HEADS),
                   rows_t(2 * KV_WIDTH), rows_t(n_gate)],
        out_shape=[jax.ShapeDtypeStruct((bsz, N_HEADS, seq, HEAD_DIM), _MM),
                   jax.ShapeDtypeStruct((bsz, 2 * N_KV_HEADS, seq, HEAD_DIM), _MM),
                   jax.ShapeDtypeStruct((bsz, N_KV_HEADS, seq, HEAD_DIM), jnp.float32),
                   jax.ShapeDtypeStruct((bsz, N_KV_HEADS, seq, HEAD_DIM), jnp.float32),
                   jax.ShapeDtypeStruct((bsz, 2 * KV_WIDTH, seq), _MM),
                   jax.ShapeDtypeStruct((bsz, n_gate, seq), jnp.float32)],
        compiler_params=_params("arbitrary", "arbitrary"), name="nsa_in_proj")(x2, w_tok, w_t, cos, sin)


def _compress_body(k_ref, v_ref, pk_ref, pv_ref, wk_ref, wvt_ref, ko_ref, vot_ref):
    rows = k_ref.shape[0]
    xk = k_ref[...]
    lo = _dot((xk + pk_ref[0:1, :]).astype(_MM), wk_ref[0])
    hi = _dot((xk + pk_ref[1:2, :]).astype(_MM), wk_ref[1])
    ko_ref[...] = (lo + pltpu.roll(hi, rows - 1, 0)).astype(ko_ref.dtype)
    xv = v_ref[...]
    lo_t = _dot_nt(wvt_ref[0], (xv + pv_ref[0:1, :]).astype(_MM))
    hi_t = _dot_nt(wvt_ref[1], (xv + pv_ref[1:2, :]).astype(_MM))
    vot_ref[...] = (lo_t + pltpu.roll(hi_t, rows - 1, 1)).astype(vot_ref.dtype)


def _compress(kc, vc, cmp_w_k, cmp_w_v, cmp_pe_k, cmp_pe_v):
    bsz, kvh, seq, _ = kc.shape
    grp = seq // CMP_STRIDE
    flat = CMP_STRIDE * HEAD_DIM
    k2 = kc.reshape(bsz * kvh * grp, flat)
    v2 = vc.reshape(bsz * kvh * grp, flat)
    consts = [cmp_pe_k.reshape(2, flat), cmp_pe_v.reshape(2, flat),
              cmp_w_k.reshape(2, flat, HEAD_DIM).astype(_MM),
              cmp_w_v.reshape(2, flat, HEAD_DIM).transpose(0, 2, 1).astype(_MM)]
    ko, vot = pl.pallas_call(
        _compress_body, grid=(bsz * kvh,),
        in_specs=[pl.BlockSpec((grp, flat), lambda i: (i, 0)), pl.BlockSpec((grp, flat), lambda i: (i, 0))]
        + [pl.BlockSpec(a.shape, functools.partial(lambda nd, i: (0,) * nd, a.ndim)) for a in consts],
        out_specs=[pl.BlockSpec((grp, HEAD_DIM), lambda i: (i, 0)),
                   pl.BlockSpec((None, HEAD_DIM, grp), lambda i: (i, 0, 0))],
        out_shape=[jax.ShapeDtypeStruct((bsz * kvh * grp, HEAD_DIM), _MM),
                   jax.ShapeDtypeStruct((bsz * kvh, HEAD_DIM, grp), _MM)],
        compiler_params=_params("arbitrary"), name="nsa_compress")(k2, v2, *consts)
    return ko.reshape(bsz, kvh, grp, HEAD_DIM), vot.reshape(bsz, kvh, HEAD_DIM, grp)


def _nsa_attn_body(q_ref, g_ref, kc_ref, vct_ref, ks_ref, vst_ref, kw_ref, vwt_ref, ov_ref, wm_ref, o_ref, sel_ref):
    qi = pl.program_id(2)
    q0 = qi * Q_TILE
    lanes = GQA * Q_TILE
    n_grp = kc_ref.shape[0]
    n_sb = ov_ref.shape[0]

    q = q_ref[...].reshape(lanes, HEAD_DIM)
    t_lane = q0 + (lax.broadcasted_iota(jnp.int32, (1, lanes), 1) % Q_TILE)

    s = _dot_nt(kc_ref[...], q)
    n_idx = lax.broadcasted_iota(jnp.int32, (n_grp, lanes), 0)
    vis = (n_idx * CMP_STRIDE + (CMP_LEN - 1)) <= t_lane
    s = jnp.where(vis, s, NEG_INF)
    e = jnp.where(vis, jnp.exp2(s - jnp.max(s, axis=0, keepdims=True)), 0.0)
    den = jnp.sum(e, axis=0, keepdims=True)
    p_c = e / jnp.where(den > 0.0, den, 1.0)
    o_c = _dot(vct_ref[...], p_c.astype(_MM))

    psum = p_c[:, 0:Q_TILE]
    for g in range(1, GQA):
        psum = psum + p_c[:, g * Q_TILE:(g + 1) * Q_TILE]
    imp = jnp.dot(ov_ref[...], psum, preferred_element_type=jnp.float32,
                  precision=lax.Precision.HIGHEST)
    blk = lax.broadcasted_iota(jnp.int32, (n_sb, Q_TILE), 0)
    cur = t_lane[:, 0:Q_TILE] // SEL_LEN
    allowed = blk <= cur
    forced = (blk == 0) | (blk == cur) | (blk == cur - 1)
    score = jnp.where(allowed, imp + jnp.where(forced, FORCE_BONUS, 0.0), -1.0)
    rank = jnp.zeros((n_sb, Q_TILE), jnp.float32)
    for j in range(n_sb):
        sj = score[j:j + 1, :]
        rank = rank + jnp.where(sj > score, 1.0, jnp.where((sj == score) & (blk > j), 1.0, 0.0))
    chosen = jnp.where((rank < float(N_SELECT)) & (score >= 0.0), 1.0, 0.0)
    sel_ref[0:n_sb, :] = jnp.concatenate([chosen] * GQA, axis=1)

    kpos_blk = lax.broadcasted_iota(jnp.int32, (SEL_LEN, lanes), 0)
    per_chunk = KEY_CHUNK // SEL_LEN

    def sel_chunk(c, carry, causal):
        m_run, l_run, acc = carry
        k0 = pl.multiple_of(c * KEY_CHUNK, KEY_CHUNK)
        sc = _dot_nt(ks_ref[pl.ds(k0, KEY_CHUNK), :], q)
        parts = []
        sel_rows = sel_ref[pl.ds(pl.multiple_of(c * per_chunk, per_chunk), per_chunk), :]
        for j in range(per_chunk):
            ok = sel_rows[j:j + 1, :] > 0.5
            if causal:
                ok = ok & ((k0 + j * SEL_LEN + kpos_blk) <= t_lane)
            parts.append(jnp.where(ok, sc[j * SEL_LEN:(j + 1) * SEL_LEN, :], NEG_INF))
        sc = jnp.concatenate(parts, axis=0)
        m_new = jnp.maximum(m_run, jnp.max(sc, axis=0, keepdims=True))
        scale = jnp.exp2(m_run - m_new)
        p = jnp.exp2(sc - m_new)
        l_new = scale * l_run + jnp.sum(p, axis=0, keepdims=True)
        acc = scale * acc + _dot(vst_ref[:, pl.ds(k0, KEY_CHUNK)], p.astype(_MM))
        return m_new, l_new, acc

    n_full = q0 // KEY_CHUNK
    init = (jnp.full((1, lanes), NEG_INF, jnp.float32), jnp.zeros((1, lanes), jnp.float32),
            jnp.zeros((HEAD_DIM, lanes), jnp.float32))
    carry = lax.fori_loop(0, n_full, functools.partial(sel_chunk, causal=False), init)
    _, l_s, acc_s = sel_chunk(n_full, carry, causal=True)
    o_s = acc_s / l_s

    span = WINDOW + Q_TILE
    back = jnp.minimum(q0, WINDOW)
    w0 = pl.multiple_of(q0 - back, Q_TILE)
    m0 = pl.multiple_of(WINDOW - back, Q_TILE)
    sw = _dot_nt(kw_ref[pl.ds(w0, span), :], q) + wm_ref[pl.ds(m0, span), :]
    pw = jnp.exp2(sw - jnp.max(sw, axis=0, keepdims=True))
    o_w = _dot(vwt_ref[:, pl.ds(w0, span)], pw.astype(_MM)) / jnp.sum(pw, axis=0, keepdims=True)

    gs = jax.nn.sigmoid(g_ref[...])
    gate = lambda c: jnp.concatenate([gs[c * GQA + g:c * GQA + g + 1, :] for g in range(GQA)], axis=1)
    out = gate(0) * o_c + gate(1) * o_s + gate(2) * o_w
    for g in range(GQA):
        o_ref[g] = out[:, g * Q_TILE:(g + 1) * Q_TILE].astype(o_ref.dtype)


def _nsa_attention(q4, g_t, k_cmp, v_cmp_t, k8, v_t):
    bsz, _, seq, _ = q4.shape
    n_grp = k_cmp.shape[2]
    n_sb = seq // SEL_LEN
    lanes = GQA * Q_TILE
    assert seq % KEY_CHUNK == 0 and seq >= WINDOW + Q_TILE and n_sb % (KEY_CHUNK // SEL_LEN) == 0
    c_start = np.arange(n_grp) * CMP_STRIDE
    s_start = np.arange(n_sb) * SEL_LEN
    overlap = ((c_start[None, :] < s_start[:, None] + SEL_LEN)
               & (c_start[None, :] + CMP_LEN > s_start[:, None])
               & (np.arange(n_grp)[None, :] < n_grp - 1)).astype(np.float32)
    u = np.arange(2 * WINDOW + Q_TILE)[:, None]
    ql = (np.arange(lanes) % Q_TILE)[None, :]
    wmask = np.where((u > ql) & (u <= WINDOW + ql), 0.0, NEG_INF).astype(np.float32)
    nkv = N_KV_HEADS
    return pl.pallas_call(
        _nsa_attn_body, grid=(bsz, nkv, seq // Q_TILE),
        in_specs=[pl.BlockSpec((None, GQA, Q_TILE, HEAD_DIM), lambda b, h, i: (b, h, i, 0)),
                  pl.BlockSpec((None, GATE_ROWS, Q_TILE), lambda b, h, i: (b, h, i)),
                  pl.BlockSpec((None, None, n_grp, HEAD_DIM), lambda b, h, i: (b, h, 0, 0)),
                  pl.BlockSpec((None, None, HEAD_DIM, n_grp), lambda b, h, i: (b, h, 0, 0)),
                  pl.BlockSpec((None, None, seq, HEAD_DIM), lambda b, h, i: (b, h, 0, 0)),
                  pl.BlockSpec((None, HEAD_DIM, seq), lambda b, h, i: (b, h, 0)),
                  pl.BlockSpec((None, None, seq, HEAD_DIM), lambda b, h, i: (b, nkv + h, 0, 0)),
                  pl.BlockSpec((None, HEAD_DIM, seq), lambda b, h, i: (b, nkv + h, 0)),
                  pl.BlockSpec(overlap.shape, lambda b, h, i: (0, 0)),
                  pl.BlockSpec(wmask.shape, lambda b, h, i: (0, 0))],
        out_specs=pl.BlockSpec((None, GQA, HEAD_DIM, Q_TILE), lambda b, h, i: (b, h, 0, i)),
        out_shape=jax.ShapeDtypeStruct((bsz, N_HEADS, HEAD_DIM, seq), _MM),
        scratch_shapes=[pltpu.VMEM((n_sb, lanes), jnp.float32)],
        compiler_params=_params("arbitrary", "arbitrary", "arbitrary"), name="nsa_attention")(
            q4, g_t, k_cmp, v_cmp_t, k8, v_t, k8, v_t, jnp.asarray(overlap), jnp.asarray(wmask))


def _nsa_mixer(x2, bsz, seq, w_in, cmp_w_k, cmp_w_v, cmp_pe_k, cmp_pe_v):
    q4, k8, kc4, vc4, v_t, g_t = _nsa_proj(x2, bsz, seq, w_in)
    k_cmp, v_cmp_t = _compress(kc4, vc4, cmp_w_k, cmp_w_v, cmp_pe_k, cmp_pe_v)
    o = _nsa_attention(q4, g_t, k_cmp, v_cmp_t, k8, v_t)
    return o.reshape(bsz, Q_WIDTH, seq)


def _proj_t_res_ln_body(yt_ref, res_ref, w_ref, g_ref, b_ref, o_ref):
    mix = lax.dot_general(yt_ref[...], w_ref[...], (((0,), (0,)), ((), ())), preferred_element_type=jnp.float32)
    o_ref[...] = _layer_norm(ALPHA * res_ref[...] + mix, g_ref[...], b_ref[...])


def _proj_t_res_ln(y_t, res, w, g, b):
    bsz, _, seq = y_t.shape
    tm = min(ROW_TILE, seq)
    nt = seq // tm
    consts = [w.astype(_MM), g.reshape(1, D_MODEL), b.reshape(1, D_MODEL)]
    row = lambda c: pl.BlockSpec((tm, c), lambda bi, i: (bi * nt + i, 0))
    return pl.pallas_call(
        _proj_t_res_ln_body, grid=(bsz, nt),
        in_specs=[pl.BlockSpec((None, Q_WIDTH, tm), lambda bi, i: (bi, 0, i)), row(D_MODEL)]
        + [pl.BlockSpec(a.shape, lambda bi, i: (0, 0)) for a in consts],
        out_specs=row(D_MODEL),
        out_shape=jax.ShapeDtypeStruct((bsz * seq, D_MODEL), jnp.float32),
        compiler_params=_params("arbitrary", "arbitrary"), name="out_proj_t_ln")(y_t, res, *consts)


CHUNK = 8
SORT_ROWS = ROW_TILE * TOP_K + N_EXPERTS * CHUNK
CHUNK_SLOTS = 1024


SLOT_RADIX = 64


def _segment_bounds(cnt):
    padded = (((cnt.astype(jnp.int32) + (CHUNK - 1)) // CHUNK) * CHUNK).astype(jnp.float32)
    e_r = lax.broadcasted_iota(jnp.int32, (N_EXPERTS, N_EXPERTS), 0)
    e_c = lax.broadcasted_iota(jnp.int32, (N_EXPERTS, N_EXPERTS), 1)
    upper = jnp.where(e_r < e_c, 1.0, 0.0).astype(jnp.bfloat16)
    start = _dot(jnp.broadcast_to(padded, (8, N_EXPERTS)).astype(jnp.bfloat16), upper)[0:1, :]
    return start, start + padded


def _router_body(x_ref, w_ref, b_ref, gate_ref, slot_ref, cnt_ref):
    logits = jnp.dot(x_ref[...], w_ref[...], preferred_element_type=jnp.float32,
                     precision=lax.Precision.HIGHEST)
    scores = jax.nn.sigmoid(logits)
    tm = scores.shape[0]
    pick = scores + b_ref[...]
    lane = lax.broadcasted_iota(jnp.int32, (tm, N_EXPERTS), 1).astype(jnp.float32)
    member = jnp.zeros((tm, N_EXPERTS), jnp.float32)
    for _ in range(TOP_K):
        best = jnp.max(pick, axis=1, keepdims=True)
        which = jnp.min(jnp.where(pick == best, lane, float(N_EXPERTS)), axis=1, keepdims=True)
        hit = lane == which
        pick = jnp.where(hit, -jnp.inf, pick)
        member = jnp.where(hit, 1.0, member)
    chosen = member > 0.0
    picked = jnp.where(chosen, scores, 0.0)
    gate_ref[...] = picked / jnp.sum(picked, axis=1, keepdims=True) * ROUTE_SCALE

    r_i = lax.broadcasted_iota(jnp.int32, (tm, tm), 0)
    c_i = lax.broadcasted_iota(jnp.int32, (tm, tm), 1)
    tri = jnp.where(c_i < r_i, 1.0, 0.0).astype(jnp.bfloat16)
    ahead = _dot(tri, member.astype(jnp.bfloat16))
    cnt = jnp.sum(member, axis=0, keepdims=True)
    start, _ = _segment_bounds(cnt)
    slot_ref[...] = jnp.where(chosen, start + ahead + 1.0, 0.0)
    cnt_ref[...] = jnp.broadcast_to(cnt, cnt_ref.shape)


def _router(x2, router_w, router_b):
    n = x2.shape[0]
    tm = min(ROW_TILE, n)
    tile = lambda c: pl.BlockSpec((tm, c), lambda i: (i, 0))
    whole = lambda a: pl.BlockSpec(a.shape, lambda i: (0, 0))
    rb = router_b.reshape(1, N_EXPERTS)
    return pl.pallas_call(
        _router_body, grid=(n // tm,),
        in_specs=[tile(D_MODEL), whole(router_w), whole(rb)],
        out_specs=[tile(N_EXPERTS), tile(N_EXPERTS), pl.BlockSpec((8, N_EXPERTS), lambda i: (i, 0))],
        out_shape=[jax.ShapeDtypeStruct((n, N_EXPERTS), jnp.float32), jax.ShapeDtypeStruct((n, N_EXPERTS), jnp.float32),
                   jax.ShapeDtypeStruct((n // tm * 8, N_EXPERTS), jnp.float32)],
        compiler_params=_params("arbitrary"), name="moe_router")(x2, router_w, rb)


def _chunk_copy(src, dst, sem):
    return pltpu.make_async_copy(src, dst, sem)


def _chunk(ref, c):
    return ref.at[pl.ds(pl.multiple_of(c * CHUNK, CHUNK), CHUNK), :]


CHUNK_UNROLL = 4


def _chunk_groups(n_chunks):
    return lax.shift_right_logical(n_chunks, CHUNK_UNROLL.bit_length() - 1)


def _expert_of_row(cnt):
    start, end = _segment_bounds(cnt)
    j = lax.broadcasted_iota(jnp.int32, (SORT_ROWS, N_EXPERTS), 0).astype(jnp.float32)
    return jnp.where((j >= start) & (j < end), 1.0, 0.0).astype(jnp.bfloat16)


def _spread(per_expert, owner):
    return _dot_nt(per_expert.astype(jnp.bfloat16), owner)


def _row_matches(slot, owner):
    hi = jnp.floor(slot * (1.0 / SLOT_RADIX))
    lo = slot - hi * SLOT_RADIX
    place = _spread(hi, owner) * SLOT_RADIX + _spread(lo, owner)
    j1 = lax.broadcasted_iota(jnp.int32, place.shape, 1).astype(jnp.float32) + 1.0
    return place == j1


def _dispatch_body(nch_ref, dst_ref, x_ref, slot_ref, cnt_ref, xs_in_ref, xs_ref, sbuf, sem):
    del xs_in_ref
    at = _row_matches(slot_ref[...], _expert_of_row(cnt_ref[0:1, :]))
    perm_t = jnp.where(at, 1.0, 0.0).astype(jnp.bfloat16)
    rows = lax.dot_general(perm_t, x_ref[...].astype(jnp.bfloat16),
                           (((0,), (0,)), ((), ())), preferred_element_type=jnp.float32)
    sbuf[...] = _pack_rows(rows)
    groups = _chunk_groups(nch_ref[pl.program_id(0)])

    def start(g, carry):
        for u in range(CHUNK_UNROLL):
            c = g * CHUNK_UNROLL + u
            _chunk_copy(_chunk(sbuf, c), _chunk(xs_ref, dst_ref[c]), sem).start()
        return carry

    def wait(g, carry):
        for u in range(CHUNK_UNROLL):
            _chunk_copy(_chunk(sbuf, g * CHUNK_UNROLL + u), _chunk(xs_ref, 0), sem).wait()
        return carry

    lax.fori_loop(0, groups, start, 0)
    lax.fori_loop(0, groups, wait, 0)


def _dispatch(n_chunks, dst_flat, x2, slot, cnt, n_rows):
    n = x2.shape[0]
    tm = min(ROW_TILE, n)
    xs0 = jnp.zeros((n_rows, HALF), jnp.uint32)
    grid_spec = pltpu.PrefetchScalarGridSpec(
        num_scalar_prefetch=1, grid=(n // tm,),
        in_specs=[pl.BlockSpec((CHUNK_SLOTS,), lambda i, nc: (i,), memory_space=pltpu.SMEM),
                  pl.BlockSpec((tm, D_MODEL), lambda i, nc: (i, 0)),
                  pl.BlockSpec((tm, N_EXPERTS), lambda i, nc: (i, 0)),
                  pl.BlockSpec((8, N_EXPERTS), lambda i, nc: (i, 0)),
                  pl.BlockSpec(memory_space=pl.ANY)],
        out_specs=pl.BlockSpec(memory_space=pl.ANY),
        scratch_shapes=[pltpu.VMEM((SORT_ROWS, HALF), jnp.uint32), pltpu.SemaphoreType.DMA(())])
    return pl.pallas_call(
        _dispatch_body, grid_spec=grid_spec,
        out_shape=jax.ShapeDtypeStruct((n_rows, HALF), jnp.uint32),
        input_output_aliases={5: 0},
        compiler_params=_params("arbitrary"), name="moe_dispatch")(n_chunks, dst_flat, x2, slot, cnt, xs0)


def _expert_body(blk_exp_ref, n_used_ref, x_ref, wg_ref, wu_ref, wd_ref, o_ref, wgu_s, wd_s):
    i = pl.program_id(0)
    prev = blk_exp_ref[jnp.maximum(i - 1, 0)]

    @pl.when((i == 0) | (blk_exp_ref[i] != prev))
    def _():
        wgu_s[:, :D_EXPERT] = wg_ref[...].astype(_MM)
        wgu_s[:, D_EXPERT:] = wu_ref[...].astype(_MM)
        wd_s[...] = wd_ref[...].astype(_MM)

    @pl.when(i < n_used_ref[0])
    def _():
        gu = _dot(_unpack_rows(x_ref[...]).astype(_MM), wgu_s[...])
        h = jax.nn.silu(gu[:, :D_EXPERT]) * gu[:, D_EXPERT:]
        o_ref[...] = _pack_rows(_dot(h.astype(_MM), wd_s[...]))

    @pl.when(i >= n_used_ref[0])
    def _():
        o_ref[...] = jnp.zeros_like(o_ref)


def _experts(xs, blk_exp, n_used, w_gate, w_up, w_down):
    n_rows = xs.shape[0]
    n_blk = n_rows // EXPERT_BLOCK
    grid_spec = pltpu.PrefetchScalarGridSpec(
        num_scalar_prefetch=2, grid=(n_blk,),
        in_specs=[pl.BlockSpec((EXPERT_BLOCK, HALF), lambda i, be, nu: (i, 0)),
                  pl.BlockSpec((None, D_MODEL, D_EXPERT), lambda i, be, nu: (be[i], 0, 0)),
                  pl.BlockSpec((None, D_MODEL, D_EXPERT), lambda i, be, nu: (be[i], 0, 0)),
                  pl.BlockSpec((None, D_EXPERT, D_MODEL), lambda i, be, nu: (be[i], 0, 0))],
        out_specs=pl.BlockSpec((EXPERT_BLOCK, HALF), lambda i, be, nu: (i, 0)),
        scratch_shapes=[pltpu.VMEM((D_MODEL, 2 * D_EXPERT), _MM), pltpu.VMEM((D_EXPERT, D_MODEL), _MM)])
    return pl.pallas_call(_expert_body, grid_spec=grid_spec,
                          out_shape=jax.ShapeDtypeStruct((n_rows, HALF), jnp.uint32),
                          compiler_params=_params("arbitrary"), name="moe_experts")(
                              blk_exp, n_used, xs, w_gate, w_up, w_down)


def _combine_body(nch_ref, dcur_ref, dnext_ref, x_ref, slot_ref, cnt_ref, gate_ref, ys_ref, sgu_ref, sd_ref,
                  g_ref, b_ref, o_ref, ybuf, sem):
    i = pl.program_id(0)
    slot = i % 2

    @pl.when(i == 0)
    def _():
        ybuf[...] = jnp.zeros_like(ybuf)

    def gather(d_ref, n, s):
        def body(g, carry):
            for u in range(CHUNK_UNROLL):
                c = g * CHUNK_UNROLL + u
                _chunk_copy(_chunk(ys_ref, d_ref[c]), _chunk(ybuf.at[s], c), sem.at[s]).start()
            return carry
        lax.fori_loop(0, _chunk_groups(n), body, 0)

    @pl.when(i == 0)
    def _():
        gather(dcur_ref, nch_ref[0], 0)

    @pl.when(i + 1 < pl.num_programs(0))
    def _():
        gather(dnext_ref, nch_ref[jnp.minimum(i + 1, pl.num_programs(0) - 1)], 1 - slot)

    def wait(g, carry):
        for u in range(CHUNK_UNROLL):
            _chunk_copy(_chunk(ys_ref, 0), _chunk(ybuf.at[slot], g * CHUNK_UNROLL + u), sem.at[slot]).wait()
        return carry

    lax.fori_loop(0, _chunk_groups(nch_ref[i]), wait, 0)

    x = x_ref[...]
    owner = _expert_of_row(cnt_ref[0:1, :])
    at = _row_matches(slot_ref[...], owner)
    gate = gate_ref[...]
    g_hi = gate.astype(jnp.bfloat16).astype(jnp.float32)
    g_lo = gate - g_hi
    p_hi = jnp.where(at, _spread(g_hi, owner), 0.0).astype(jnp.bfloat16)
    p_lo = jnp.where(at, _spread(g_lo, owner), 0.0).astype(jnp.bfloat16)
    y = _unpack_rows(ybuf[slot]).astype(jnp.bfloat16)
    routed = _dot(p_hi, y) + _dot(p_lo, y)
    gu = _dot(x.astype(_MM), sgu_ref[...])
    hs = jax.nn.silu(gu[:, :D_EXPERT]) * gu[:, D_EXPERT:]
    shared = _dot(hs.astype(_MM), sd_ref[...])
    o_ref[...] = _layer_norm(ALPHA * x + (routed + shared), g_ref[...], b_ref[...])


def _combine(n_chunks, dst_flat, x2, slot, cnt, gate, ys, s_gu, s_down, ln_g, ln_b):
    n = x2.shape[0]
    tm = min(ROW_TILE, n)
    nt = n // tm
    consts = [s_gu, s_down, ln_g.reshape(1, D_MODEL), ln_b.reshape(1, D_MODEL)]
    grid_spec = pltpu.PrefetchScalarGridSpec(
        num_scalar_prefetch=1, grid=(nt,),
        in_specs=[pl.BlockSpec((CHUNK_SLOTS,), lambda i, nc: (i,), memory_space=pltpu.SMEM),
                  pl.BlockSpec((CHUNK_SLOTS,), lambda i, nc: (jnp.minimum(i + 1, nt - 1),), memory_space=pltpu.SMEM),
                  pl.BlockSpec((tm, D_MODEL), lambda i, nc: (i, 0)),
                  pl.BlockSpec((tm, N_EXPERTS), lambda i, nc: (i, 0)),
                  pl.BlockSpec((8, N_EXPERTS), lambda i, nc: (i, 0)),
                  pl.BlockSpec((tm, N_EXPERTS), lambda i, nc: (i, 0)),
                  pl.BlockSpec(memory_space=pl.ANY)]
        + [pl.BlockSpec(a.shape, lambda i, nc: (0, 0)) for a in consts],
        out_specs=pl.BlockSpec((tm, D_MODEL), lambda i, nc: (i, 0)),
        scratch_shapes=[pltpu.VMEM((2, SORT_ROWS, HALF), jnp.uint32), pltpu.SemaphoreType.DMA((2,))])
    return pl.pallas_call(
        _combine_body, grid_spec=grid_spec,
        out_shape=jax.ShapeDtypeStruct((n, D_MODEL), jnp.float32),
        compiler_params=_params("arbitrary"), name="moe_combine")(
            n_chunks, dst_flat, dst_flat, x2, slot, cnt, gate, ys, *consts)


def _moe_ln(x2, router_w, router_b, w_gate, w_up, w_down, s_gate, s_up, s_down, ln_g, ln_b):
    n = x2.shape[0]
    tm = min(ROW_TILE, n)
    nt = n // tm
    gate, slot, cnt8 = _router(x2, router_w, router_b)
    cnt = cnt8[::8].astype(jnp.int32)
    seg = (cnt + CHUNK - 1) // CHUNK * CHUNK
    loc_end = jnp.cumsum(seg, axis=1)
    loc_start = loc_end - seg
    exp_rows = jnp.sum(seg, axis=0)
    padded = (exp_rows + EXPERT_BLOCK - 1) // EXPERT_BLOCK * EXPERT_BLOCK
    pad_end = jnp.cumsum(padded)
    seg_dst = (pad_end - padded)[None, :] + jnp.cumsum(seg, axis=0) - seg
    max_rows = n * TOP_K + nt * N_EXPERTS * (CHUNK - 1)
    n_blk = -(-max_rows // EXPERT_BLOCK) + N_EXPERTS + 1
    n_real = (loc_end[:, -1] // CHUNK).astype(jnp.int32)
    n_chunks = (n_real + CHUNK_UNROLL - 1) // CHUNK_UNROLL * CHUNK_UNROLL
    c_idx = jnp.arange(SORT_ROWS // CHUNK, dtype=jnp.int32)
    c_row = c_idx[None, :, None] * CHUNK
    in_seg = (loc_start[:, None, :] <= c_row) & (c_row < loc_end[:, None, :])
    shift = jnp.sum(jnp.where(in_seg, (seg_dst - loc_start)[:, None, :], 0), axis=2)
    dst = (shift + c_idx[None, :] * CHUNK) // CHUNK
    spare = (n_blk - 1) * (EXPERT_BLOCK // CHUNK)
    dst = jnp.where(c_idx[None, :] < n_real[:, None], dst, spare + c_idx[None, :] - n_real[:, None])
    dst_flat = jnp.pad(dst, ((0, 0), (0, CHUNK_SLOTS - dst.shape[1]))).reshape(nt * CHUNK_SLOTS).astype(jnp.int32)
    blk_row = jnp.arange(n_blk, dtype=jnp.int32) * EXPERT_BLOCK
    blk_exp = jnp.minimum(jnp.sum((pad_end[None, :] <= blk_row[:, None]).astype(jnp.int32), axis=1), N_EXPERTS - 1)
    n_used = (pad_end[-1:] // EXPERT_BLOCK).astype(jnp.int32)
    xs = _dispatch(n_chunks, dst_flat, x2, slot, cnt8, n_blk * EXPERT_BLOCK)
    ys = _experts(xs, blk_exp, n_used, w_gate, w_up, w_down)
    s_gu = jnp.concatenate([s_gate, s_up], axis=-1).astype(_MM)
    return _combine(n_chunks, dst_flat, x2, slot, cnt8, gate, ys, s_gu, s_down.astype(_MM), ln_g, ln_b)


def kernel(x, ev_w_in, ev_b_in, s5_lam_re, s5_lam_im, s5_log_dt, s5_b_re, s5_b_im, s5_c_re, s5_c_im, s5_d, s5_w_glu, s5_b_glu, cv_w, cv_b, cv_ln_g, cv_ln_b, ev_w_out, od_w_in, cmp_w_k, cmp_w_v, cmp_pe_k, cmp_pe_v, od_w_out, ln1_g, ln1_b, ln2_g, ln2_b, router_w, router_b, ex_w_gate, ex_w_up, ex_w_down, sh_w_gate, sh_w_up, sh_w_down):
    bsz, seq, _ = x.shape
    h = x.reshape(bsz * seq, D_MODEL)
    for layer in range(DEPTH):
        i = layer // 2
        if layer % 2 == 0:
            mix = _even_mixer(h, bsz, seq, ev_w_in[i], ev_b_in[i], s5_lam_re[i], s5_lam_im[i], s5_log_dt[i],
                              s5_b_re[i], s5_b_im[i], s5_c_re[i], s5_c_im[i], s5_d[i].reshape(-1),
                              s5_w_glu[i], s5_b_glu[i], cv_w[i], cv_b[i], cv_ln_g[i], cv_ln_b[i])
            (h,) = _row_call(_proj_res_ln_body, [mix, h],
                             [ev_w_out[i].astype(_MM), ln1_g[layer].reshape(1, D_MODEL),
                              ln1_b[layer].reshape(1, D_MODEL)],
                             [(D_MODEL, jnp.float32)], "out_proj_ln")
        else:
            mix_t = _nsa_mixer(h, bsz, seq, od_w_in[i], cmp_w_k[i], cmp_w_v[i], cmp_pe_k[i], cmp_pe_v[i])
            h = _proj_t_res_ln(mix_t, h, od_w_out[i], ln1_g[layer], ln1_b[layer])
        h = _moe_ln(h, router_w[layer], router_b[layer], ex_w_gate[layer], ex_w_up[layer], ex_w_down[layer],
                    sh_w_gate[layer], sh_w_up[layer], sh_w_down[layer], ln2_g[layer], ln2_b[layer])
    return h.reshape(bsz, seq, D_MODEL)
```

```python
import functools
import math

import numpy as np
import jax
import jax.numpy as jnp
from jax import lax
from jax.experimental import pallas as pl
from jax.experimental.pallas import tpu as pltpu

D_MODEL = 1024
DEPTH = 4
ALPHA = (2.0 * DEPTH) ** 0.25
LN_EPS = 1e-5
NEG_INF = -1e30

D_S5 = 512
S5_GROUP = 16
S5_GROUPS = 32
S5_STATE = 64
S5_LANES = S5_GROUPS * S5_STATE
D_CONV = 512
CONV_WIDTH = 31
EVEN_IN = D_S5 + 2 * D_CONV

HEAD_DIM = 64
N_HEADS = 16
N_KV_HEADS = 4
GQA = 4
CMP_LEN = 32
CMP_STRIDE = 16
SEL_LEN = 64
N_SELECT = 16
WINDOW = 512
FORCE_BONUS = 1e4
ROPE_THETA = 10000.0
Q_WIDTH = N_HEADS * HEAD_DIM
KV_WIDTH = N_KV_HEADS * HEAD_DIM
GATE_WIDTH = 3 * N_HEADS

N_EXPERTS = 64
TOP_K = 8
D_EXPERT = 256
ROUTE_SCALE = 2.5

LANE = 128
_MM = jnp.bfloat16
_VMEM_LIMIT = 56 * 1024 * 1024

ROW_TILE = 256
PROJ_TILE = 512
SCAN_ROWS = 128
SEQ_TILE = 256
CONV_HALO = 32
Q_TILE = 256
KEY_CHUNK = 1024
EXPERT_BLOCK = 512


def _dot(a, b):
    return jnp.dot(a, b, preferred_element_type=jnp.float32)


def _dot_nt(a, b):
    return lax.dot_general(a, b, (((1,), (1,)), ((), ())), preferred_element_type=jnp.float32)


def _layer_norm(x, g, b):
    mu = jnp.mean(x, axis=-1, keepdims=True)
    xc = x - mu
    var = jnp.mean(xc * xc, axis=-1, keepdims=True)
    return xc * lax.rsqrt(var + LN_EPS) * g + b


HALF = D_MODEL // 2


def _pack_rows(x):
    bits = lambda v: lax.bitcast_convert_type(v.astype(jnp.bfloat16).astype(jnp.float32), jnp.uint32)
    return (bits(x[:, HALF:]) & jnp.uint32(0xFFFF0000)) | (bits(x[:, :HALF]) >> 16)


def _unpack_rows(w):
    lo = lax.bitcast_convert_type(w << 16, jnp.float32)
    hi = lax.bitcast_convert_type(w & jnp.uint32(0xFFFF0000), jnp.float32)
    return jnp.concatenate([lo, hi], axis=1)


def _params(*sem):
    return pltpu.CompilerParams(dimension_semantics=sem, vmem_limit_bytes=_VMEM_LIMIT)


def _row_call(body, row_ins, const_ins, outs, name, tm=PROJ_TILE):
    m = row_ins[0].shape[0]
    tm = min(tm, m)
    assert m % tm == 0
    in_specs = [pl.BlockSpec((tm, a.shape[1]), lambda i: (i, 0)) for a in row_ins]
    in_specs += [pl.BlockSpec(a.shape, functools.partial(lambda nd, i: (0,) * nd, a.ndim)) for a in const_ins]
    out_specs = [pl.BlockSpec((tm, c), lambda i: (i, 0)) for c, _ in outs]
    out_shape = [jax.ShapeDtypeStruct((m, c), dt) for c, dt in outs]
    return pl.pallas_call(body, grid=(m // tm,), in_specs=in_specs, out_specs=out_specs,
                          out_shape=out_shape, compiler_params=_params("arbitrary"), name=name)(
                              *row_ins, *const_ins)


def _proj_bias_body(x_ref, w_ref, b_ref, o_ref):
    o_ref[...] = _dot(x_ref[...].astype(_MM), w_ref[...]) + b_ref[...]


def _proj_res_ln_body(y_ref, res_ref, w_ref, g_ref, b_ref, o_ref):
    mix = _dot(y_ref[...].astype(_MM), w_ref[...])
    o_ref[...] = _layer_norm(ALPHA * res_ref[...] + mix, g_ref[...], b_ref[...])


def _s5_tables(lam_re, lam_im, log_dt, b_re, b_im, c_re, c_im):
    f32 = jnp.float32
    dt = jnp.exp(log_dt.astype(f32))[:, None]
    decay = jnp.exp(lam_re * dt)
    a_re, a_im = decay * jnp.cos(lam_im * dt), decay * jnp.sin(lam_im * dt)
    den = lam_re ** 2 + lam_im ** 2
    f_re = ((a_re - 1.0) * lam_re + a_im * lam_im) / den
    f_im = (a_im * lam_re - (a_re - 1.0) * lam_im) / den
    bb_re = f_re[..., None] * b_re - f_im[..., None] * b_im
    bb_im = f_re[..., None] * b_im + f_im[..., None] * b_re

    gl = LANE // S5_GROUP
    nj = S5_GROUPS // gl
    eye = jnp.eye(gl, dtype=f32)

    def in_blocks(bb):
        t = bb.reshape(nj, gl, S5_STATE, S5_GROUP)
        t = jnp.einsum('jgph,gk->jghkp', t, eye)
        return t.reshape(nj, gl * S5_GROUP, gl * S5_STATE).astype(_MM)

    def out_blocks(c):
        t = c.reshape(nj, gl, S5_GROUP, S5_STATE)
        t = jnp.einsum('jghp,gk->jgpkh', t, eye)
        return t.reshape(nj, gl * S5_STATE, gl * S5_GROUP).astype(_MM)

    ar, ai = a_re.reshape(1, S5_LANES), a_im.reshape(1, S5_LANES)
    pows_r, pows_i = [ar], [ai]
    for _ in range(int(math.log2(SCAN_ROWS)) - 1):
        pr, pi = pows_r[-1], pows_i[-1]
        pows_r.append(pr * pr - pi * pi)
        pows_i.append(2.0 * pr * pi)
    tr, ti = ar, ai
    for k in range(int(math.log2(SCAN_ROWS))):
        pr, pi = pows_r[k], pows_i[k]
        tr, ti = (jnp.concatenate([tr, tr * pr - ti * pi], axis=0),
                  jnp.concatenate([ti, tr * pi + ti * pr], axis=0))
    nlb = S5_LANES // LANE
    to3 = lambda t: t.reshape(t.shape[0], nlb, LANE).transpose(1, 0, 2)
    return (in_blocks(bb_re), in_blocks(bb_im), out_blocks(c_re), out_blocks(c_im),
            to3(jnp.concatenate(pows_r, axis=0)), to3(jnp.concatenate(pows_i, axis=0)), to3(tr), to3(ti))


def _s5conv_body(h_ref, bre_ref, bim_ref, cre_ref, cim_ref, pwr_ref, pwi_ref, tbr_ref, tbi_ref,
                 d_ref, wglu_ref, bglu_ref, cvw_ref, cvb_ref, lng_ref, lnb_ref, o_ref,
                 st_re, st_im, xr_ref, xi_ref, hbuf):
    tile = h_ref.shape[0]
    n_steps = int(math.log2(SCAN_ROWS))
    nlb = S5_LANES // LANE
    nj = bre_ref.shape[0]
    per_j = nlb // nj

    @pl.when(pl.program_id(1) == 0)
    def _():
        st_re[...] = jnp.zeros_like(st_re)
        st_im[...] = jnp.zeros_like(st_im)
        hbuf[0:CONV_HALO, :] = jnp.zeros((CONV_HALO, D_CONV), jnp.float32)

    row = lax.broadcasted_iota(jnp.int32, (SCAN_ROWS, LANE), 0)

    for c in range(tile // SCAN_ROWS):
        r0 = c * SCAN_ROWS
        u = h_ref[r0:r0 + SCAN_ROWS, 0:D_S5]
        ub = u.astype(_MM)
        for j in range(nj):
            uj = ub[:, j * LANE:(j + 1) * LANE]
            br = _dot(uj, bre_ref[j])
            bi = _dot(uj, bim_ref[j])
            for q in range(per_j):
                xr_ref[j * per_j + q] = br[:, q * LANE:(q + 1) * LANE]
                xi_ref[j * per_j + q] = bi[:, q * LANE:(q + 1) * LANE]

        def scan_block(lb, carry):
            xr, xi = xr_ref[lb], xi_ref[lb]
            for k in range(n_steps):
                d = 1 << k
                ar = pwr_ref[lb, k:k + 1, :]
                ai = pwi_ref[lb, k:k + 1, :]
                keep = row >= d
                sr = jnp.where(keep, pltpu.roll(xr, d, 0), 0.0)
                si = jnp.where(keep, pltpu.roll(xi, d, 0), 0.0)
                xr, xi = xr + ar * sr - ai * si, xi + ar * si + ai * sr
            pr, pi = st_re[lb, 0:1, :], st_im[lb, 0:1, :]
            tr, ti = tbr_ref[lb], tbi_ref[lb]
            xr, xi = xr + tr * pr - ti * pi, xi + tr * pi + ti * pr
            xr_ref[lb] = xr
            xi_ref[lb] = xi
            st_re[lb, 0:1, :] = xr[SCAN_ROWS - 1:SCAN_ROWS, :]
            st_im[lb, 0:1, :] = xi[SCAN_ROWS - 1:SCAN_ROWS, :]
            return carry

        lax.fori_loop(0, nlb, scan_block, 0)

        ys = []
        for j in range(nj):
            xr = jnp.concatenate([xr_ref[j * per_j + q] for q in range(per_j)], axis=1).astype(_MM)
            xi = jnp.concatenate([xi_ref[j * per_j + q] for q in range(per_j)], axis=1).astype(_MM)
            ys.append(_dot(xr, cre_ref[j]) - _dot(xi, cim_ref[j]))
        y = jnp.concatenate(ys, axis=1) + d_ref[...] * u
        y = jax.nn.gelu(y)
        y = y * jax.nn.sigmoid(_dot(y.astype(_MM), wglu_ref[...]) + bglu_ref[...])
        o_ref[r0:r0 + SCAN_ROWS, 0:D_S5] = y.astype(o_ref.dtype)

    val = h_ref[:, D_S5:D_S5 + D_CONV]
    gate = h_ref[:, D_S5 + D_CONV:D_S5 + 2 * D_CONV]
    hbuf[CONV_HALO:CONV_HALO + tile, :] = val * jax.nn.sigmoid(gate)
    off = CONV_HALO - (CONV_WIDTH - 1)
    acc = jnp.zeros((tile, D_CONV), jnp.float32)
    for k in range(CONV_WIDTH):
        acc = acc + cvw_ref[k:k + 1, :] * hbuf[off + k:off + k + tile, :]
    acc = acc + cvb_ref[...]
    yb = _layer_norm(acc, lng_ref[...], lnb_ref[...])
    o_ref[:, D_S5:D_S5 + D_CONV] = (yb * jax.nn.sigmoid(yb)).astype(o_ref.dtype)
    hbuf[0:CONV_HALO, :] = hbuf[tile:tile + CONV_HALO, :]


def _s5conv(h, tables, d_skip, w_glu, b_glu, cv_w, cv_b, ln_g, ln_b):
    bsz, seq, _ = h.shape
    tile = min(SEQ_TILE, seq)
    assert seq % tile == 0 and tile % SCAN_ROWS == 0
    bre, bim, cre, cim, pwr, pwi, tbr, tbi = tables
    consts = [bre, bim, cre, cim, pwr, pwi, tbr, tbi,
              d_skip.reshape(1, D_S5), w_glu.astype(_MM), b_glu.reshape(1, D_S5),
              cv_w, cv_b.reshape(1, D_CONV), ln_g.reshape(1, D_CONV), ln_b.reshape(1, D_CONV)]
    nlb = S5_LANES // LANE
    in_specs = [pl.BlockSpec((None, tile, EVEN_IN), lambda b, l: (b, l, 0))]
    in_specs += [pl.BlockSpec(a.shape, functools.partial(lambda nd, b, l: (0,) * nd, a.ndim)) for a in consts]
    return pl.pallas_call(
        _s5conv_body, grid=(bsz, seq // tile), in_specs=in_specs,
        out_specs=pl.BlockSpec((None, tile, D_S5 + D_CONV), lambda b, l: (b, l, 0)),
        out_shape=jax.ShapeDtypeStruct((bsz, seq, D_S5 + D_CONV), _MM),
        scratch_shapes=[pltpu.VMEM((nlb, 8, LANE), jnp.float32), pltpu.VMEM((nlb, 8, LANE), jnp.float32),
                        pltpu.VMEM((nlb, SCAN_ROWS, LANE), jnp.float32),
                        pltpu.VMEM((nlb, SCAN_ROWS, LANE), jnp.float32),
                        pltpu.VMEM((tile + CONV_HALO, D_CONV), jnp.float32)],
        compiler_params=_params("arbitrary", "arbitrary"), name="s5conv")(h, *consts)


def _even_mixer(x2, bsz, seq, w_in, b_in, lam_re, lam_im, log_dt, b_re, b_im, c_re, c_im, d_skip,
                w_glu, b_glu, cv_w, cv_b, cv_ln_g, cv_ln_b):
    (h,) = _row_call(_proj_bias_body, [x2], [w_in.astype(_MM), b_in.reshape(1, EVEN_IN)],
                     [(EVEN_IN, jnp.float32)], "even_in_proj")
    tables = _s5_tables(lam_re, lam_im, log_dt, b_re, b_im, c_re, c_im)
    y = _s5conv(h.reshape(bsz, seq, EVEN_IN), tables, d_skip, w_glu, b_glu, cv_w, cv_b, cv_ln_g, cv_ln_b)
    return y.reshape(bsz * seq, D_S5 + D_CONV)


ROPE_W = Q_WIDTH + 3 * KV_WIDTH
TOK_W = ROPE_W + KV_WIDTH
GATE_ROWS = 16
LOG2E = 1.4426950408889634


def _nsa_proj_body(x_ref, w_ref, wt_ref, cos_ref, sin_ref, q_ref, k_ref, kc_ref, vc_ref, vt_ref, gt_ref):
    xb = x_ref[...].astype(_MM)
    y = _dot(xb, w_ref[...])
    r = y[:, :ROPE_W]
    reps = ROPE_W // LANE
    cos = jnp.concatenate([cos_ref[...]] * reps, axis=1)
    sin = jnp.concatenate([sin_ref[...]] * reps, axis=1)
    lane = lax.broadcasted_iota(jnp.int32, r.shape, 1)
    half = HEAD_DIM // 2
    first = (lane % HEAD_DIM) < half
    rot = jnp.where(first, -pltpu.roll(r, ROPE_W - half, 1), pltpu.roll(r, half, 1))
    r = r * cos + rot * sin
    qs = r[:, :Q_WIDTH] * (HEAD_DIM ** -0.5 * LOG2E)
    for hd in range(N_HEADS):
        q_ref[hd] = qs[:, hd * HEAD_DIM:(hd + 1) * HEAD_DIM].astype(q_ref.dtype)
    for j in range(2 * N_KV_HEADS):
        c0 = Q_WIDTH + j * HEAD_DIM
        k_ref[j] = r[:, c0:c0 + HEAD_DIM].astype(k_ref.dtype)
    for j in range(N_KV_HEADS):
        c0 = Q_WIDTH + 2 * KV_WIDTH + j * HEAD_DIM
        kc_ref[j] = r[:, c0:c0 + HEAD_DIM]
        vc_ref[j] = y[:, ROPE_W + j * HEAD_DIM:ROPE_W + (j + 1) * HEAD_DIM]
    yt = _dot_nt(wt_ref[...], xb)
    vt_ref[...] = yt[:2 * KV_WIDTH, :].astype(vt_ref.dtype)
    gt_ref[...] = yt[2 * KV_WIDTH:, :]


def _nsa_proj(x2, bsz, seq, w_in):
    tm = min(ROW_TILE, seq)
    nt = seq // tm
    sizes = [Q_WIDTH] + [KV_WIDTH] * 6 + [GATE_WIDTH]
    offs = np.cumsum([0] + sizes)
    cols = lambda i: w_in[:, offs[i]:offs[i + 1]]
    w_tok = jnp.concatenate([cols(0), cols(3), cols(5), cols(1), cols(2)], axis=1).astype(_MM)
    gcols = cols(7).reshape(D_MODEL, N_KV_HEADS, GQA, 3).transpose(0, 1, 3, 2).reshape(D_MODEL, N_KV_HEADS, 3 * GQA)
    gcols = jnp.pad(gcols, ((0, 0), (0, 0), (0, GATE_ROWS - 3 * GQA))).reshape(D_MODEL, N_KV_HEADS * GATE_ROWS)
    w_t = jnp.concatenate([cols(4), cols(6), gcols], axis=1).T.astype(_MM)
    half = HEAD_DIM // 2
    inv = ROPE_THETA ** (-jnp.arange(half, dtype=jnp.float32) / half)
    ang = jnp.arange(seq, dtype=jnp.float32)[:, None] * inv[None, :]
    cos = jnp.tile(jnp.cos(ang), (1, LANE // half))
    sin = jnp.tile(jnp.sin(ang), (1, LANE // half))
    heads = lambda nh: pl.BlockSpec((None, nh, tm, HEAD_DIM), lambda b, i: (b, 0, i, 0))
    rows_t = lambda nr: pl.BlockSpec((None, nr, tm), lambda b, i: (b, 0, i))
    n_gate = N_KV_HEADS * GATE_ROWS
    return pl.pallas_call(
        _nsa_proj_body, grid=(bsz, nt),
        in_specs=[pl.BlockSpec((tm, D_MODEL), lambda b, i: (b * nt + i, 0)),
                  pl.BlockSpec(w_tok.shape, lambda b, i: (0, 0)),
                  pl.BlockSpec(w_t.shape, lambda b, i: (0, 0)),
                  pl.BlockSpec((tm, LANE), lambda b, i: (i, 0)),
                  pl.BlockSpec((tm, LANE), lambda b, i: (i, 0))],
        out_specs=[heads(N_HEADS), heads(2 * N_KV_HEADS), heads(N_KV_HEADS), heads(N_KV_HEADS),
                   rows_t(2 * KV_WIDTH), rows_t(n_gate)],
        out_shape=[jax.ShapeDtypeStruct((bsz, N_HEADS, seq, HEAD_DIM), _MM),
                   jax.ShapeDtypeStruct((bsz, 2 * N_KV_HEADS, seq, HEAD_DIM), _MM),
                   jax.ShapeDtypeStruct((bsz, N_KV_HEADS, seq, HEAD_DIM), jnp.float32),
                   jax.ShapeDtypeStruct((bsz, N_KV_HEADS, seq, HEAD_DIM), jnp.float32),
                   jax.ShapeDtypeStruct((bsz, 2 * KV_WIDTH, seq), _MM),
                   jax.ShapeDtypeStruct((bsz, n_gate, seq), jnp.float32)],
        compiler_params=_params("arbitrary", "arbitrary"), name="nsa_in_proj")(x2, w_tok, w_t, cos, sin)


def _compress_body(k_ref, v_ref, pk_ref, pv_ref, wk_ref, wvt_ref, ko_ref, vot_ref):
    rows = k_ref.shape[0]
    xk = k_ref[...]
    lo = _dot((xk + pk_ref[0:1, :]).astype(_MM), wk_ref[0])
    hi = _dot((xk + pk_ref[1:2, :]).astype(_MM), wk_ref[1])
    ko_ref[...] = (lo + pltpu.roll(hi, rows - 1, 0)).astype(ko_ref.dtype)
    xv = v_ref[...]
    lo_t = _dot_nt(wvt_ref[0], (xv + pv_ref[0:1, :]).astype(_MM))
    hi_t = _dot_nt(wvt_ref[1], (xv + pv_ref[1:2, :]).astype(_MM))
    vot_ref[...] = (lo_t + pltpu.roll(hi_t, rows - 1, 1)).astype(vot_ref.dtype)


def _compress(kc, vc, cmp_w_k, cmp_w_v, cmp_pe_k, cmp_pe_v):
    bsz, kvh, seq, _ = kc.shape
    grp = seq // CMP_STRIDE
    flat = CMP_STRIDE * HEAD_DIM
    k2 = kc.reshape(bsz * kvh * grp, flat)
    v2 = vc.reshape(bsz * kvh * grp, flat)
    consts = [cmp_pe_k.reshape(2, flat), cmp_pe_v.reshape(2, flat),
              cmp_w_k.reshape(2, flat, HEAD_DIM).astype(_MM),
              cmp_w_v.reshape(2, flat, HEAD_DIM).transpose(0, 2, 1).astype(_MM)]
    ko, vot = pl.pallas_call(
        _compress_body, grid=(bsz * kvh,),
        in_specs=[pl.BlockSpec((grp, flat), lambda i: (i, 0)), pl.BlockSpec((grp, flat), lambda i: (i, 0))]
        + [pl.BlockSpec(a.shape, functools.partial(lambda nd, i: (0,) * nd, a.ndim)) for a in consts],
        out_specs=[pl.BlockSpec((grp, HEAD_DIM), lambda i: (i, 0)),
                   pl.BlockSpec((None, HEAD_DIM, grp), lambda i: (i, 0, 0))],
        out_shape=[jax.ShapeDtypeStruct((bsz * kvh * grp, HEAD_DIM), _MM),
                   jax.ShapeDtypeStruct((bsz * kvh, HEAD_DIM, grp), _MM)],
        compiler_params=_params("arbitrary"), name="nsa_compress")(k2, v2, *consts)
    return ko.reshape(bsz, kvh, grp, HEAD_DIM), vot.reshape(bsz, kvh, HEAD_DIM, grp)


def _nsa_attn_body(q_ref, g_ref, kc_ref, vct_ref, ks_ref, vst_ref, kw_ref, vwt_ref, ov_ref, wm_ref, o_ref, sel_ref):
    qi = pl.program_id(2)
    q0 = qi * Q_TILE
    lanes = GQA * Q_TILE
    n_grp = kc_ref.shape[0]
    n_sb = ov_ref.shape[0]

    q = q_ref[...].reshape(lanes, HEAD_DIM)
    t_lane = q0 + (lax.broadcasted_iota(jnp.int32, (1, lanes), 1) % Q_TILE)

    s = _dot_nt(kc_ref[...], q)
    n_idx = lax.broadcasted_iota(jnp.int32, (n_grp, lanes), 0)
    vis = (n_idx * CMP_STRIDE + (CMP_LEN - 1)) <= t_lane
    s = jnp.where(vis, s, NEG_INF)
    e = jnp.where(vis, jnp.exp2(s - jnp.max(s, axis=0, keepdims=True)), 0.0)
    den = jnp.sum(e, axis=0, keepdims=True)
    p_c = e / jnp.where(den > 0.0, den, 1.0)
    o_c = _dot(vct_ref[...], p_c.astype(_MM))

    psum = p_c[:, 0:Q_TILE]
    for g in range(1, GQA):
        psum = psum + p_c[:, g * Q_TILE:(g + 1) * Q_TILE]
    imp = jnp.dot(ov_ref[...], psum, preferred_element_type=jnp.float32,
                  precision=lax.Precision.HIGHEST)
    blk = lax.broadcasted_iota(jnp.int32, (n_sb, Q_TILE), 0)
    cur = t_lane[:, 0:Q_TILE] // SEL_LEN
    allowed = blk <= cur
    forced = (blk == 0) | (blk == cur) | (blk == cur - 1)
    score = jnp.where(allowed, imp + jnp.where(forced, FORCE_BONUS, 0.0), -1.0)
    sel_ref[0:n_sb, 0:Q_TILE] = score
    row_group = 8

    def rank_group(g, rank):
        rows = sel_ref[pl.ds(pl.multiple_of(g * row_group, row_group), row_group), 0:Q_TILE]
        for r in range(row_group):
            sj = rows[r:r + 1, :]
            ahead = (sj == score) & (blk > g * row_group + r)
            rank = rank + jnp.where(sj > score, 1.0, jnp.where(ahead, 1.0, 0.0))
        return rank

    n_groups = (q0 + Q_TILE - 1) // (SEL_LEN * row_group) + 1
    rank = lax.fori_loop(0, n_groups, rank_group, jnp.zeros((n_sb, Q_TILE), jnp.float32))
    chosen = jnp.where((rank < float(N_SELECT)) & (score >= 0.0), 1.0, 0.0)
    sel_ref[0:n_sb, :] = jnp.concatenate([chosen] * GQA, axis=1)

    kpos_blk = lax.broadcasted_iota(jnp.int32, (SEL_LEN, lanes), 0)

    per_chunk = KEY_CHUNK // SEL_LEN

    def sel_chunk(c, carry, causal):
        m_run, l_run, acc = carry
        k0 = pl.multiple_of(c * KEY_CHUNK, KEY_CHUNK)
        sc = _dot_nt(ks_ref[pl.ds(k0, KEY_CHUNK), :], q)
        parts = []
        sel_rows = sel_ref[pl.ds(pl.multiple_of(c * per_chunk, per_chunk), per_chunk), :]
        for j in range(per_chunk):
            ok = sel_rows[j:j + 1, :] > 0.5
            if causal:
                ok = ok & ((k0 + j * SEL_LEN + kpos_blk) <= t_lane)
            parts.append(jnp.where(ok, sc[j * SEL_LEN:(j + 1) * SEL_LEN, :], NEG_INF))
        sc = jnp.concatenate(parts, axis=0)
        m_new = jnp.maximum(m_run, jnp.max(sc, axis=0, keepdims=True))
        scale = jnp.exp2(m_run - m_new)
        p = jnp.exp2(sc - m_new)
        l_new = scale * l_run + jnp.sum(p, axis=0, keepdims=True)
        acc = scale * acc + _dot(vst_ref[:, pl.ds(k0, KEY_CHUNK)], p.astype(_MM))
        return m_new, l_new, acc

    n_full = q0 // KEY_CHUNK
    init = (jnp.full((1, lanes), NEG_INF, jnp.float32), jnp.zeros((1, lanes), jnp.float32),
            jnp.zeros((HEAD_DIM, lanes), jnp.float32))
    carry = lax.fori_loop(0, n_full, functools.partial(sel_chunk, causal=False), init)
    _, l_s, acc_s = sel_chunk(n_full, carry, causal=True)
    o_s = acc_s / l_s

    span = WINDOW + Q_TILE
    back = jnp.minimum(q0, WINDOW)
    w0 = pl.multiple_of(q0 - back, Q_TILE)
    m0 = pl.multiple_of(WINDOW - back, Q_TILE)
    sw = _dot_nt(kw_ref[pl.ds(w0, span), :], q) + wm_ref[pl.ds(m0, span), :]
    pw = jnp.exp2(sw - jnp.max(sw, axis=0, keepdims=True))
    o_w = _dot(vwt_ref[:, pl.ds(w0, span)], pw.astype(_MM)) / jnp.sum(pw, axis=0, keepdims=True)

    gs = jax.nn.sigmoid(g_ref[...])
    gate = lambda c: jnp.concatenate([gs[c * GQA + g:c * GQA + g + 1, :] for g in range(GQA)], axis=1)
    out = gate(0) * o_c + gate(1) * o_s + gate(2) * o_w
    for g in range(GQA):
        o_ref[g] = out[:, g * Q_TILE:(g + 1) * Q_TILE].astype(o_ref.dtype)


def _nsa_attention(q4, g_t, k_cmp, v_cmp_t, k8, v_t):
    bsz, _, seq, _ = q4.shape
    n_grp = k_cmp.shape[2]
    n_sb = seq // SEL_LEN
    lanes = GQA * Q_TILE
    assert seq % KEY_CHUNK == 0 and seq >= WINDOW + Q_TILE and n_sb % (KEY_CHUNK // SEL_LEN) == 0
    c_start = np.arange(n_grp) * CMP_STRIDE
    s_start = np.arange(n_sb) * SEL_LEN
    overlap = ((c_start[None, :] < s_start[:, None] + SEL_LEN)
               & (c_start[None, :] + CMP_LEN > s_start[:, None])
               & (np.arange(n_grp)[None, :] < n_grp - 1)).astype(np.float32)
    u = np.arange(2 * WINDOW + Q_TILE)[:, None]
    ql = (np.arange(lanes) % Q_TILE)[None, :]
    wmask = np.where((u > ql) & (u <= WINDOW + ql), 0.0, NEG_INF).astype(np.float32)
    nkv = N_KV_HEADS
    return pl.pallas_call(
        _nsa_attn_body, grid=(bsz, nkv, seq // Q_TILE),
        in_specs=[pl.BlockSpec((None, GQA, Q_TILE, HEAD_DIM), lambda b, h, i: (b, h, i, 0)),
                  pl.BlockSpec((None, GATE_ROWS, Q_TILE), lambda b, h, i: (b, h, i)),
                  pl.BlockSpec((None, None, n_grp, HEAD_DIM), lambda b, h, i: (b, h, 0, 0)),
                  pl.BlockSpec((None, None, HEAD_DIM, n_grp), lambda b, h, i: (b, h, 0, 0)),
                  pl.BlockSpec((None, None, seq, HEAD_DIM), lambda b, h, i: (b, h, 0, 0)),
                  pl.BlockSpec((None, HEAD_DIM, seq), lambda b, h, i: (b, h, 0)),
                  pl.BlockSpec((None, None, seq, HEAD_DIM), lambda b, h, i: (b, nkv + h, 0, 0)),
                  pl.BlockSpec((None, HEAD_DIM, seq), lambda b, h, i: (b, nkv + h, 0)),
                  pl.BlockSpec(overlap.shape, lambda b, h, i: (0, 0)),
                  pl.BlockSpec(wmask.shape, lambda b, h, i: (0, 0))],
        out_specs=pl.BlockSpec((None, GQA, HEAD_DIM, Q_TILE), lambda b, h, i: (b, h, 0, i)),
        out_shape=jax.ShapeDtypeStruct((bsz, N_HEADS, HEAD_DIM, seq), _MM),
        scratch_shapes=[pltpu.VMEM((n_sb, lanes), jnp.float32)],
        compiler_params=_params("arbitrary", "arbitrary", "arbitrary"), name="nsa_attention")(
            q4, g_t, k_cmp, v_cmp_t, k8, v_t, k8, v_t, jnp.asarray(overlap), jnp.asarray(wmask))


def _nsa_mixer(x2, bsz, seq, w_in, cmp_w_k, cmp_w_v, cmp_pe_k, cmp_pe_v):
    q4, k8, kc4, vc4, v_t, g_t = _nsa_proj(x2, bsz, seq, w_in)
    k_cmp, v_cmp_t = _compress(kc4, vc4, cmp_w_k, cmp_w_v, cmp_pe_k, cmp_pe_v)
    o = _nsa_attention(q4, g_t, k_cmp, v_cmp_t, k8, v_t)
    return o.reshape(bsz, Q_WIDTH, seq)


def _proj_t_res_ln_body(yt_ref, res_ref, w_ref, g_ref, b_ref, o_ref):
    mix = lax.dot_general(yt_ref[...], w_ref[...], (((0,), (0,)), ((), ())), preferred_element_type=jnp.float32)
    o_ref[...] = _layer_norm(ALPHA * res_ref[...] + mix, g_ref[...], b_ref[...])


def _proj_t_res_ln(y_t, res, w, g, b):
    bsz, _, seq = y_t.shape
    tm = min(PROJ_TILE, seq)
    nt = seq // tm
    consts = [w.astype(_MM), g.reshape(1, D_MODEL), b.reshape(1, D_MODEL)]
    row = lambda c: pl.BlockSpec((tm, c), lambda bi, i: (bi * nt + i, 0))
    return pl.pallas_call(
        _proj_t_res_ln_body, grid=(bsz, nt),
        in_specs=[pl.BlockSpec((None, Q_WIDTH, tm), lambda bi, i: (bi, 0, i)), row(D_MODEL)]
        + [pl.BlockSpec(a.shape, lambda bi, i: (0, 0)) for a in consts],
        out_specs=row(D_MODEL),
        out_shape=jax.ShapeDtypeStruct((bsz * seq, D_MODEL), jnp.float32),
        compiler_params=_params("arbitrary", "arbitrary"), name="out_proj_t_ln")(y_t, res, *consts)


CHUNK = 8
SORT_ROWS = ROW_TILE * TOP_K + N_EXPERTS * CHUNK
CHUNK_SLOTS = 1024


SLOT_RADIX = 64


def _segment_bounds(cnt):
    padded = (((cnt.astype(jnp.int32) + (CHUNK - 1)) // CHUNK) * CHUNK).astype(jnp.float32)
    e_r = lax.broadcasted_iota(jnp.int32, (N_EXPERTS, N_EXPERTS), 0)
    e_c = lax.broadcasted_iota(jnp.int32, (N_EXPERTS, N_EXPERTS), 1)
    upper = jnp.where(e_r < e_c, 1.0, 0.0).astype(jnp.bfloat16)
    start = _dot(jnp.broadcast_to(padded, (8, N_EXPERTS)).astype(jnp.bfloat16), upper)[0:1, :]
    return start, start + padded


def _router_body(x_ref, w_ref, b_ref, gate_ref, slot_ref, cnt_ref):
    logits = jnp.dot(x_ref[...], w_ref[...], preferred_element_type=jnp.float32,
                     precision=lax.Precision.HIGHEST)
    scores = jax.nn.sigmoid(logits)
    tm = scores.shape[0]
    pick = scores + b_ref[...]
    lane = lax.broadcasted_iota(jnp.int32, (tm, N_EXPERTS), 1).astype(jnp.float32)
    member = jnp.zeros((tm, N_EXPERTS), jnp.float32)
    for _ in range(TOP_K):
        best = jnp.max(pick, axis=1, keepdims=True)
        which = jnp.min(jnp.where(pick == best, lane, float(N_EXPERTS)), axis=1, keepdims=True)
        hit = lane == which
        pick = jnp.where(hit, -jnp.inf, pick)
        member = jnp.where(hit, 1.0, member)
    chosen = member > 0.0
    picked = jnp.where(chosen, scores, 0.0)
    gate_ref[...] = picked / jnp.sum(picked, axis=1, keepdims=True) * ROUTE_SCALE

    r_i = lax.broadcasted_iota(jnp.int32, (tm, tm), 0)
    c_i = lax.broadcasted_iota(jnp.int32, (tm, tm), 1)
    tri = jnp.where(c_i < r_i, 1.0, 0.0).astype(jnp.bfloat16)
    ahead = _dot(tri, member.astype(jnp.bfloat16))
    cnt = jnp.sum(member, axis=0, keepdims=True)
    start, _ = _segment_bounds(cnt)
    slot_ref[...] = jnp.where(chosen, start + ahead + 1.0, 0.0)
    cnt_ref[...] = jnp.broadcast_to(cnt, cnt_ref.shape)


def _router(x2, router_w, router_b):
    n = x2.shape[0]
    tm = min(ROW_TILE, n)
    tile = lambda c: pl.BlockSpec((tm, c), lambda i: (i, 0))
    whole = lambda a: pl.BlockSpec(a.shape, lambda i: (0, 0))
    rb = router_b.reshape(1, N_EXPERTS)
    return pl.pallas_call(
        _router_body, grid=(n // tm,),
        in_specs=[tile(D_MODEL), whole(router_w), whole(rb)],
        out_specs=[tile(N_EXPERTS), tile(N_EXPERTS), pl.BlockSpec((8, N_EXPERTS), lambda i: (i, 0))],
        out_shape=[jax.ShapeDtypeStruct((n, N_EXPERTS), jnp.float32), jax.ShapeDtypeStruct((n, N_EXPERTS), jnp.float32),
                   jax.ShapeDtypeStruct((n // tm * 8, N_EXPERTS), jnp.float32)],
        compiler_params=_params("arbitrary"), name="moe_router")(x2, router_w, rb)


def _chunk_copy(src, dst, sem):
    return pltpu.make_async_copy(src, dst, sem)


def _chunk(ref, c):
    return ref.at[pl.ds(pl.multiple_of(c * CHUNK, CHUNK), CHUNK), :]


CHUNK_UNROLL = 4


def _chunk_groups(n_chunks):
    return lax.shift_right_logical(n_chunks, CHUNK_UNROLL.bit_length() - 1)


def _expert_of_row(cnt):
    start, end = _segment_bounds(cnt)
    j = lax.broadcasted_iota(jnp.int32, (SORT_ROWS, N_EXPERTS), 0).astype(jnp.float32)
    return jnp.where((j >= start) & (j < end), 1.0, 0.0).astype(jnp.bfloat16)


def _spread(per_expert, owner):
    return _dot_nt(per_expert.astype(jnp.bfloat16), owner)


def _row_matches(slot, owner):
    hi = jnp.floor(slot * (1.0 / SLOT_RADIX))
    lo = slot - hi * SLOT_RADIX
    place = _spread(hi, owner) * SLOT_RADIX + _spread(lo, owner)
    j1 = lax.broadcasted_iota(jnp.int32, place.shape, 1).astype(jnp.float32) + 1.0
    return place == j1


def _dispatch_body(nch_ref, dst_ref, x_ref, slot_ref, cnt_ref, xs_in_ref, xs_ref, sbuf, sem):
    del xs_in_ref
    at = _row_matches(slot_ref[...], _expert_of_row(cnt_ref[0:1, :]))
    perm_t = jnp.where(at, 1.0, 0.0).astype(jnp.bfloat16)
    rows = lax.dot_general(perm_t, x_ref[...].astype(jnp.bfloat16),
                           (((0,), (0,)), ((), ())), preferred_element_type=jnp.float32)
    sbuf[...] = _pack_rows(rows)
    groups = _chunk_groups(nch_ref[pl.program_id(0)])

    def start(g, carry):
        for u in range(CHUNK_UNROLL):
            c = g * CHUNK_UNROLL + u
            _chunk_copy(_chunk(sbuf, c), _chunk(xs_ref, dst_ref[c]), sem).start()
        return carry

    def wait(g, carry):
        for u in range(CHUNK_UNROLL):
            _chunk_copy(_chunk(sbuf, g * CHUNK_UNROLL + u), _chunk(xs_ref, 0), sem).wait()
        return carry

    lax.fori_loop(0, groups, start, 0)
    lax.fori_loop(0, groups, wait, 0)


def _dispatch(n_chunks, dst_flat, x2, slot, cnt, n_rows):
    n = x2.shape[0]
    tm = min(ROW_TILE, n)
    xs0 = jnp.zeros((n_rows, HALF), jnp.uint32)
    grid_spec = pltpu.PrefetchScalarGridSpec(
        num_scalar_prefetch=1, grid=(n // tm,),
        in_specs=[pl.BlockSpec((CHUNK_SLOTS,), lambda i, nc: (i,), memory_space=pltpu.SMEM),
                  pl.BlockSpec((tm, D_MODEL), lambda i, nc: (i, 0)),
                  pl.BlockSpec((tm, N_EXPERTS), lambda i, nc: (i, 0)),
                  pl.BlockSpec((8, N_EXPERTS), lambda i, nc: (i, 0)),
                  pl.BlockSpec(memory_space=pl.ANY)],
        out_specs=pl.BlockSpec(memory_space=pl.ANY),
        scratch_shapes=[pltpu.VMEM((SORT_ROWS, HALF), jnp.uint32), pltpu.SemaphoreType.DMA(())])
    return pl.pallas_call(
        _dispatch_body, grid_spec=grid_spec,
        out_shape=jax.ShapeDtypeStruct((n_rows, HALF), jnp.uint32),
        input_output_aliases={5: 0},
        compiler_params=_params("arbitrary"), name="moe_dispatch")(n_chunks, dst_flat, x2, slot, cnt, xs0)


def _expert_body(blk_exp_ref, n_used_ref, x_ref, wg_ref, wu_ref, wd_ref, o_ref, wgu_s, wd_s):
    i = pl.program_id(0)
    prev = blk_exp_ref[jnp.maximum(i - 1, 0)]

    @pl.when((i == 0) | (blk_exp_ref[i] != prev))
    def _():
        wgu_s[:, :D_EXPERT] = wg_ref[...].astype(_MM)
        wgu_s[:, D_EXPERT:] = wu_ref[...].astype(_MM)
        wd_s[...] = wd_ref[...].astype(_MM)

    @pl.when(i < n_used_ref[0])
    def _():
        gu = _dot(_unpack_rows(x_ref[...]).astype(_MM), wgu_s[...])
        h = jax.nn.silu(gu[:, :D_EXPERT]) * gu[:, D_EXPERT:]
        o_ref[...] = _pack_rows(_dot(h.astype(_MM), wd_s[...]))

    @pl.when(i >= n_used_ref[0])
    def _():
        o_ref[...] = jnp.zeros_like(o_ref)


def _experts(xs, blk_exp, n_used, w_gate, w_up, w_down):
    n_rows = xs.shape[0]
    n_blk = n_rows // EXPERT_BLOCK
    grid_spec = pltpu.PrefetchScalarGridSpec(
        num_scalar_prefetch=2, grid=(n_blk,),
        in_specs=[pl.BlockSpec((EXPERT_BLOCK, HALF), lambda i, be, nu: (i, 0)),
                  pl.BlockSpec((None, D_MODEL, D_EXPERT), lambda i, be, nu: (be[i], 0, 0)),
                  pl.BlockSpec((None, D_MODEL, D_EXPERT), lambda i, be, nu: (be[i], 0, 0)),
                  pl.BlockSpec((None, D_EXPERT, D_MODEL), lambda i, be, nu: (be[i], 0, 0))],
        out_specs=pl.BlockSpec((EXPERT_BLOCK, HALF), lambda i, be, nu: (i, 0)),
        scratch_shapes=[pltpu.VMEM((D_MODEL, 2 * D_EXPERT), _MM), pltpu.VMEM((D_EXPERT, D_MODEL), _MM)])
    return pl.pallas_call(_expert_body, grid_spec=grid_spec,
                          out_shape=jax.ShapeDtypeStruct((n_rows, HALF), jnp.uint32),
                          compiler_params=_params("arbitrary"), name="moe_experts")(
                              blk_exp, n_used, xs, w_gate, w_up, w_down)


def _combine_body(nch_ref, dcur_ref, dnext_ref, x_ref, slot_ref, cnt_ref, gate_ref, ys_ref, sgu_ref, sd_ref,
                  g_ref, b_ref, o_ref, ybuf, sem):
    i = pl.program_id(0)
    slot = i % 2

    @pl.when(i == 0)
    def _():
        ybuf[...] = jnp.zeros_like(ybuf)

    def gather(d_ref, n, s):
        def body(g, carry):
            for u in range(CHUNK_UNROLL):
                c = g * CHUNK_UNROLL + u
                _chunk_copy(_chunk(ys_ref, d_ref[c]), _chunk(ybuf.at[s], c), sem.at[s]).start()
            return carry
        lax.fori_loop(0, _chunk_groups(n), body, 0)

    @pl.when(i == 0)
    def _():
        gather(dcur_ref, nch_ref[0], 0)

    @pl.when(i + 1 < pl.num_programs(0))
    def _():
        gather(dnext_ref, nch_ref[jnp.minimum(i + 1, pl.num_programs(0) - 1)], 1 - slot)

    def wait(g, carry):
        for u in range(CHUNK_UNROLL):
            _chunk_copy(_chunk(ys_ref, 0), _chunk(ybuf.at[slot], g * CHUNK_UNROLL + u), sem.at[slot]).wait()
        return carry

    lax.fori_loop(0, _chunk_groups(nch_ref[i]), wait, 0)

    x = x_ref[...]
    owner = _expert_of_row(cnt_ref[0:1, :])
    at = _row_matches(slot_ref[...], owner)
    gate = gate_ref[...]
    g_hi = gate.astype(jnp.bfloat16).astype(jnp.float32)
    g_lo = gate - g_hi
    p_hi = jnp.where(at, _spread(g_hi, owner), 0.0).astype(jnp.bfloat16)
    p_lo = jnp.where(at, _spread(g_lo, owner), 0.0).astype(jnp.bfloat16)
    y = _unpack_rows(ybuf[slot]).astype(jnp.bfloat16)
    routed = _dot(p_hi, y) + _dot(p_lo, y)
    gu = _dot(x.astype(_MM), sgu_ref[...])
    hs = jax.nn.silu(gu[:, :D_EXPERT]) * gu[:, D_EXPERT:]
    shared = _dot(hs.astype(_MM), sd_ref[...])
    o_ref[...] = _layer_norm(ALPHA * x + (routed + shared), g_ref[...], b_ref[...])


def _combine(n_chunks, dst_flat, x2, slot, cnt, gate, ys, s_gu, s_down, ln_g, ln_b):
    n = x2.shape[0]
    tm = min(ROW_TILE, n)
    nt = n // tm
    consts = [s_gu, s_down, ln_g.reshape(1, D_MODEL), ln_b.reshape(1, D_MODEL)]
    grid_spec = pltpu.PrefetchScalarGridSpec(
        num_scalar_prefetch=1, grid=(nt,),
        in_specs=[pl.BlockSpec((CHUNK_SLOTS,), lambda i, nc: (i,), memory_space=pltpu.SMEM),
                  pl.BlockSpec((CHUNK_SLOTS,), lambda i, nc: (jnp.minimum(i + 1, nt - 1),), memory_space=pltpu.SMEM),
                  pl.BlockSpec((tm, D_MODEL), lambda i, nc: (i, 0)),
                  pl.BlockSpec((tm, N_EXPERTS), lambda i, nc: (i, 0)),
                  pl.BlockSpec((8, N_EXPERTS), lambda i, nc: (i, 0)),
                  pl.BlockSpec((tm, N_EXPERTS), lambda i, nc: (i, 0)),
                  pl.BlockSpec(memory_space=pl.ANY)]
        + [pl.BlockSpec(a.shape, lambda i, nc: (0, 0)) for a in consts],
        out_specs=pl.BlockSpec((tm, D_MODEL), lambda i, nc: (i, 0)),
        scratch_shapes=[pltpu.VMEM((2, SORT_ROWS, HALF), jnp.uint32), pltpu.SemaphoreType.DMA((2,))])
    return pl.pallas_call(
        _combine_body, grid_spec=grid_spec,
        out_shape=jax.ShapeDtypeStruct((n, D_MODEL), jnp.float32),
        compiler_params=_params("arbitrary"), name="moe_combine")(
            n_chunks, dst_flat, dst_flat, x2, slot, cnt, gate, ys, *consts)


def _moe_ln(x2, router_w, router_b, w_gate, w_up, w_down, s_gate, s_up, s_down, ln_g, ln_b):
    n = x2.shape[0]
    tm = min(ROW_TILE, n)
    nt = n // tm
    gate, slot, cnt8 = _router(x2, router_w, router_b)
    cnt = cnt8[::8].astype(jnp.int32)
    seg = (cnt + CHUNK - 1) // CHUNK * CHUNK
    loc_end = jnp.cumsum(seg, axis=1)
    loc_start = loc_end - seg
    exp_rows = jnp.sum(seg, axis=0)
    padded = (exp_rows + EXPERT_BLOCK - 1) // EXPERT_BLOCK * EXPERT_BLOCK
    pad_end = jnp.cumsum(padded)
    seg_dst = (pad_end - padded)[None, :] + jnp.cumsum(seg, axis=0) - seg
    max_rows = n * TOP_K + nt * N_EXPERTS * (CHUNK - 1)
    n_blk = -(-max_rows // EXPERT_BLOCK) + N_EXPERTS + 1
    n_real = (loc_end[:, -1] // CHUNK).astype(jnp.int32)
    n_chunks = (n_real + CHUNK_UNROLL - 1) // CHUNK_UNROLL * CHUNK_UNROLL
    c_idx = jnp.arange(SORT_ROWS // CHUNK, dtype=jnp.int32)
    c_row = c_idx[None, :, None] * CHUNK
    in_seg = (loc_start[:, None, :] <= c_row) & (c_row < loc_end[:, None, :])
    shift = jnp.sum(jnp.where(in_seg, (seg_dst - loc_start)[:, None, :], 0), axis=2)
    dst = (shift + c_idx[None, :] * CHUNK) // CHUNK
    spare = (n_blk - 1) * (EXPERT_BLOCK // CHUNK)
    dst = jnp.where(c_idx[None, :] < n_real[:, None], dst, spare + c_idx[None, :] - n_real[:, None])
    dst_flat = jnp.pad(dst, ((0, 0), (0, CHUNK_SLOTS - dst.shape[1]))).reshape(nt * CHUNK_SLOTS).astype(jnp.int32)
    blk_row = jnp.arange(n_blk, dtype=jnp.int32) * EXPERT_BLOCK
    blk_exp = jnp.minimum(jnp.sum((pad_end[None, :] <= blk_row[:, None]).astype(jnp.int32), axis=1), N_EXPERTS - 1)
    n_used = (pad_end[-1:] // EXPERT_BLOCK).astype(jnp.int32)
    xs = _dispatch(n_chunks, dst_flat, x2, slot, cnt8, n_blk * EXPERT_BLOCK)
    ys = _experts(xs, blk_exp, n_used, w_gate, w_up, w_down)
    s_gu = jnp.concatenate([s_gate, s_up], axis=-1).astype(_MM)
    return _combine(n_chunks, dst_flat, x2, slot, cnt8, gate, ys, s_gu, s_down.astype(_MM), ln_g, ln_b)


def kernel(x, ev_w_in, ev_b_in, s5_lam_re, s5_lam_im, s5_log_dt, s5_b_re, s5_b_im, s5_c_re, s5_c_im, s5_d, s5_w_glu, s5_b_glu, cv_w, cv_b, cv_ln_g, cv_ln_b, ev_w_out, od_w_in, cmp_w_k, cmp_w_v, cmp_pe_k, cmp_pe_v, od_w_out, ln1_g, ln1_b, ln2_g, ln2_b, router_w, router_b, ex_w_gate, ex_w_up, ex_w_down, sh_w_gate, sh_w_up, sh_w_down):
    bsz, seq, _ = x.shape
    h = x.reshape(bsz * seq, D_MODEL)
    for layer in range(DEPTH):
        i = layer // 2
        if layer % 2 == 0:
            mix = _even_mixer(h, bsz, seq, ev_w_in[i], ev_b_in[i], s5_lam_re[i], s5_lam_im[i], s5_log_dt[i],
                              s5_b_re[i], s5_b_im[i], s5_c_re[i], s5_c_im[i], s5_d[i].reshape(-1),
                              s5_w_glu[i], s5_b_glu[i], cv_w[i], cv_b[i], cv_ln_g[i], cv_ln_b[i])
            (h,) = _row_call(_proj_res_ln_body, [mix, h],
                             [ev_w_out[i].astype(_MM), ln1_g[layer].reshape(1, D_MODEL),
                              ln1_b[layer].reshape(1, D_MODEL)],
                             [(D_MODEL, jnp.float32)], "out_proj_ln")
        else:
            mix_t = _nsa_mixer(h, bsz, seq, od_w_in[i], cmp_w_k[i], cmp_w_v[i], cmp_pe_k[i], cmp_pe_v[i])
            h = _proj_t_res_ln(mix_t, h, od_w_out[i], ln1_g[layer], ln1_b[layer])
        h = _moe_ln(h, router_w[layer], router_b[layer], ex_w_gate[layer], ex_w_up[layer], ex_w_down[layer],
                    sh_w_gate[layer], sh_w_up[layer], sh_w_down[layer], ln2_g[layer], ln2_b[layer])
    return h.reshape(bsz, seq, D_MODEL)
```

```python
import functools
import math

import numpy as np
import jax
import jax.numpy as jnp
from jax import lax
from jax.experimental import pallas as pl
from jax.experimental.pallas import tpu as pltpu

D_MODEL = 1024
DEPTH = 4
ALPHA = (2.0 * DEPTH) ** 0.25
LN_EPS = 1e-5
NEG_INF = -1e30

D_S5 = 512
S5_GROUP = 16
S5_GROUPS = 32
S5_STATE = 64
S5_LANES = S5_GROUPS * S5_STATE
D_CONV = 512
CONV_WIDTH = 31
EVEN_IN = D_S5 + 2 * D_CONV

HEAD_DIM = 64
N_HEADS = 16
N_KV_HEADS = 4
GQA = 4
CMP_LEN = 32
CMP_STRIDE = 16
SEL_LEN = 64
N_SELECT = 16
WINDOW = 512
FORCE_BONUS = 1e4
ROPE_THETA = 10000.0
Q_WIDTH = N_HEADS * HEAD_DIM
KV_WIDTH = N_KV_HEADS * HEAD_DIM
GATE_WIDTH = 3 * N_HEADS

N_EXPERTS = 64
TOP_K = 8
D_EXPERT = 256
ROUTE_SCALE = 2.5

LANE = 128
_MM = jnp.bfloat16
_VMEM_LIMIT = 56 * 1024 * 1024

ROW_TILE = 256
PROJ_TILE = 512
SCAN_ROWS = 128
SEQ_TILE = 256
CONV_HALO = 32
Q_TILE = 256
KEY_CHUNK = 1024
EXPERT_BLOCK = 1024


def _dot(a, b):
    return jnp.dot(a, b, preferred_element_type=jnp.float32)


def _dot_nt(a, b):
    return lax.dot_general(a, b, (((1,), (1,)), ((), ())), preferred_element_type=jnp.float32)


def _layer_norm(x, g, b):
    mu = jnp.mean(x, axis=-1, keepdims=True)
    xc = x - mu
    var = jnp.mean(xc * xc, axis=-1, keepdims=True)
    return xc * lax.rsqrt(var + LN_EPS) * g + b


HALF = D_MODEL // 2


def _pack_rows(x):
    bits = lambda v: lax.bitcast_convert_type(v.astype(jnp.bfloat16).astype(jnp.float32), jnp.uint32)
    return (bits(x[:, HALF:]) & jnp.uint32(0xFFFF0000)) | (bits(x[:, :HALF]) >> 16)


def _unpack_rows(w):
    lo = lax.bitcast_convert_type(w << 16, jnp.float32)
    hi = lax.bitcast_convert_type(w & jnp.uint32(0xFFFF0000), jnp.float32)
    return jnp.concatenate([lo, hi], axis=1)


def _params(*sem):
    return pltpu.CompilerParams(dimension_semantics=sem, vmem_limit_bytes=_VMEM_LIMIT)


def _row_call(body, row_ins, const_ins, outs, name, tm=PROJ_TILE):
    m = row_ins[0].shape[0]
    tm = min(tm, m)
    assert m % tm == 0
    in_specs = [pl.BlockSpec((tm, a.shape[1]), lambda i: (i, 0)) for a in row_ins]
    in_specs += [pl.BlockSpec(a.shape, functools.partial(lambda nd, i: (0,) * nd, a.ndim)) for a in const_ins]
    out_specs = [pl.BlockSpec((tm, c), lambda i: (i, 0)) for c, _ in outs]
    out_shape = [jax.ShapeDtypeStruct((m, c), dt) for c, dt in outs]
    return pl.pallas_call(body, grid=(m // tm,), in_specs=in_specs, out_specs=out_specs,
                          out_shape=out_shape, compiler_params=_params("arbitrary"), name=name)(
                              *row_ins, *const_ins)


def _proj_bias_body(x_ref, w_ref, b_ref, o_ref):
    o_ref[...] = _dot(x_ref[...].astype(_MM), w_ref[...]) + b_ref[...]


def _proj_res_ln_body(y_ref, res_ref, w_ref, g_ref, b_ref, o_ref):
    mix = _dot(y_ref[...].astype(_MM), w_ref[...])
    o_ref[...] = _layer_norm(ALPHA * res_ref[...] + mix, g_ref[...], b_ref[...])


def _s5_tables(lam_re, lam_im, log_dt, b_re, b_im, c_re, c_im):
    f32 = jnp.float32
    dt = jnp.exp(log_dt.astype(f32))[:, None]
    decay = jnp.exp(lam_re * dt)
    a_re, a_im = decay * jnp.cos(lam_im * dt), decay * jnp.sin(lam_im * dt)
    den = lam_re ** 2 + lam_im ** 2
    f_re = ((a_re - 1.0) * lam_re + a_im * lam_im) / den
    f_im = (a_im * lam_re - (a_re - 1.0) * lam_im) / den
    bb_re = f_re[..., None] * b_re - f_im[..., None] * b_im
    bb_im = f_re[..., None] * b_im + f_im[..., None] * b_re

    gl = LANE // S5_GROUP
    nj = S5_GROUPS // gl
    eye = jnp.eye(gl, dtype=f32)

    def in_blocks(bb):
        t = bb.reshape(nj, gl, S5_STATE, S5_GROUP)
        t = jnp.einsum('jgph,gk->jghkp', t, eye)
        return t.reshape(nj, gl * S5_GROUP, gl * S5_STATE).astype(_MM)

    def out_blocks(c):
        t = c.reshape(nj, gl, S5_GROUP, S5_STATE)
        t = jnp.einsum('jghp,gk->jgpkh', t, eye)
        return t.reshape(nj, gl * S5_STATE, gl * S5_GROUP).astype(_MM)

    ar, ai = a_re.reshape(1, S5_LANES), a_im.reshape(1, S5_LANES)
    pows_r, pows_i = [ar], [ai]
    for _ in range(int(math.log2(SCAN_ROWS)) - 1):
        pr, pi = pows_r[-1], pows_i[-1]
        pows_r.append(pr * pr - pi * pi)
        pows_i.append(2.0 * pr * pi)
    tr, ti = ar, ai
    for k in range(int(math.log2(SCAN_ROWS))):
        pr, pi = pows_r[k], pows_i[k]
        tr, ti = (jnp.concatenate([tr, tr * pr - ti * pi], axis=0),
                  jnp.concatenate([ti, tr * pi + ti * pr], axis=0))
    nlb = S5_LANES // LANE
    to3 = lambda t: t.reshape(t.shape[0], nlb, LANE).transpose(1, 0, 2)
    return (in_blocks(bb_re), in_blocks(bb_im), out_blocks(c_re), out_blocks(c_im),
            to3(jnp.concatenate(pows_r, axis=0)), to3(jnp.concatenate(pows_i, axis=0)), to3(tr), to3(ti))


def _s5conv_body(h_ref, bre_ref, bim_ref, cre_ref, cim_ref, pwr_ref, pwi_ref, tbr_ref, tbi_ref,
                 d_ref, wglu_ref, bglu_ref, cvw_ref, cvb_ref, lng_ref, lnb_ref, o_ref,
                 st_re, st_im, xr_ref, xi_ref, hbuf):
    tile = h_ref.shape[0]
    n_steps = int(math.log2(SCAN_ROWS))
    nlb = S5_LANES // LANE
    nj = bre_ref.shape[0]
    per_j = nlb // nj

    @pl.when(pl.program_id(1) == 0)
    def _():
        st_re[...] = jnp.zeros_like(st_re)
        st_im[...] = jnp.zeros_like(st_im)
        hbuf[0:CONV_HALO, :] = jnp.zeros((CONV_HALO, D_CONV), jnp.float32)

    row = lax.broadcasted_iota(jnp.int32, (SCAN_ROWS, LANE), 0)

    for c in range(tile // SCAN_ROWS):
        r0 = c * SCAN_ROWS
        u = h_ref[r0:r0 + SCAN_ROWS, 0:D_S5]
        ub = u.astype(_MM)
        for j in range(nj):
            uj = ub[:, j * LANE:(j + 1) * LANE]
            br = _dot(uj, bre_ref[j])
            bi = _dot(uj, bim_ref[j])
            for q in range(per_j):
                xr_ref[j * per_j + q] = br[:, q * LANE:(q + 1) * LANE]
                xi_ref[j * per_j + q] = bi[:, q * LANE:(q + 1) * LANE]

        def scan_block(lb, carry):
            xr, xi = xr_ref[lb], xi_ref[lb]
            for k in range(n_steps):
                d = 1 << k
                ar = pwr_ref[lb, k:k + 1, :]
                ai = pwi_ref[lb, k:k + 1, :]
                keep = row >= d
                sr = jnp.where(keep, pltpu.roll(xr, d, 0), 0.0)
                si = jnp.where(keep, pltpu.roll(xi, d, 0), 0.0)
                xr, xi = xr + ar * sr - ai * si, xi + ar * si + ai * sr
            pr, pi = st_re[lb, 0:1, :], st_im[lb, 0:1, :]
            tr, ti = tbr_ref[lb], tbi_ref[lb]
            xr, xi = xr + tr * pr - ti * pi, xi + tr * pi + ti * pr
            xr_ref[lb] = xr
            xi_ref[lb] = xi
            st_re[lb, 0:1, :] = xr[SCAN_ROWS - 1:SCAN_ROWS, :]
            st_im[lb, 0:1, :] = xi[SCAN_ROWS - 1:SCAN_ROWS, :]
            return carry

        lax.fori_loop(0, nlb, scan_block, 0)

        ys = []
        for j in range(nj):
            xr = jnp.concatenate([xr_ref[j * per_j + q] for q in range(per_j)], axis=1).astype(_MM)
            xi = jnp.concatenate([xi_ref[j * per_j + q] for q in range(per_j)], axis=1).astype(_MM)
            ys.append(_dot(xr, cre_ref[j]) - _dot(xi, cim_ref[j]))
        y = jnp.concatenate(ys, axis=1) + d_ref[...] * u
        y = jax.nn.gelu(y)
        y = y * jax.nn.sigmoid(_dot(y.astype(_MM), wglu_ref[...]) + bglu_ref[...])
        o_ref[r0:r0 + SCAN_ROWS, 0:D_S5] = y.astype(o_ref.dtype)

    val = h_ref[:, D_S5:D_S5 + D_CONV]
    gate = h_ref[:, D_S5 + D_CONV:D_S5 + 2 * D_CONV]
    hbuf[CONV_HALO:CONV_HALO + tile, :] = val * jax.nn.sigmoid(gate)
    off = CONV_HALO - (CONV_WIDTH - 1)
    acc = jnp.zeros((tile, D_CONV), jnp.float32)
    for k in range(CONV_WIDTH):
        acc = acc + cvw_ref[k:k + 1, :] * hbuf[off + k:off + k + tile, :]
    acc = acc + cvb_ref[...]
    yb = _layer_norm(acc, lng_ref[...], lnb_ref[...])
    o_ref[:, D_S5:D_S5 + D_CONV] = (yb * jax.nn.sigmoid(yb)).astype(o_ref.dtype)
    hbuf[0:CONV_HALO, :] = hbuf[tile:tile + CONV_HALO, :]


def _s5conv(h, tables, d_skip, w_glu, b_glu, cv_w, cv_b, ln_g, ln_b):
    bsz, seq, _ = h.shape
    tile = min(SEQ_TILE, seq)
    assert seq % tile == 0 and tile % SCAN_ROWS == 0
    bre, bim, cre, cim, pwr, pwi, tbr, tbi = tables
    consts = [bre, bim, cre, cim, pwr, pwi, tbr, tbi,
              d_skip.reshape(1, D_S5), w_glu.astype(_MM), b_glu.reshape(1, D_S5),
              cv_w, cv_b.reshape(1, D_CONV), ln_g.reshape(1, D_CONV), ln_b.reshape(1, D_CONV)]
    nlb = S5_LANES // LANE
    in_specs = [pl.BlockSpec((None, tile, EVEN_IN), lambda b, l: (b, l, 0))]
    in_specs += [pl.BlockSpec(a.shape, functools.partial(lambda nd, b, l: (0,) * nd, a.ndim)) for a in consts]
    return pl.pallas_call(
        _s5conv_body, grid=(bsz, seq // tile), in_specs=in_specs,
        out_specs=pl.BlockSpec((None, tile, D_S5 + D_CONV), lambda b, l: (b, l, 0)),
        out_shape=jax.ShapeDtypeStruct((bsz, seq, D_S5 + D_CONV), _MM),
        scratch_shapes=[pltpu.VMEM((nlb, 8, LANE), jnp.float32), pltpu.VMEM((nlb, 8, LANE), jnp.float32),
                        pltpu.VMEM((nlb, SCAN_ROWS, LANE), jnp.float32),
                        pltpu.VMEM((nlb, SCAN_ROWS, LANE), jnp.float32),
                        pltpu.VMEM((tile + CONV_HALO, D_CONV), jnp.float32)],
        compiler_params=_params("arbitrary", "arbitrary"), name="s5conv")(h, *consts)


def _even_mixer(x2, bsz, seq, w_in, b_in, lam_re, lam_im, log_dt, b_re, b_im, c_re, c_im, d_skip,
                w_glu, b_glu, cv_w, cv_b, cv_ln_g, cv_ln_b):
    (h,) = _row_call(_proj_bias_body, [x2], [w_in.astype(_MM), b_in.reshape(1, EVEN_IN)],
                     [(EVEN_IN, jnp.float32)], "even_in_proj")
    tables = _s5_tables(lam_re, lam_im, log_dt, b_re, b_im, c_re, c_im)
    y = _s5conv(h.reshape(bsz, seq, EVEN_IN), tables, d_skip, w_glu, b_glu, cv_w, cv_b, cv_ln_g, cv_ln_b)
    return y.reshape(bsz * seq, D_S5 + D_CONV)


ROPE_W = Q_WIDTH + 3 * KV_WIDTH
TOK_W = ROPE_W + KV_WIDTH
GATE_ROWS = 16
LOG2E = 1.4426950408889634


def _nsa_proj_body(x_ref, w_ref, wt_ref, cos_ref, sin_ref, q_ref, k_ref, kc_ref, vc_ref, vt_ref, gt_ref):
    xb = x_ref[...].astype(_MM)
    y = _dot(xb, w_ref[...])
    r = y[:, :ROPE_W]
    reps = ROPE_W // LANE
    cos = jnp.concatenate([cos_ref[...]] * reps, axis=1)
    sin = jnp.concatenate([sin_ref[...]] * reps, axis=1)
    lane = lax.broadcasted_iota(jnp.int32, r.shape, 1)
    half = HEAD_DIM // 2
    first = (lane % HEAD_DIM) < half
    rot = jnp.where(first, -pltpu.roll(r, ROPE_W - half, 1), pltpu.roll(r, half, 1))
    r = r * cos + rot * sin
    qs = r[:, :Q_WIDTH] * (HEAD_DIM ** -0.5 * LOG2E)
    for hd in range(N_HEADS):
        q_ref[hd] = qs[:, hd * HEAD_DIM:(hd + 1) * HEAD_DIM].astype(q_ref.dtype)
    for j in range(2 * N_KV_HEADS):
        c0 = Q_WIDTH + j * HEAD_DIM
        k_ref[j] = r[:, c0:c0 + HEAD_DIM].astype(k_ref.dtype)
    for j in range(N_KV_HEADS):
        c0 = Q_WIDTH + 2 * KV_WIDTH + j * HEAD_DIM
        kc_ref[j] = r[:, c0:c0 + HEAD_DIM]
        vc_ref[j] = y[:, ROPE_W + j * HEAD_DIM:ROPE_W + (j + 1) * HEAD_DIM]
    yt = _dot_nt(wt_ref[...], xb)
    vt_ref[...] = yt[:2 * KV_WIDTH, :].astype(vt_ref.dtype)
    gt_ref[...] = yt[2 * KV_WIDTH:, :]


def _nsa_proj(x2, bsz, seq, w_in):
    tm = min(ROW_TILE, seq)
    nt = seq // tm
    sizes = [Q_WIDTH] + [KV_WIDTH] * 6 + [GATE_WIDTH]
    offs = np.cumsum([0] + sizes)
    cols = lambda i: w_in[:, offs[i]:offs[i + 1]]
    w_tok = jnp.concatenate([cols(0), cols(3), cols(5), cols(1), cols(2)], axis=1).astype(_MM)
    gcols = cols(7).reshape(D_MODEL, N_KV_HEADS, GQA, 3).transpose(0, 1, 3, 2).reshape(D_MODEL, N_KV_HEADS, 3 * GQA)
    gcols = jnp.pad(gcols, ((0, 0), (0, 0), (0, GATE_ROWS - 3 * GQA))).reshape(D_MODEL, N_KV_HEADS * GATE_ROWS)
    w_t = jnp.concatenate([cols(4), cols(6), gcols], axis=1).T.astype(_MM)
    half = HEAD_DIM // 2
    inv = ROPE_THETA ** (-jnp.arange(half, dtype=jnp.float32) / half)
    ang = jnp.arange(seq, dtype=jnp.float32)[:, None] * inv[None, :]
    cos = jnp.tile(jnp.cos(ang), (1, LANE // half))
    sin = jnp.tile(jnp.sin(ang), (1, LANE // half))
    heads = lambda nh: pl.BlockSpec((None, nh, tm, HEAD_DIM), lambda b, i: (b, 0, i, 0))
    rows_t = lambda nr: pl.BlockSpec((None, nr, tm), lambda b, i: (b, 0, i))
    n_gate = N_KV_HEADS * GATE_ROWS
    return pl.pallas_call(
        _nsa_proj_body, grid=(bsz, nt),
        in_specs=[pl.BlockSpec((tm, D_MODEL), lambda b, i: (b * nt + i, 0)),
                  pl.BlockSpec(w_tok.shape, lambda b, i: (0, 0)),
                  pl.BlockSpec(w_t.shape, lambda b, i: (0, 0)),
                  pl.BlockSpec((tm, LANE), lambda b, i: (i, 0)),
                  pl.BlockSpec((tm, LANE), lambda b, i: (i, 0))],
        out_specs=[heads(N_HEADS), heads(2 * N_KV_HEADS), heads(N_KV_HEADS), heads(N_KV_HEADS),
                   rows_t(2 * KV_WIDTH), rows_t(n_gate)],
        out_shape=[jax.ShapeDtypeStruct((bsz, N_HEADS, seq, HEAD_DIM), _MM),
                   jax.ShapeDtypeStruct((bsz, 2 * N_KV_HEADS, seq, HEAD_DIM), _MM),
                   jax.ShapeDtypeStruct((bsz, N_KV_HEADS, seq, HEAD_DIM), jnp.float32),
                   jax.ShapeDtypeStruct((bsz, N_KV_HEADS, seq, HEAD_DIM), jnp.float32),
                   jax.ShapeDtypeStruct((bsz, 2 * KV_WIDTH, seq), _MM),
                   jax.ShapeDtypeStruct((bsz, n_gate, seq), jnp.float32)],
        compiler_params=_params("arbitrary", "arbitrary"), name="nsa_in_proj")(x2, w_tok, w_t, cos, sin)


def _compress_body(k_ref, v_ref, pk_ref, pv_ref, wk_ref, wvt_ref, ko_ref, vot_ref):
    rows = k_ref.shape[0]
    xk = k_ref[...]
    lo = _dot((xk + pk_ref[0:1, :]).astype(_MM), wk_ref[0])
    hi = _dot((xk + pk_ref[1:2, :]).astype(_MM), wk_ref[1])
    ko_ref[...] = (lo + pltpu.roll(hi, rows - 1, 0)).astype(ko_ref.dtype)
    xv = v_ref[...]
    lo_t = _dot_nt(wvt_ref[0], (xv + pv_ref[0:1, :]).astype(_MM))
    hi_t = _dot_nt(wvt_ref[1], (xv + pv_ref[1:2, :]).astype(_MM))
    vot_ref[...] = (lo_t + pltpu.roll(hi_t, rows - 1, 1)).astype(vot_ref.dtype)


def _compress(kc, vc, cmp_w_k, cmp_w_v, cmp_pe_k, cmp_pe_v):
    bsz, kvh, seq, _ = kc.shape
    grp = seq // CMP_STRIDE
    flat = CMP_STRIDE * HEAD_DIM
    k2 = kc.reshape(bsz * kvh * grp, flat)
    v2 = vc.reshape(bsz * kvh * grp, flat)
    consts = [cmp_pe_k.reshape(2, flat), cmp_pe_v.reshape(2, flat),
              cmp_w_k.reshape(2, flat, HEAD_DIM).astype(_MM),
              cmp_w_v.reshape(2, flat, HEAD_DIM).transpose(0, 2, 1).astype(_MM)]
    ko, vot = pl.pallas_call(
        _compress_body, grid=(bsz * kvh,),
        in_specs=[pl.BlockSpec((grp, flat), lambda i: (i, 0)), pl.BlockSpec((grp, flat), lambda i: (i, 0))]
        + [pl.BlockSpec(a.shape, functools.partial(lambda nd, i: (0,) * nd, a.ndim)) for a in consts],
        out_specs=[pl.BlockSpec((grp, HEAD_DIM), lambda i: (i, 0)),
                   pl.BlockSpec((None, HEAD_DIM, grp), lambda i: (i, 0, 0))],
        out_shape=[jax.ShapeDtypeStruct((bsz * kvh * grp, HEAD_DIM), _MM),
                   jax.ShapeDtypeStruct((bsz * kvh, HEAD_DIM, grp), _MM)],
        compiler_params=_params("arbitrary"), name="nsa_compress")(k2, v2, *consts)
    return ko.reshape(bsz, kvh, grp, HEAD_DIM), vot.reshape(bsz, kvh, HEAD_DIM, grp)


def _nsa_attn_body(q_ref, g_ref, kc_ref, vct_ref, ks_ref, vst_ref, kw_ref, vwt_ref, ov_ref, wm_ref, o_ref, sel_ref):
    qi = pl.program_id(2)
    q0 = qi * Q_TILE
    lanes = GQA * Q_TILE
    n_grp = kc_ref.shape[0]
    n_sb = ov_ref.shape[0]

    q = q_ref[...].reshape(lanes, HEAD_DIM)
    t_lane = q0 + (lax.broadcasted_iota(jnp.int32, (1, lanes), 1) % Q_TILE)

    s = _dot_nt(kc_ref[...], q)
    n_idx = lax.broadcasted_iota(jnp.int32, (n_grp, lanes), 0)
    vis = (n_idx * CMP_STRIDE + (CMP_LEN - 1)) <= t_lane
    s = jnp.where(vis, s, NEG_INF)
    e = jnp.where(vis, jnp.exp2(s - jnp.max(s, axis=0, keepdims=True)), 0.0)
    den = jnp.sum(e, axis=0, keepdims=True)
    p_c = e / jnp.where(den > 0.0, den, 1.0)
    o_c = _dot(vct_ref[...], p_c.astype(_MM))

    psum = p_c[:, 0:Q_TILE]
    for g in range(1, GQA):
        psum = psum + p_c[:, g * Q_TILE:(g + 1) * Q_TILE]
    imp = jnp.dot(ov_ref[...], psum, preferred_element_type=jnp.float32,
                  precision=lax.Precision.HIGHEST)
    blk = lax.broadcasted_iota(jnp.int32, (n_sb, Q_TILE), 0)
    cur = t_lane[:, 0:Q_TILE] // SEL_LEN
    allowed = blk <= cur
    forced = (blk == 0) | (blk == cur) | (blk == cur - 1)
    score = jnp.where(allowed, imp + jnp.where(forced, FORCE_BONUS, 0.0), -1.0)
    sel_ref[0:n_sb, 0:Q_TILE] = score
    row_group = 8

    def rank_group(g, rank):
        rows = sel_ref[pl.ds(pl.multiple_of(g * row_group, row_group), row_group), 0:Q_TILE]
        for r in range(row_group):
            sj = rows[r:r + 1, :]
            ahead = (sj == score) & (blk > g * row_group + r)
            rank = rank + jnp.where(sj > score, 1.0, jnp.where(ahead, 1.0, 0.0))
        return rank

    n_groups = (q0 + Q_TILE - 1) // (SEL_LEN * row_group) + 1
    rank = lax.fori_loop(0, n_groups, rank_group, jnp.zeros((n_sb, Q_TILE), jnp.float32))
    chosen = jnp.where((rank < float(N_SELECT)) & (score >= 0.0), 1.0, 0.0)
    sel_ref[0:n_sb, :] = jnp.concatenate([chosen] * GQA, axis=1)

    kpos_blk = lax.broadcasted_iota(jnp.int32, (SEL_LEN, lanes), 0)

    per_chunk = KEY_CHUNK // SEL_LEN

    def sel_chunk(c, carry, causal):
        m_run, l_run, acc = carry
        k0 = pl.multiple_of(c * KEY_CHUNK, KEY_CHUNK)
        sc = _dot_nt(ks_ref[pl.ds(k0, KEY_CHUNK), :], q)
        parts = []
        sel_rows = sel_ref[pl.ds(pl.multiple_of(c * per_chunk, per_chunk), per_chunk), :]
        for j in range(per_chunk):
            ok = sel_rows[j:j + 1, :] > 0.5
            if causal:
                ok = ok & ((k0 + j * SEL_LEN + kpos_blk) <= t_lane)
            parts.append(jnp.where(ok, sc[j * SEL_LEN:(j + 1) * SEL_LEN, :], NEG_INF))
        sc = jnp.concatenate(parts, axis=0)
        m_new = jnp.maximum(m_run, jnp.max(sc, axis=0, keepdims=True))
        scale = jnp.exp2(m_run - m_new)
        p = jnp.exp2(sc - m_new)
        l_new = scale * l_run + jnp.sum(p, axis=0, keepdims=True)
        acc = scale * acc + _dot(vst_ref[:, pl.ds(k0, KEY_CHUNK)], p.astype(_MM))
        return m_new, l_new, acc

    n_full = q0 // KEY_CHUNK
    init = (jnp.full((1, lanes), NEG_INF, jnp.float32), jnp.zeros((1, lanes), jnp.float32),
            jnp.zeros((HEAD_DIM, lanes), jnp.float32))
    carry = lax.fori_loop(0, n_full, functools.partial(sel_chunk, causal=False), init)
    _, l_s, acc_s = sel_chunk(n_full, carry, causal=True)
    o_s = acc_s / l_s

    span = WINDOW + Q_TILE
    back = jnp.minimum(q0, WINDOW)
    w0 = pl.multiple_of(q0 - back, Q_TILE)
    m0 = pl.multiple_of(WINDOW - back, Q_TILE)
    sw = _dot_nt(kw_ref[pl.ds(w0, span), :], q) + wm_ref[pl.ds(m0, span), :]
    pw = jnp.exp2(sw - jnp.max(sw, axis=0, keepdims=True))
    o_w = _dot(vwt_ref[:, pl.ds(w0, span)], pw.astype(_MM)) / jnp.sum(pw, axis=0, keepdims=True)

    gs = jax.nn.sigmoid(g_ref[...])
    gate = lambda c: jnp.concatenate([gs[c * GQA + g:c * GQA + g + 1, :] for g in range(GQA)], axis=1)
    out = gate(0) * o_c + gate(1) * o_s + gate(2) * o_w
    for g in range(GQA):
        o_ref[g] = out[:, g * Q_TILE:(g + 1) * Q_TILE].astype(o_ref.dtype)


def _nsa_attention(q4, g_t, k_cmp, v_cmp_t, k8, v_t):
    bsz, _, seq, _ = q4.shape
    n_grp = k_cmp.shape[2]
    n_sb = seq // SEL_LEN
    lanes = GQA * Q_TILE
    assert seq % KEY_CHUNK == 0 and seq >= WINDOW + Q_TILE and n_sb % (KEY_CHUNK // SEL_LEN) == 0
    c_start = np.arange(n_grp) * CMP_STRIDE
    s_start = np.arange(n_sb) * SEL_LEN
    overlap = ((c_start[None, :] < s_start[:, None] + SEL_LEN)
               & (c_start[None, :] + CMP_LEN > s_start[:, None])
               & (np.arange(n_grp)[None, :] < n_grp - 1)).astype(np.float32)
    u = np.arange(2 * WINDOW + Q_TILE)[:, None]
    ql = (np.arange(lanes) % Q_TILE)[None, :]
    wmask = np.where((u > ql) & (u <= WINDOW + ql), 0.0, NEG_INF).astype(np.float32)
    nkv = N_KV_HEADS
    return pl.pallas_call(
        _nsa_attn_body, grid=(bsz, nkv, seq // Q_TILE),
        in_specs=[pl.BlockSpec((None, GQA, Q_TILE, HEAD_DIM), lambda b, h, i: (b, h, i, 0)),
                  pl.BlockSpec((None, GATE_ROWS, Q_TILE), lambda b, h, i: (b, h, i)),
                  pl.BlockSpec((None, None, n_grp, HEAD_DIM), lambda b, h, i: (b, h, 0, 0)),
                  pl.BlockSpec((None, None, HEAD_DIM, n_grp), lambda b, h, i: (b, h, 0, 0)),
                  pl.BlockSpec((None, None, seq, HEAD_DIM), lambda b, h, i: (b, h, 0, 0)),
                  pl.BlockSpec((None, HEAD_DIM, seq), lambda b, h, i: (b, h, 0)),
                  pl.BlockSpec((None, None, seq, HEAD_DIM), lambda b, h, i: (b, nkv + h, 0, 0)),
                  pl.BlockSpec((None, HEAD_DIM, seq), lambda b, h, i: (b, nkv + h, 0)),
                  pl.BlockSpec(overlap.shape, lambda b, h, i: (0, 0)),
                  pl.BlockSpec(wmask.shape, lambda b, h, i: (0, 0))],
        out_specs=pl.BlockSpec((None, GQA, HEAD_DIM, Q_TILE), lambda b, h, i: (b, h, 0, i)),
        out_shape=jax.ShapeDtypeStruct((bsz, N_HEADS, HEAD_DIM, seq), _MM),
        scratch_shapes=[pltpu.VMEM((n_sb, lanes), jnp.float32)],
        compiler_params=_params("arbitrary", "arbitrary", "arbitrary"), name="nsa_attention")(
            q4, g_t, k_cmp, v_cmp_t, k8, v_t, k8, v_t, jnp.asarray(overlap), jnp.asarray(wmask))


def _nsa_mixer(x2, bsz, seq, w_in, cmp_w_k, cmp_w_v, cmp_pe_k, cmp_pe_v):
    q4, k8, kc4, vc4, v_t, g_t = _nsa_proj(x2, bsz, seq, w_in)
    k_cmp, v_cmp_t = _compress(kc4, vc4, cmp_w_k, cmp_w_v, cmp_pe_k, cmp_pe_v)
    o = _nsa_attention(q4, g_t, k_cmp, v_cmp_t, k8, v_t)
    return o.reshape(bsz, Q_WIDTH, seq)


def _proj_t_res_ln_body(yt_ref, res_ref, w_ref, g_ref, b_ref, o_ref):
    mix = lax.dot_general(yt_ref[...], w_ref[...], (((0,), (0,)), ((), ())), preferred_element_type=jnp.float32)
    o_ref[...] = _layer_norm(ALPHA * res_ref[...] + mix, g_ref[...], b_ref[...])


def _proj_t_res_ln(y_t, res, w, g, b):
    bsz, _, seq = y_t.shape
    tm = min(PROJ_TILE, seq)
    nt = seq // tm
    consts = [w.astype(_MM), g.reshape(1, D_MODEL), b.reshape(1, D_MODEL)]
    row = lambda c: pl.BlockSpec((tm, c), lambda bi, i: (bi * nt + i, 0))
    return pl.pallas_call(
        _proj_t_res_ln_body, grid=(bsz, nt),
        in_specs=[pl.BlockSpec((None, Q_WIDTH, tm), lambda bi, i: (bi, 0, i)), row(D_MODEL)]
        + [pl.BlockSpec(a.shape, lambda bi, i: (0, 0)) for a in consts],
        out_specs=row(D_MODEL),
        out_shape=jax.ShapeDtypeStruct((bsz * seq, D_MODEL), jnp.float32),
        compiler_params=_params("arbitrary", "arbitrary"), name="out_proj_t_ln")(y_t, res, *consts)


CHUNK = 8
SORT_ROWS = ROW_TILE * TOP_K + N_EXPERTS * CHUNK
CHUNK_SLOTS = 1024


SLOT_RADIX = 64


def _segment_bounds(cnt):
    padded = (((cnt.astype(jnp.int32) + (CHUNK - 1)) // CHUNK) * CHUNK).astype(jnp.float32)
    e_r = lax.broadcasted_iota(jnp.int32, (N_EXPERTS, N_EXPERTS), 0)
    e_c = lax.broadcasted_iota(jnp.int32, (N_EXPERTS, N_EXPERTS), 1)
    upper = jnp.where(e_r < e_c, 1.0, 0.0).astype(jnp.bfloat16)
    start = _dot(jnp.broadcast_to(padded, (8, N_EXPERTS)).astype(jnp.bfloat16), upper)[0:1, :]
    return start, start + padded


def _router_body(x_ref, w_ref, b_ref, gate_ref, slot_ref, cnt_ref):
    logits = jnp.dot(x_ref[...], w_ref[...], preferred_element_type=jnp.float32,
                     precision=lax.Precision.HIGHEST)
    scores = jax.nn.sigmoid(logits)
    tm = scores.shape[0]
    pick = scores + b_ref[...]
    lane = lax.broadcasted_iota(jnp.int32, (tm, N_EXPERTS), 1).astype(jnp.float32)
    member = jnp.zeros((tm, N_EXPERTS), jnp.float32)
    for _ in range(TOP_K):
        best = jnp.max(pick, axis=1, keepdims=True)
        which = jnp.min(jnp.where(pick == best, lane, float(N_EXPERTS)), axis=1, keepdims=True)
        hit = lane == which
        pick = jnp.where(hit, -jnp.inf, pick)
        member = jnp.where(hit, 1.0, member)
    chosen = member > 0.0
    picked = jnp.where(chosen, scores, 0.0)
    gate_ref[...] = picked / jnp.sum(picked, axis=1, keepdims=True) * ROUTE_SCALE

    r_i = lax.broadcasted_iota(jnp.int32, (tm, tm), 0)
    c_i = lax.broadcasted_iota(jnp.int32, (tm, tm), 1)
    tri = jnp.where(c_i < r_i, 1.0, 0.0).astype(jnp.bfloat16)
    ahead = _dot(tri, member.astype(jnp.bfloat16))
    cnt = jnp.sum(member, axis=0, keepdims=True)
    start, _ = _segment_bounds(cnt)
    slot_ref[...] = jnp.where(chosen, start + ahead + 1.0, 0.0)
    cnt_ref[...] = jnp.broadcast_to(cnt, cnt_ref.shape)


def _router(x2, router_w, router_b):
    n = x2.shape[0]
    tm = min(ROW_TILE, n)
    tile = lambda c: pl.BlockSpec((tm, c), lambda i: (i, 0))
    whole = lambda a: pl.BlockSpec(a.shape, lambda i: (0, 0))
    rb = router_b.reshape(1, N_EXPERTS)
    return pl.pallas_call(
        _router_body, grid=(n // tm,),
        in_specs=[tile(D_MODEL), whole(router_w), whole(rb)],
        out_specs=[tile(N_EXPERTS), tile(N_EXPERTS), pl.BlockSpec((8, N_EXPERTS), lambda i: (i, 0))],
        out_shape=[jax.ShapeDtypeStruct((n, N_EXPERTS), jnp.float32), jax.ShapeDtypeStruct((n, N_EXPERTS), jnp.float32),
                   jax.ShapeDtypeStruct((n // tm * 8, N_EXPERTS), jnp.float32)],
        compiler_params=_params("arbitrary"), name="moe_router")(x2, router_w, rb)


def _chunk_copy(src, dst, sem):
    return pltpu.make_async_copy(src, dst, sem)


def _chunk(ref, c):
    return ref.at[pl.ds(pl.multiple_of(c * CHUNK, CHUNK), CHUNK), :]


CHUNK_UNROLL = 4


def _chunk_groups(n_chunks):
    return lax.shift_right_logical(n_chunks, CHUNK_UNROLL.bit_length() - 1)


def _expert_of_row(cnt):
    start, end = _segment_bounds(cnt)
    j = lax.broadcasted_iota(jnp.int32, (SORT_ROWS, N_EXPERTS), 0).astype(jnp.float32)
    return jnp.where((j >= start) & (j < end), 1.0, 0.0).astype(jnp.bfloat16)


def _spread(per_expert, owner):
    return _dot_nt(per_expert.astype(jnp.bfloat16), owner)


def _row_matches(slot, owner):
    hi = jnp.floor(slot * (1.0 / SLOT_RADIX))
    lo = slot - hi * SLOT_RADIX
    place = _spread(hi, owner) * SLOT_RADIX + _spread(lo, owner)
    j1 = lax.broadcasted_iota(jnp.int32, place.shape, 1).astype(jnp.float32) + 1.0
    return place == j1


def _dispatch_body(nch_ref, dst_ref, x_ref, slot_ref, cnt_ref, xs_in_ref, xs_ref, sbuf, sem):
    del xs_in_ref
    at = _row_matches(slot_ref[...], _expert_of_row(cnt_ref[0:1, :]))
    perm_t = jnp.where(at, 1.0, 0.0).astype(jnp.bfloat16)
    rows = lax.dot_general(perm_t, x_ref[...].astype(jnp.bfloat16),
                           (((0,), (0,)), ((), ())), preferred_element_type=jnp.float32)
    sbuf[...] = _pack_rows(rows)
    groups = _chunk_groups(nch_ref[pl.program_id(0)])

    def start(g, carry):
        for u in range(CHUNK_UNROLL):
            c = g * CHUNK_UNROLL + u
            _chunk_copy(_chunk(sbuf, c), _chunk(xs_ref, dst_ref[c]), sem).start()
        return carry

    def wait(g, carry):
        for u in range(CHUNK_UNROLL):
            _chunk_copy(_chunk(sbuf, g * CHUNK_UNROLL + u), _chunk(xs_ref, 0), sem).wait()
        return carry

    lax.fori_loop(0, groups, start, 0)
    lax.fori_loop(0, groups, wait, 0)


def _dispatch(n_chunks, dst_flat, x2, slot, cnt, n_rows):
    n = x2.shape[0]
    tm = min(ROW_TILE, n)
    xs0 = jnp.zeros((n_rows, HALF), jnp.uint32)
    grid_spec = pltpu.PrefetchScalarGridSpec(
        num_scalar_prefetch=1, grid=(n // tm,),
        in_specs=[pl.BlockSpec((CHUNK_SLOTS,), lambda i, nc: (i,), memory_space=pltpu.SMEM),
                  pl.BlockSpec((tm, D_MODEL), lambda i, nc: (i, 0)),
                  pl.BlockSpec((tm, N_EXPERTS), lambda i, nc: (i, 0)),
                  pl.BlockSpec((8, N_EXPERTS), lambda i, nc: (i, 0)),
                  pl.BlockSpec(memory_space=pl.ANY)],
        out_specs=pl.BlockSpec(memory_space=pl.ANY),
        scratch_shapes=[pltpu.VMEM((SORT_ROWS, HALF), jnp.uint32), pltpu.SemaphoreType.DMA(())])
    return pl.pallas_call(
        _dispatch_body, grid_spec=grid_spec,
        out_shape=jax.ShapeDtypeStruct((n_rows, HALF), jnp.uint32),
        input_output_aliases={5: 0},
        compiler_params=_params("arbitrary"), name="moe_dispatch")(n_chunks, dst_flat, x2, slot, cnt, xs0)


def _expert_body(blk_exp_ref, n_used_ref, x_ref, wg_ref, wu_ref, wd_ref, o_ref, wgu_s, wd_s):
    i = pl.program_id(0)
    prev = blk_exp_ref[jnp.maximum(i - 1, 0)]

    @pl.when((i == 0) | (blk_exp_ref[i] != prev))
    def _():
        wgu_s[:, :D_EXPERT] = wg_ref[...].astype(_MM)
        wgu_s[:, D_EXPERT:] = wu_ref[...].astype(_MM)
        wd_s[...] = wd_ref[...].astype(_MM)

    @pl.when(i < n_used_ref[0])
    def _():
        gu = _dot(_unpack_rows(x_ref[...]).astype(_MM), wgu_s[...])
        h = jax.nn.silu(gu[:, :D_EXPERT]) * gu[:, D_EXPERT:]
        o_ref[...] = _pack_rows(_dot(h.astype(_MM), wd_s[...]))

    @pl.when(i >= n_used_ref[0])
    def _():
        o_ref[...] = jnp.zeros_like(o_ref)


def _experts(xs, blk_exp, n_used, w_gate, w_up, w_down):
    n_rows = xs.shape[0]
    n_blk = n_rows // EXPERT_BLOCK
    grid_spec = pltpu.PrefetchScalarGridSpec(
        num_scalar_prefetch=2, grid=(n_blk,),
        in_specs=[pl.BlockSpec((EXPERT_BLOCK, HALF), lambda i, be, nu: (i, 0)),
                  pl.BlockSpec((None, D_MODEL, D_EXPERT), lambda i, be, nu: (be[i], 0, 0)),
                  pl.BlockSpec((None, D_MODEL, D_EXPERT), lambda i, be, nu: (be[i], 0, 0)),
                  pl.BlockSpec((None, D_EXPERT, D_MODEL), lambda i, be, nu: (be[i], 0, 0))],
        out_specs=pl.BlockSpec((EXPERT_BLOCK, HALF), lambda i, be, nu: (i, 0)),
        scratch_shapes=[pltpu.VMEM((D_MODEL, 2 * D_EXPERT), _MM), pltpu.VMEM((D_EXPERT, D_MODEL), _MM)])
    return pl.pallas_call(_expert_body, grid_spec=grid_spec,
                          out_shape=jax.ShapeDtypeStruct((n_rows, HALF), jnp.uint32),
                          compiler_params=_params("arbitrary"), name="moe_experts")(
                              blk_exp, n_used, xs, w_gate, w_up, w_down)


def _combine_body(nch_ref, dcur_ref, dnext_ref, x_ref, slot_ref, cnt_ref, gate_ref, ys_ref, sgu_ref, sd_ref,
                  g_ref, b_ref, o_ref, ybuf, sem):
    i = pl.program_id(0)
    slot = i % 2

    @pl.when(i == 0)
    def _():
        ybuf[...] = jnp.zeros_like(ybuf)

    def gather(d_ref, n, s):
        def body(g, carry):
            for u in range(CHUNK_UNROLL):
                c = g * CHUNK_UNROLL + u
                _chunk_copy(_chunk(ys_ref, d_ref[c]), _chunk(ybuf.at[s], c), sem.at[s]).start()
            return carry
        lax.fori_loop(0, _chunk_groups(n), body, 0)

    @pl.when(i == 0)
    def _():
        gather(dcur_ref, nch_ref[0], 0)

    @pl.when(i + 1 < pl.num_programs(0))
    def _():
        gather(dnext_ref, nch_ref[jnp.minimum(i + 1, pl.num_programs(0) - 1)], 1 - slot)

    def wait(g, carry):
        for u in range(CHUNK_UNROLL):
            _chunk_copy(_chunk(ys_ref, 0), _chunk(ybuf.at[slot], g * CHUNK_UNROLL + u), sem.at[slot]).wait()
        return carry

    lax.fori_loop(0, _chunk_groups(nch_ref[i]), wait, 0)

    x = x_ref[...]
    owner = _expert_of_row(cnt_ref[0:1, :])
    at = _row_matches(slot_ref[...], owner)
    gate = gate_ref[...]
    g_hi = gate.astype(jnp.bfloat16).astype(jnp.float32)
    g_lo = gate - g_hi
    p_hi = jnp.where(at, _spread(g_hi, owner), 0.0).astype(jnp.bfloat16)
    p_lo = jnp.where(at, _spread(g_lo, owner), 0.0).astype(jnp.bfloat16)
    y = _unpack_rows(ybuf[slot]).astype(jnp.bfloat16)
    routed = _dot(p_hi, y) + _dot(p_lo, y)
    gu = _dot(x.astype(_MM), sgu_ref[...])
    hs = jax.nn.silu(gu[:, :D_EXPERT]) * gu[:, D_EXPERT:]
    shared = _dot(hs.astype(_MM), sd_ref[...])
    o_ref[...] = _layer_norm(ALPHA * x + (routed + shared), g_ref[...], b_ref[...])


def _combine(n_chunks, dst_flat, x2, slot, cnt, gate, ys, s_gu, s_down, ln_g, ln_b):
    n = x2.shape[0]
    tm = min(ROW_TILE, n)
    nt = n // tm
    consts = [s_gu, s_down, ln_g.reshape(1, D_MODEL), ln_b.reshape(1, D_MODEL)]
    grid_spec = pltpu.PrefetchScalarGridSpec(
        num_scalar_prefetch=1, grid=(nt,),
        in_specs=[pl.BlockSpec((CHUNK_SLOTS,), lambda i, nc: (i,), memory_space=pltpu.SMEM),
                  pl.BlockSpec((CHUNK_SLOTS,), lambda i, nc: (jnp.minimum(i + 1, nt - 1),), memory_space=pltpu.SMEM),
                  pl.BlockSpec((tm, D_MODEL), lambda i, nc: (i, 0)),
                  pl.BlockSpec((tm, N_EXPERTS), lambda i, nc: (i, 0)),
                  pl.BlockSpec((8, N_EXPERTS), lambda i, nc: (i, 0)),
                  pl.BlockSpec((tm, N_EXPERTS), lambda i, nc: (i, 0)),
                  pl.BlockSpec(memory_space=pl.ANY)]
        + [pl.BlockSpec(a.shape, lambda i, nc: (0, 0)) for a in consts],
        out_specs=pl.BlockSpec((tm, D_MODEL), lambda i, nc: (i, 0)),
        scratch_shapes=[pltpu.VMEM((2, SORT_ROWS, HALF), jnp.uint32), pltpu.SemaphoreType.DMA((2,))])
    return pl.pallas_call(
        _combine_body, grid_spec=grid_spec,
        out_shape=jax.ShapeDtypeStruct((n, D_MODEL), jnp.float32),
        compiler_params=_params("arbitrary"), name="moe_combine")(
            n_chunks, dst_flat, dst_flat, x2, slot, cnt, gate, ys, *consts)


def _moe_ln(x2, router_w, router_b, w_gate, w_up, w_down, s_gate, s_up, s_down, ln_g, ln_b):
    n = x2.shape[0]
    tm = min(ROW_TILE, n)
    nt = n // tm
    gate, slot, cnt8 = _router(x2, router_w, router_b)
    cnt = cnt8[::8].astype(jnp.int32)
    seg = (cnt + CHUNK - 1) // CHUNK * CHUNK
    loc_end = jnp.cumsum(seg, axis=1)
    loc_start = loc_end - seg
    exp_rows = jnp.sum(seg, axis=0)
    padded = (exp_rows + EXPERT_BLOCK - 1) // EXPERT_BLOCK * EXPERT_BLOCK
    pad_end = jnp.cumsum(padded)
    seg_dst = (pad_end - padded)[None, :] + jnp.cumsum(seg, axis=0) - seg
    max_rows = n * TOP_K + nt * N_EXPERTS * (CHUNK - 1)
    n_blk = -(-max_rows // EXPERT_BLOCK) + N_EXPERTS + 1
    n_real = (loc_end[:, -1] // CHUNK).astype(jnp.int32)
    n_chunks = (n_real + CHUNK_UNROLL - 1) // CHUNK_UNROLL * CHUNK_UNROLL
    c_idx = jnp.arange(SORT_ROWS // CHUNK, dtype=jnp.int32)
    c_row = c_idx[None, :, None] * CHUNK
    in_seg = (loc_start[:, None, :] <= c_row) & (c_row < loc_end[:, None, :])
    shift = jnp.sum(jnp.where(in_seg, (seg_dst - loc_start)[:, None, :], 0), axis=2)
    dst = (shift + c_idx[None, :] * CHUNK) // CHUNK
    spare = (n_blk - 1) * (EXPERT_BLOCK // CHUNK)
    dst = jnp.where(c_idx[None, :] < n_real[:, None], dst, spare + c_idx[None, :] - n_real[:, None])
    dst_flat = jnp.pad(dst, ((0, 0), (0, CHUNK_SLOTS - dst.shape[1]))).reshape(nt * CHUNK_SLOTS).astype(jnp.int32)
    blk_row = jnp.arange(n_blk, dtype=jnp.int32) * EXPERT_BLOCK
    blk_exp = jnp.minimum(jnp.sum((pad_end[None, :] <= blk_row[:, None]).astype(jnp.int32), axis=1), N_EXPERTS - 1)
    n_used = (pad_end[-1:] // EXPERT_BLOCK).astype(jnp.int32)
    xs = _dispatch(n_chunks, dst_flat, x2, slot, cnt8, n_blk * EXPERT_BLOCK)
    ys = _experts(xs, blk_exp, n_used, w_gate, w_up, w_down)
    s_gu = jnp.concatenate([s_gate, s_up], axis=-1).astype(_MM)
    return _combine(n_chunks, dst_flat, x2, slot, cnt8, gate, ys, s_gu, s_down.astype(_MM), ln_g, ln_b)


def kernel(x, ev_w_in, ev_b_in, s5_lam_re, s5_lam_im, s5_log_dt, s5_b_re, s5_b_im, s5_c_re, s5_c_im, s5_d, s5_w_glu, s5_b_glu, cv_w, cv_b, cv_ln_g, cv_ln_b, ev_w_out, od_w_in, cmp_w_k, cmp_w_v, cmp_pe_k, cmp_pe_v, od_w_out, ln1_g, ln1_b, ln2_g, ln2_b, router_w, router_b, ex_w_gate, ex_w_up, ex_w_down, sh_w_gate, sh_w_up, sh_w_down):
    bsz, seq, _ = x.shape
    h = x.reshape(bsz * seq, D_MODEL)
    for layer in range(DEPTH):
        i = layer // 2
        if layer % 2 == 0:
            mix = _even_mixer(h, bsz, seq, ev_w_in[i], ev_b_in[i], s5_lam_re[i], s5_lam_im[i], s5_log_dt[i],
                              s5_b_re[i], s5_b_im[i], s5_c_re[i], s5_c_im[i], s5_d[i].reshape(-1),
                              s5_w_glu[i], s5_b_glu[i], cv_w[i], cv_b[i], cv_ln_g[i], cv_ln_b[i])
            (h,) = _row_call(_proj_res_ln_body, [mix, h],
                             [ev_w_out[i].astype(_MM), ln1_g[layer].reshape(1, D_MODEL),
                              ln1_b[layer].reshape(1, D_MODEL)],
                             [(D_MODEL, jnp.float32)], "out_proj_ln")
        else:
            mix_t = _nsa_mixer(h, bsz, seq, od_w_in[i], cmp_w_k[i], cmp_w_v[i], cmp_pe_k[i], cmp_pe_v[i])
            h = _proj_t_res_ln(mix_t, h, od_w_out[i], ln1_g[layer], ln1_b[layer])
        h = _moe_ln(h, router_w[layer], router_b[layer], ex_w_gate[layer], ex_w_up[layer], ex_w_down[layer],
                    sh_w_gate[layer], sh_w_up[layer], sh_w_down[layer], ln2_g[layer], ln2_b[layer])
    return h.reshape(bsz, seq, D_MODEL)
```
